```python
import math
import jax, jax.numpy as jnp
from jax import lax
import numpy as np

D_MODEL = 1024
BATCH = 4
SEQ = 8192
DEPTH = 2

N_ATT_HEADS = 4
QK_DIM = 64
V_DIM = 2 * QK_DIM
ATT_QK_WIDTH = N_ATT_HEADS * 2 * QK_DIM
ATT_WIDTH = N_ATT_HEADS * V_DIM
POOL_WINDOWS = (2, 4, 8, 16)
N_POOL_GROUPS = len(POOL_WINDOWS)
POOL_WIDTH = D_MODEL - ATT_WIDTH
POOL_GROUP_DIM = POOL_WIDTH // N_POOL_GROUPS
MIX_WIDTH = ATT_WIDTH + POOL_WIDTH
IN_WIDTH = 2 * ATT_QK_WIDTH + ATT_WIDTH + POOL_WIDTH
N_BUCKETS = 32
MAX_DISTANCE = 128
Q_BLOCK = 128
N_EXPERTS = 32
TOP_K = 4
D_FF = D_MODEL
SWIGLU_LIMIT = 7.0
SWIGLU_ALPHA = 1.702
EXPERT_BLOCK = 128
EPS = 1e-5

kernel_name = "hybrid_diffattn_pool_moe_encoder"


def rmsnorm(x, g):
    xf = x.astype(jnp.float32)
    y = xf * lax.rsqrt(jnp.mean(xf * xf, axis=-1, keepdims=True) + EPS)
    return (y * g.astype(jnp.float32)).astype(x.dtype)


def rel_bucket(rel):
    nb = N_BUCKETS // 2
    max_exact = nb // 2
    ret = jnp.where(rel > 0, nb, 0)
    n = jnp.abs(rel)
    n_f = jnp.maximum(n, 1).astype(jnp.float32)
    large = max_exact + (jnp.log(n_f / max_exact) / math.log(MAX_DISTANCE / max_exact)
                         * (nb - max_exact)).astype(jnp.int32)
    large = jnp.minimum(large, nb - 1)
    return ret + jnp.where(n < max_exact, n, large)


def diff_attention(q, k, v, rel_table, lam, sub_g, lam_init):
    B, S = q.shape[0], q.shape[1]
    scale = QK_DIM ** -0.5
    q = jnp.transpose(q, (0, 2, 3, 1, 4)) * scale
    k = jnp.transpose(k, (0, 2, 3, 1, 4))
    v = jnp.transpose(v, (0, 2, 1, 3))
    kpos = jnp.arange(S, dtype=jnp.int32)

    def block(i):
        start = i * Q_BLOCK
        qb = lax.dynamic_slice_in_dim(q, start, Q_BLOCK, axis=3)
        logits = jnp.einsum('bhmqd,bhmkd->bhmqk', qb, k).astype(jnp.float32)
        qpos = start + jnp.arange(Q_BLOCK, dtype=jnp.int32)
        bias = rel_table.astype(jnp.float32)[rel_bucket(kpos[None, :] - qpos[:, None])]
        bias = jnp.transpose(bias, (2, 0, 1))
        p = jax.nn.softmax(logits + bias[None, :, None], axis=-1)
        a = p[:, :, 0] - lam * p[:, :, 1]
        return jnp.einsum('bhqk,bhkd->bhqd', a.astype(v.dtype), v)

    o = lax.map(block, jnp.arange(S // Q_BLOCK))
    o = jnp.transpose(o, (1, 0, 3, 2, 4)).reshape(B, S, N_ATT_HEADS, V_DIM)
    o = rmsnorm(o, sub_g) * (1.0 - lam_init)
    return o.reshape(B, S, ATT_WIDTH)


def pool_mixer(p, w_pool, pool_scale):
    B, S, C = p.shape
    pf = p.astype(jnp.float32)
    c = jnp.concatenate([jnp.zeros((B, 1, C), jnp.float32), jnp.cumsum(pf, axis=1)], axis=1)
    t = jnp.arange(S, dtype=jnp.int32)
    outs = []
    for g, win in enumerate(POOL_WINDOWS):
        lo = jnp.maximum(t - win // 2, 0)
        hi = jnp.minimum(t - win // 2 + win - 1, S - 1)
        sl = slice(g * POOL_GROUP_DIM, (g + 1) * POOL_GROUP_DIM)
        cg = c[:, :, sl]
        s = jnp.take(cg, hi + 1, axis=1) - jnp.take(cg, lo, axis=1)
        mean = s / (hi - lo + 1).astype(jnp.float32)[None, :, None]
        d = (mean - pf[:, :, sl]).astype(p.dtype)
        outs.append(d @ w_pool[g])
    return jnp.concatenate(outs, axis=-1) * pool_scale


def moe(h, w_router, b_router, w_gate, b_gate, w_up, b_up, w_down, b_down):
    B, S, D = h.shape
    N = B * S
    hf = h.reshape(N, D)
    logits = (hf @ w_router + b_router).astype(jnp.float32)
    topv, topi = lax.top_k(logits, TOP_K)
    gates = jax.nn.softmax(topv, axis=-1)
    A = N * TOP_K
    flat_e = topi.reshape(A)
    flat_tok = jnp.repeat(jnp.arange(N, dtype=jnp.int32), TOP_K)
    flat_g = gates.reshape(A)
    order = jnp.argsort(flat_e)
    sorted_e = flat_e[order]
    sorted_tok = flat_tok[order]
    sorted_g = flat_g[order]
    counts = jnp.zeros((N_EXPERTS,), jnp.int32).at[flat_e].add(1)
    starts = jnp.cumsum(counts) - counts
    padded = ((counts + EXPERT_BLOCK - 1) // EXPERT_BLOCK) * EXPERT_BLOCK
    padded_ends = jnp.cumsum(padded)
    padded_starts = padded_ends - padded
    rank = jnp.arange(A, dtype=jnp.int32) - starts[sorted_e]
    dest = padded_starts[sorted_e] + rank
    NB = -(-(A + N_EXPERTS * (EXPERT_BLOCK - 1)) // EXPERT_BLOCK)
    P = NB * EXPERT_BLOCK
    row_tok = jnp.zeros((P,), jnp.int32).at[dest].set(sorted_tok)
    row_gate = jnp.zeros((P,), jnp.float32).at[dest].set(sorted_g)
    block_start = jnp.arange(NB, dtype=jnp.int32) * EXPERT_BLOCK
    block_expert = jnp.clip(jnp.searchsorted(padded_ends, block_start, side='right'), 0, N_EXPERTS - 1)

    def expert_block(args):
        e, tok, gt = args
        xb = hf[tok]
        g = xb @ w_gate[e] + b_gate[e]
        u = xb @ w_up[e] + b_up[e]
        g = jnp.minimum(g, SWIGLU_LIMIT)
        u = jnp.clip(u, -SWIGLU_LIMIT, SWIGLU_LIMIT)
        act = (u + 1.0) * (g * jax.nn.sigmoid(g * SWIGLU_ALPHA))
        y = act @ w_down[e] + b_down[e]
        return y * gt[:, None].astype(y.dtype)

    y = lax.map(expert_block, (block_expert, row_tok.reshape(NB, EXPERT_BLOCK),
                               row_gate.reshape(NB, EXPERT_BLOCK)))
    out = jnp.zeros((N, D), h.dtype).at[row_tok].add(y.reshape(P, D).astype(h.dtype))
    return out.reshape(B, S, D)


def setup_inputs(seed: int = 0) -> dict:
    key = jax.random.key(seed)
    ks = jax.random.split(key, 24)
    nrm = lambda k, shape, s: jax.random.normal(k, shape, jnp.float32) * s
    L, D, E, F = DEPTH, D_MODEL, N_EXPERTS, D_FF
    return {
        "x": nrm(ks[0], (BATCH, SEQ, D), 1.0),
        "rel_table": nrm(ks[1], (N_BUCKETS, N_ATT_HEADS), 0.5),
        "norm1": 1.0 + nrm(ks[2], (L, D), 0.05),
        "w_in": nrm(ks[3], (L, D, IN_WIDTH), D ** -0.5),
        "lambda_q1": nrm(ks[4], (L, QK_DIM), 0.1),
        "lambda_k1": nrm(ks[5], (L, QK_DIM), 0.1),
        "lambda_q2": nrm(ks[6], (L, QK_DIM), 0.1),
        "lambda_k2": nrm(ks[7], (L, QK_DIM), 0.1),
        "subln_g": 1.0 + nrm(ks[8], (L, V_DIM), 0.05),
        "w_pool": nrm(ks[9], (L, N_POOL_GROUPS, POOL_GROUP_DIM, POOL_GROUP_DIM), POOL_GROUP_DIM ** -0.5),
        "pool_scale": 1.0 + nrm(ks[10], (L, POOL_WIDTH), 0.1),
        "w_out": nrm(ks[11], (L, MIX_WIDTH, D), MIX_WIDTH ** -0.5),
        "norm2": 1.0 + nrm(ks[12], (L, D), 0.05),
        "w_router": nrm(ks[13], (L, D, E), D ** -0.5),
        "b_router": nrm(ks[14], (L, E), 0.01),
        "w_gate": nrm(ks[15], (L, E, D, F), D ** -0.5),
        "b_gate": nrm(ks[16], (L, E, F), 0.01),
        "w_up": nrm(ks[17], (L, E, D, F), D ** -0.5),
        "b_up": nrm(ks[18], (L, E, F), 0.01),
        "w_down": nrm(ks[19], (L, E, F, D), F ** -0.5),
        "b_down": nrm(ks[20], (L, E, D), 0.01),
        "final_norm": 1.0 + nrm(ks[21], (D,), 0.05),
    }


def reference(x, rel_table, norm1, w_in, lambda_q1, lambda_k1, lambda_q2, lambda_k2,
              subln_g, w_pool, pool_scale, w_out, norm2, w_router, b_router,
              w_gate, b_gate, w_up, b_up, w_down, b_down, final_norm):
    B, S, D = x.shape
    for l in range(DEPTH):
        h = rmsnorm(x, norm1[l])
        proj = h @ w_in[l]
        q = proj[..., :ATT_QK_WIDTH].reshape(B, S, N_ATT_HEADS, 2, QK_DIM)
        k = proj[..., ATT_QK_WIDTH:2 * ATT_QK_WIDTH].reshape(B, S, N_ATT_HEADS, 2, QK_DIM)
        v = proj[..., 2 * ATT_QK_WIDTH:2 * ATT_QK_WIDTH + ATT_WIDTH].reshape(B, S, N_ATT_HEADS, V_DIM)
        p = proj[..., 2 * ATT_QK_WIDTH + ATT_WIDTH:]
        lam_init = 0.8 - 0.6 * math.exp(-0.3 * l)
        lam = (jnp.exp(jnp.sum(lambda_q1[l].astype(jnp.float32) * lambda_k1[l].astype(jnp.float32)))
               - jnp.exp(jnp.sum(lambda_q2[l].astype(jnp.float32) * lambda_k2[l].astype(jnp.float32)))
               + lam_init)
        att = diff_attention(q, k, v, rel_table, lam, subln_g[l], lam_init)
        pool = pool_mixer(p, w_pool[l], pool_scale[l])
        x = x + jnp.concatenate([att, pool], axis=-1) @ w_out[l]
        x = x + moe(rmsnorm(x, norm2[l]), w_router[l], b_router[l], w_gate[l], b_gate[l],
                    w_up[l], b_up[l], w_down[l], b_down[l])
    return rmsnorm(x, final_norm)
```

```python
import functools
import math

import jax
import jax.numpy as jnp
from jax import lax
from jax.experimental import pallas as pl
from jax.experimental.pallas import tpu as pltpu

D_MODEL = 1024
N_HEADS = 4
QK_DIM = 64
V_DIM = 128
QK_WIDTH = N_HEADS * 2 * QK_DIM
ATT_WIDTH = N_HEADS * V_DIM
POOL_WINDOWS = (2, 4, 8, 16)
POOL_GROUP = 128
POOL_WIDTH = 512
IN_WIDTH = 2 * QK_WIDTH + ATT_WIDTH + POOL_WIDTH
N_BUCKETS = 32
MAX_DISTANCE = 128
N_EXPERTS = 32
TOP_K = 4
SWIGLU_LIMIT = 7.0
SWIGLU_ALPHA = 1.702
EPS = 1e-5

LANES = 128
SUBLANES = 8
ROW_CHUNKS = D_MODEL // LANES
VMEM_LIMIT = 56 * 1024 * 1024

TOK_TILE = 512
ATT_TILE = 512
HALO = 16
EXPERT_ROWS = 256
MOVE_TILE = 256

_NT = (((1,), (1,)), ((), ()))


def _rms(x, g):
    return x * lax.rsqrt(jnp.mean(x * x, axis=-1, keepdims=True) + EPS) * g


def _bias_kernel(tab_ref, out_ref, *, tile):
    h = pl.program_id(0)
    d = pl.program_id(1)
    r = lax.broadcasted_iota(jnp.int32, (tile, tile), 0)
    c = lax.broadcasted_iota(jnp.int32, (tile, tile), 1)
    rel = (d - 1) * tile + c - r
    nb = N_BUCKETS // 2
    max_exact = nb // 2
    ret = jnp.where(rel > 0, nb, 0)
    n = jnp.abs(rel)
    n_f = jnp.maximum(n, 1).astype(jnp.float32)
    large = max_exact + (jnp.log(n_f / max_exact) / math.log(MAX_DISTANCE / max_exact)
                         * (nb - max_exact)).astype(jnp.int32)
    large = jnp.minimum(large, nb - 1)
    bucket = ret + jnp.where(n < max_exact, n, large)
    acc = jnp.zeros((tile, tile), jnp.float32)
    for b in range(N_BUCKETS):
        acc = jnp.where(bucket == b, tab_ref[b * N_HEADS + h], acc)
    out_ref[0, 0] = acc


def _bias_tiles(rel_table, tile):
    return pl.pallas_call(
        functools.partial(_bias_kernel, tile=tile),
        out_shape=jax.ShapeDtypeStruct((N_HEADS, 3, tile, tile), jnp.float32),
        grid=(N_HEADS, 3),
        in_specs=[pl.BlockSpec(memory_space=pltpu.SMEM)],
        out_specs=pl.BlockSpec((1, 1, tile, tile), lambda h, d: (h, d, 0, 0)),
        name="rel_bias_tiles",
    )(rel_table.reshape(-1))


def _norm_proj_kernel(x_ref, g_ref, w_ref, o_ref):
    h = _rms(x_ref[...], g_ref[...]).astype(jnp.bfloat16)
    o_ref[...] = jnp.dot(h, w_ref[...], preferred_element_type=jnp.float32).astype(o_ref.dtype)


def _norm_proj(x2d, g, w_bf16):
    n = x2d.shape[0]
    return pl.pallas_call(
        _norm_proj_kernel,
        out_shape=jax.ShapeDtypeStruct((n, IN_WIDTH), jnp.bfloat16),
        grid=(n // TOK_TILE,),
        in_specs=[pl.BlockSpec((TOK_TILE, D_MODEL), lambda i: (i, 0)),
                  pl.BlockSpec((1, D_MODEL), lambda i: (0, 0)),
                  pl.BlockSpec((D_MODEL, IN_WIDTH), lambda i: (0, 0))],
        out_specs=pl.BlockSpec((TOK_TILE, IN_WIDTH), lambda i: (i, 0)),
        compiler_params=pltpu.CompilerParams(dimension_semantics=("parallel",),
                                             vmem_limit_bytes=VMEM_LIMIT),
        name="norm_in_proj",
    )(x2d, g.reshape(1, D_MODEL), w_bf16)


def _attn_kernel(far_ref, q_ref, k_ref, v_ref, bias_ref, lq1_ref, lk1_ref, lq2_ref, lk2_ref, sg_ref,
                 o_ref, m_ref, l_ref, acc_ref, *, tile, n_kv, lam_init):
    h = pl.program_id(1)
    qi = pl.program_id(2)
    q = q_ref[...]
    lane = lax.broadcasted_iota(jnp.int32, q.shape, 1)
    zero = jnp.zeros_like(q)
    qs = jnp.concatenate([jnp.where(lane < QK_DIM, q, zero), jnp.where(lane >= QK_DIM, q, zero)], axis=0)

    m_ref[...] = jnp.full(m_ref.shape, -jnp.inf, jnp.float32)
    l_ref[...] = jnp.zeros(l_ref.shape, jnp.float32)
    acc_ref[...] = jnp.zeros(acc_ref.shape, jnp.float32)

    def step(ki, tile_bias, const_bias):
        start = pl.multiple_of(ki * tile, tile)
        k = k_ref[pl.ds(start, tile), :]
        v = v_ref[pl.ds(start, tile), :]
        s = lax.dot_general(qs, k, _NT, preferred_element_type=jnp.float32)
        if tile_bias is not None:
            s = (s.reshape(2, tile, tile) + tile_bias[None]).reshape(2 * tile, tile)
        m_prev = m_ref[...]
        m_new = jnp.maximum(m_prev, jnp.max(s, axis=1, keepdims=True) + const_bias)
        alpha = jnp.exp(m_prev - m_new)
        p = jnp.exp(s - (m_new - const_bias))
        l_ref[...] = alpha * l_ref[...] + jnp.sum(p, axis=1, keepdims=True)
        acc_ref[...] = alpha * acc_ref[...] + jnp.dot(p.astype(jnp.bfloat16), v,
                                                      preferred_element_type=jnp.float32)
        m_ref[...] = m_new

    near_lo = jnp.maximum(qi - 1, 0)
    near_hi = jnp.minimum(qi + 2, n_kv)
    c_lo = far_ref[2 * h]
    c_hi = far_ref[2 * h + 1]

    def far_lo(ki, c):
        step(ki, None, c_lo)
        return c

    def near(ki, c):
        step(ki, bias_ref[0, ki - qi + 1], 0.0)
        return c

    def far_hi(ki, c):
        step(ki, None, c_hi)
        return c

    lax.fori_loop(0, near_lo, far_lo, 0)
    lax.fori_loop(near_lo, near_hi, near, 0)
    lax.fori_loop(near_hi, n_kv, far_hi, 0)

    o = acc_ref[...] / l_ref[...]
    lam = (jnp.exp(jnp.sum(lq1_ref[...] * lk1_ref[...], axis=1, keepdims=True))
           - jnp.exp(jnp.sum(lq2_ref[...] * lk2_ref[...], axis=1, keepdims=True)) + lam_init)
    o = o[:tile] - lam * o[tile:]
    o_ref[...] = (_rms(o, sg_ref[...]) * (1.0 - lam_init)).astype(o_ref.dtype)


def _attention(proj, bias, far, lq1, lk1, lq2, lk2, sub_g, *, batch, seq, lam_init):
    t = ATT_TILE
    nq = seq // t
    kcol = QK_WIDTH // LANES
    vcol = 2 * QK_WIDTH // LANES
    vec = lambda width: pl.BlockSpec((1, width), lambda b, h, qi: (0, 0))
    return pl.pallas_call(
        functools.partial(_attn_kernel, tile=t, n_kv=nq, lam_init=lam_init),
        out_shape=jax.ShapeDtypeStruct((batch * seq, ATT_WIDTH), jnp.bfloat16),
        grid=(batch, N_HEADS, nq),
        in_specs=[pl.BlockSpec(memory_space=pltpu.SMEM),
                  pl.BlockSpec((t, LANES), lambda b, h, qi: (b * nq + qi, h)),
                  pl.BlockSpec((seq, LANES), lambda b, h, qi: (b, kcol + h)),
                  pl.BlockSpec((seq, LANES), lambda b, h, qi: (b, vcol + h)),
                  pl.BlockSpec((1, 3, t, t), lambda b, h, qi: (h, 0, 0, 0)),
                  vec(QK_DIM), vec(QK_DIM), vec(QK_DIM), vec(QK_DIM), vec(V_DIM)],
        out_specs=pl.BlockSpec((t, V_DIM), lambda b, h, qi: (b * nq + qi, h)),
        scratch_shapes=[pltpu.VMEM((2 * t, 1), jnp.float32),
                        pltpu.VMEM((2 * t, 1), jnp.float32),
                        pltpu.VMEM((2 * t, V_DIM), jnp.float32)],
        compiler_params=pltpu.CompilerParams(dimension_semantics=("parallel", "parallel", "parallel"),
                                             vmem_limit_bytes=VMEM_LIMIT),
        name="diff_attention",
    )(far, proj, proj, proj, bias, lq1.reshape(1, -1), lk1.reshape(1, -1), lq2.reshape(1, -1),
      lk2.reshape(1, -1), sub_g.reshape(1, -1))


def _mix_route_kernel(att_ref, p_ref, prev_ref, next_ref, x_ref, wpool_ref, pscale_ref, wout_ref, g2_ref,
                      wr_ref, br_ref, tri_ref,
                      x1_ref, hflat_ref, topi_ref, rank_ref, gate_ref, cnt_out_ref, cnt_ref,
                      *, tile, seq):
    i = pl.program_id(0)
    tiles_per_seq = seq // tile
    si = i % tiles_per_seq

    @pl.when(i == 0)
    def _():
        cnt_ref[...] = jnp.zeros(cnt_ref.shape, jnp.float32)

    pc = p_ref[...].astype(jnp.float32)
    pp = jnp.where(si == 0, 0.0, prev_ref[...].astype(jnp.float32))
    pn = jnp.where(si == tiles_per_seq - 1, 0.0, next_ref[...].astype(jnp.float32))
    pe = jnp.concatenate([pp, pc, pn], axis=0)
    n_ext = tile + 2 * HALO
    pos = si * tile + lax.broadcasted_iota(jnp.int32, (tile, 1), 0)
    pooled = []
    for g, win in enumerate(POOL_WINDOWS):
        xg = pe[:, g * POOL_GROUP:(g + 1) * POOL_GROUP]
        w = xg + pltpu.roll(xg, 1, 0)
        half = 1
        while 2 * half < win:
            w = pltpu.roll(w, half, 0) + pltpu.roll(w, n_ext - half, 0)
            half *= 2
        lo = jnp.maximum(pos - win // 2, 0)
        hi = jnp.minimum(pos - win // 2 + win - 1, seq - 1)
        mean = w[HALO:HALO + tile] / (hi - lo + 1).astype(jnp.float32)
        dg = (mean - pc[:, g * POOL_GROUP:(g + 1) * POOL_GROUP]).astype(jnp.bfloat16)
        pooled.append(jnp.dot(dg, wpool_ref[g], preferred_element_type=jnp.float32))
    pool = jnp.concatenate(pooled, axis=1) * pscale_ref[...]

    mix = jnp.concatenate([att_ref[...], pool.astype(jnp.bfloat16)], axis=1)
    x1 = x_ref[...] + jnp.dot(mix, wout_ref[...], preferred_element_type=jnp.float32)
    x1_ref[...] = x1

    h2 = _rms(x1, g2_ref[...])
    for c in range(ROW_CHUNKS):
        hflat_ref[pl.ds(c, tile, stride=ROW_CHUNKS), :] = h2[:, c * LANES:(c + 1) * LANES]

    logits = lax.dot_general(wr_ref[...], h2.astype(jnp.bfloat16), _NT,
                             preferred_element_type=jnp.float32) + br_ref[...]
    eio = lax.broadcasted_iota(jnp.int32, logits.shape, 0)
    work = logits
    sel = jnp.zeros(logits.shape, jnp.float32)
    top_v, top_i, hot = [], [], []
    for _ in range(TOP_K):
        mx = jnp.max(work, axis=0, keepdims=True)
        idx = jnp.min(jnp.where(work == mx, eio, N_EXPERTS), axis=0, keepdims=True)
        oh = eio == idx
        top_v.append(mx)
        top_i.append(idx)
        hot.append(oh)
        work = jnp.where(oh, -jnp.inf, work)
        sel = sel + oh.astype(jnp.float32)
    ex = [jnp.exp(v - top_v[0]) for v in top_v]
    den = ex[0] + ex[1] + ex[2] + ex[3]
    ahead = jnp.dot(sel.astype(jnp.bfloat16), tri_ref[...], preferred_element_type=jnp.float32) + cnt_ref[...]
    for j in range(TOP_K):
        topi_ref[j:j + 1, :] = top_i[j]
        gate_ref[j:j + 1, :] = ex[j] / den
        rank_ref[j:j + 1, :] = jnp.sum(jnp.where(hot[j], ahead, 0.0), axis=0, keepdims=True).astype(jnp.int32)
    cnt_ref[...] = cnt_ref[...] + jnp.sum(sel, axis=1, keepdims=True)
    cnt_out_ref[...] = jnp.broadcast_to(cnt_ref[...], cnt_out_ref.shape)


def _mix_route(att, proj, x2d, wpool, pscale, wout, g2, wr_t, br, tri, *, seq):
    n = x2d.shape[0]
    t = TOK_TILE
    nt = n // t
    pcol = (2 * QK_WIDTH + ATT_WIDTH) // POOL_WIDTH
    hb = t // HALO
    last_halo = n // HALO - 1
    const = lambda shape: pl.BlockSpec(shape, lambda i: (0,) * len(shape))
    out_shape = (jax.ShapeDtypeStruct((n, D_MODEL), jnp.float32),
                 jax.ShapeDtypeStruct((n * ROW_CHUNKS, LANES), jnp.float32),
                 jax.ShapeDtypeStruct((TOP_K, n), jnp.int32),
                 jax.ShapeDtypeStruct((TOP_K, n), jnp.int32),
                 jax.ShapeDtypeStruct((TOP_K, n), jnp.float32),
                 jax.ShapeDtypeStruct((N_EXPERTS, LANES), jnp.float32))
    return pl.pallas_call(
        functools.partial(_mix_route_kernel, tile=t, seq=seq),
        out_shape=out_shape,
        grid=(nt,),
        in_specs=[pl.BlockSpec((t, ATT_WIDTH), lambda i: (i, 0)),
                  pl.BlockSpec((t, POOL_WIDTH), lambda i: (i, pcol)),
                  pl.BlockSpec((HALO, POOL_WIDTH), lambda i: (jnp.maximum(i * hb - 1, 0), pcol)),
                  pl.BlockSpec((HALO, POOL_WIDTH), lambda i: (jnp.minimum((i + 1) * hb, last_halo), pcol)),
                  pl.BlockSpec((t, D_MODEL), lambda i: (i, 0)),
                  const((len(POOL_WINDOWS), POOL_GROUP, POOL_GROUP)),
                  const((1, POOL_WIDTH)),
                  const((D_MODEL, D_MODEL)),
                  const((1, D_MODEL)),
                  const((N_EXPERTS, D_MODEL)),
                  const((N_EXPERTS, 1)),
                  const((t, t))],
        out_specs=(pl.BlockSpec((t, D_MODEL), lambda i: (i, 0)),
                   pl.BlockSpec((t * ROW_CHUNKS, LANES), lambda i: (i, 0)),
                   pl.BlockSpec((TOP_K, t), lambda i: (0, i)),
                   pl.BlockSpec((TOP_K, t), lambda i: (0, i)),
                   pl.BlockSpec((TOP_K, t), lambda i: (0, i)),
                   pl.BlockSpec((N_EXPERTS, LANES), lambda i: (0, 0))),
        scratch_shapes=[pltpu.VMEM((N_EXPERTS, 1), jnp.float32)],
        compiler_params=pltpu.CompilerParams(dimension_semantics=("arbitrary",),
                                             vmem_limit_bytes=VMEM_LIMIT),
        name="mix_out_route",
    )(att, proj, proj, proj, x2d, wpool, pscale.reshape(1, -1), wout, g2.reshape(1, -1), wr_t,
      br.reshape(-1, 1), tri)


def _row(ref, r):
    return ref.at[pl.ds(pl.multiple_of(r * ROW_CHUNKS, ROW_CHUNKS), ROW_CHUNKS), :]


def _dispatch_kernel(dest_hbm, h_ref, xs_in, xs_out, idx_ref, idx_sem, row_sem, *, tile):
    del xs_in
    i = pl.program_id(0)
    per = tile * TOP_K
    idx_copy = pltpu.make_async_copy(dest_hbm.at[pl.ds(pl.multiple_of(i * per, per), per)], idx_ref, idx_sem)
    idx_copy.start()
    idx_copy.wait()

    def body(t, c):
        src = _row(h_ref, t)
        for j in range(TOP_K):
            pltpu.make_async_copy(src, _row(xs_out, idx_ref[t * TOP_K + j]), row_sem).start()
        return c

    lax.fori_loop(0, tile, body, 0)
    for _ in range(TOP_K):
        pltpu.make_async_copy(h_ref, xs_out.at[pl.ds(0, tile * ROW_CHUNKS), :], row_sem).wait()


def _dispatch(dest_flat, hflat, xs_zero):
    t = MOVE_TILE
    per = t * TOP_K
    nt = dest_flat.shape[0] // per
    return pl.pallas_call(
        functools.partial(_dispatch_kernel, tile=t),
        out_shape=jax.ShapeDtypeStruct(xs_zero.shape, xs_zero.dtype),
        grid=(nt,),
        in_specs=[pl.BlockSpec(memory_space=pl.ANY),
                  pl.BlockSpec((t * ROW_CHUNKS, LANES), lambda i: (i, 0)),
                  pl.BlockSpec(memory_space=pl.ANY)],
        out_specs=pl.BlockSpec(memory_space=pl.ANY),
        scratch_shapes=[pltpu.SMEM((per,), jnp.int32),
                        pltpu.SemaphoreType.DMA,
                        pltpu.SemaphoreType.DMA],
        input_output_aliases={2: 0},
        compiler_params=pltpu.CompilerParams(dimension_semantics=("arbitrary",),
                                             vmem_limit_bytes=VMEM_LIMIT),
        name="moe_dispatch",
    )(dest_flat, hflat, xs_zero)


def _expert_kernel(be_ref, used_ref, x_ref, wg_ref, bg_ref, wu_ref, bu_ref, wd_ref, bd_ref, y_ref, *, rows):
    del be_ref
    live = pl.program_id(0) < used_ref[0]

    @pl.when(jnp.logical_not(live))
    def _():
        y_ref[...] = jnp.zeros(y_ref.shape, y_ref.dtype)

    @pl.when(live)
    def _():
        x = jnp.concatenate([x_ref[pl.ds(c, rows, stride=ROW_CHUNKS), :] for c in range(ROW_CHUNKS)],
                            axis=1).astype(jnp.bfloat16)
        g = jnp.dot(x, wg_ref[0], preferred_element_type=jnp.float32) + bg_ref[0]
        u = jnp.dot(x, wu_ref[0], preferred_element_type=jnp.float32) + bu_ref[0]
        g = jnp.minimum(g, SWIGLU_LIMIT)
        u = jnp.clip(u, -SWIGLU_LIMIT, SWIGLU_LIMIT)
        act = (u + 1.0) * (g * (1.0 / (1.0 + jnp.exp(-SWIGLU_ALPHA * g))))
        y = jnp.dot(act.astype(jnp.bfloat16), wd_ref[0], preferred_element_type=jnp.float32) + bd_ref[0]
        for c in range(ROW_CHUNKS):
            y_ref[pl.ds(c, rows, stride=ROW_CHUNKS), :] = y[:, c * LANES:(c + 1) * LANES]


def _experts(block_expert, n_used, xs, wg, bg, wu, bu, wd, bd):
    nb = block_expert.shape[0]
    rows = EXPERT_ROWS
    blk = lambda b, be, used: (jnp.minimum(b, used[0] - 1), 0)
    wsel = lambda b, be, used: (be[b], 0, 0)
    wspec = pl.BlockSpec((1, D_MODEL, D_MODEL), wsel)
    bspec = pl.BlockSpec((1, 1, D_MODEL), wsel)
    return pl.pallas_call(
        functools.partial(_expert_kernel, rows=rows),
        out_shape=jax.ShapeDtypeStruct(xs.shape, jnp.float32),
        grid_spec=pltpu.PrefetchScalarGridSpec(
            num_scalar_prefetch=2,
            grid=(nb,),
            in_specs=[pl.BlockSpec((rows * ROW_CHUNKS, LANES), blk),
                      wspec, bspec, wspec, bspec, wspec, bspec],
            out_specs=pl.BlockSpec((rows * ROW_CHUNKS, LANES), lambda b, be, used: (b, 0))),
        compiler_params=pltpu.CompilerParams(dimension_semantics=("arbitrary",),
                                             vmem_limit_bytes=VMEM_LIMIT),
        name="moe_experts",
    )(block_expert, n_used, xs, wg, bg.reshape(N_EXPERTS, 1, -1), wu, bu.reshape(N_EXPERTS, 1, -1),
      wd, bd.reshape(N_EXPERTS, 1, -1))


def _combine_kernel(dest_hbm, ys_hbm, x_ref, gate_ref, gf_ref, o_ref, idx_ref, buf_ref, idx_sem, row_sem,
                    *, tile, final):
    i = pl.program_id(0)
    per = tile * TOP_K
    idx_copy = pltpu.make_async_copy(dest_hbm.at[pl.ds(pl.multiple_of(i * per, per), per)], idx_ref, idx_sem)
    idx_copy.start()
    idx_copy.wait()

    def body(t, c):
        for j in range(TOP_K):
            pltpu.make_async_copy(_row(ys_hbm, idx_ref[t * TOP_K + j]), _row(buf_ref.at[j], t), row_sem).start()
        return c

    lax.fori_loop(0, tile, body, 0)
    for j in range(TOP_K):
        pltpu.make_async_copy(ys_hbm.at[pl.ds(0, tile * ROW_CHUNKS), :], buf_ref.at[j], row_sem).wait()

    gates = gate_ref[...]
    cols = []
    for c in range(ROW_CHUNKS):
        acc = x_ref[:, c * LANES:(c + 1) * LANES]
        for j in range(TOP_K):
            acc = acc + buf_ref[j, pl.ds(c, tile, stride=ROW_CHUNKS), :] * gates[:, j:j + 1]
        cols.append(acc)
    x2 = jnp.concatenate(cols, axis=1)
    o_ref[...] = _rms(x2, gf_ref[...]) if final else x2


def _combine(dest_flat, ys, x1, gates_tok, g_final, *, final):
    t = MOVE_TILE
    per = t * TOP_K
    nt = dest_flat.shape[0] // per
    n = x1.shape[0]
    return pl.pallas_call(
        functools.partial(_combine_kernel, tile=t, final=final),
        out_shape=jax.ShapeDtypeStruct((n, D_MODEL), jnp.float32),
        grid=(nt,),
        in_specs=[pl.BlockSpec(memory_space=pl.ANY),
                  pl.BlockSpec(memory_space=pl.ANY),
                  pl.BlockSpec((t, D_MODEL), lambda i: (i, 0)),
                  pl.BlockSpec((t, TOP_K), lambda i: (i, 0)),
                  pl.BlockSpec((1, D_MODEL), lambda i: (0, 0))],
        out_specs=pl.BlockSpec((t, D_MODEL), lambda i: (i, 0)),
        scratch_shapes=[pltpu.SMEM((per,), jnp.int32),
                        pltpu.VMEM((TOP_K, t * ROW_CHUNKS, LANES), jnp.float32),
                        pltpu.SemaphoreType.DMA,
                        pltpu.SemaphoreType.DMA],
        compiler_params=pltpu.CompilerParams(dimension_semantics=("arbitrary",),
                                             vmem_limit_bytes=VMEM_LIMIT),
        name="moe_combine",
    )(dest_flat, ys, x1, gates_tok, g_final.reshape(1, -1))


def kernel(x, rel_table, norm1, w_in, lambda_q1, lambda_k1, lambda_q2, lambda_k2, subln_g, w_pool, pool_scale,
           w_out, norm2, w_router, b_router, w_gate, b_gate, w_up, b_up, w_down, b_down, final_norm):
    batch, seq, d = x.shape
    depth = norm1.shape[0]
    n = batch * seq
    assert d == D_MODEL and seq % ATT_TILE == 0 and seq % TOK_TILE == 0 and n % MOVE_TILE == 0
    bf = jnp.bfloat16

    q_scale = jnp.concatenate([jnp.full((QK_WIDTH,), QK_DIM ** -0.5, jnp.float32),
                               jnp.ones((IN_WIDTH - QK_WIDTH,), jnp.float32)])
    w_in_b = (w_in * q_scale).astype(bf)
    w_out_b = w_out.astype(bf)
    w_pool_b = w_pool.astype(bf)
    wr_t = jnp.swapaxes(w_router, 1, 2).astype(bf)
    wg_b, wu_b, wd_b = w_gate.astype(bf), w_up.astype(bf), w_down.astype(bf)
    tri = (lax.broadcasted_iota(jnp.int32, (TOK_TILE, TOK_TILE), 0)
           < lax.broadcasted_iota(jnp.int32, (TOK_TILE, TOK_TILE), 1)).astype(bf)

    bias = _bias_tiles(rel_table, ATT_TILE)
    nbk = N_BUCKETS // 2
    far = jnp.stack([rel_table[nbk - 1], rel_table[N_BUCKETS - 1]], axis=1).reshape(-1)

    n_assign = n * TOP_K
    n_blocks = -(-(n_assign + N_EXPERTS * (EXPERT_ROWS - 1)) // EXPERT_ROWS)
    n_rows = n_blocks * EXPERT_ROWS
    block_start = jnp.arange(n_blocks, dtype=jnp.int32) * EXPERT_ROWS

    x2d = x.reshape(n, d)
    for l in range(depth):
        lam_init = 0.8 - 0.6 * math.exp(-0.3 * l)
        proj = _norm_proj(x2d, norm1[l], w_in_b[l])
        att = _attention(proj, bias, far, lambda_q1[l], lambda_k1[l], lambda_q2[l], lambda_k2[l], subln_g[l],
                         batch=batch, seq=seq, lam_init=lam_init)
        x1, hflat, topi, rank, gates, counts = _mix_route(
            att, proj, x2d, w_pool_b[l], pool_scale[l], w_out_b[l], norm2[l], wr_t[l], b_router[l], tri, seq=seq)

        cnt = counts[:, 0].astype(jnp.int32)
        padded = ((cnt + EXPERT_ROWS - 1) // EXPERT_ROWS) * EXPERT_ROWS
        pad_end = jnp.cumsum(padded)
        dest = (pad_end - padded)[topi] + rank
        dest_flat = dest.T.reshape(-1)
        block_expert = jnp.clip(jnp.searchsorted(pad_end, block_start, side='right'),
                                0, N_EXPERTS - 1).astype(jnp.int32)
        n_used = (pad_end[-1:] // EXPERT_ROWS).astype(jnp.int32)

        xs = _dispatch(dest_flat, hflat, jnp.zeros((n_rows * ROW_CHUNKS, LANES), jnp.float32))
        ys = _experts(block_expert, n_used, xs, wg_b[l], b_gate[l], wu_b[l], b_up[l], wd_b[l], b_down[l])
        x2d = _combine(dest_flat, ys, x1, gates.T, final_norm, final=(l == depth - 1))
    return x2d.reshape(batch, seq, d)
```

```python
import functools
import math

import jax
import jax.numpy as jnp
from jax import lax
from jax.experimental import pallas as pl
from jax.experimental.pallas import tpu as pltpu

D_MODEL = 1024
N_HEADS = 4
QK_DIM = 64
V_DIM = 128
QK_WIDTH = N_HEADS * 2 * QK_DIM
ATT_WIDTH = N_HEADS * V_DIM
POOL_WINDOWS = (2, 4, 8, 16)
POOL_GROUP = 128
POOL_WIDTH = 512
IN_WIDTH = 2 * QK_WIDTH + ATT_WIDTH + POOL_WIDTH
N_BUCKETS = 32
MAX_DISTANCE = 128
N_EXPERTS = 32
TOP_K = 4
SWIGLU_LIMIT = 7.0
SWIGLU_ALPHA = 1.702
EPS = 1e-5

LANES = 128
SUBLANES = 8
ROW_CHUNKS = D_MODEL // LANES
VMEM_LIMIT = 56 * 1024 * 1024

TOK_TILE = 512
ATT_TILE = 512
ATT_CHUNK = 256
BIAS_TILES = 5
HALO = 16
EXPERT_ROWS = 256
MOVE_TILE = 256

_NT = (((1,), (1,)), ((), ()))
LOG2E = math.log2(math.e)


def _rms(x, g):
    return x * lax.rsqrt(jnp.mean(x * x, axis=-1, keepdims=True) + EPS) * g


def _bias_kernel(tab_ref, out_ref, *, tile):
    h = pl.program_id(0)
    d = pl.program_id(1)
    key = lax.broadcasted_iota(jnp.int32, (tile, tile), 0)
    qry = lax.broadcasted_iota(jnp.int32, (tile, tile), 1)
    rel = (d - BIAS_TILES // 2) * tile + key - qry
    nb = N_BUCKETS // 2
    max_exact = nb // 2
    ret = jnp.where(rel > 0, nb, 0)
    n = jnp.abs(rel)
    n_f = jnp.maximum(n, 1).astype(jnp.float32)
    large = max_exact + (jnp.log(n_f / max_exact) / math.log(MAX_DISTANCE / max_exact)
                         * (nb - max_exact)).astype(jnp.int32)
    large = jnp.minimum(large, nb - 1)
    bucket = ret + jnp.where(n < max_exact, n, large)
    acc = jnp.zeros((tile, tile), jnp.float32)
    for b in range(N_BUCKETS):
        acc = jnp.where(bucket == b, tab_ref[b * N_HEADS + h], acc)
    out_ref[0, 0] = acc * LOG2E


def _bias_tiles(rel_table, tile):
    return pl.pallas_call(
        functools.partial(_bias_kernel, tile=tile),
        out_shape=jax.ShapeDtypeStruct((N_HEADS, BIAS_TILES, tile, tile), jnp.float32),
        grid=(N_HEADS, BIAS_TILES),
        in_specs=[pl.BlockSpec(memory_space=pltpu.SMEM)],
        out_specs=pl.BlockSpec((1, 1, tile, tile), lambda h, d: (h, d, 0, 0)),
        name="rel_bias_tiles",
    )(rel_table.reshape(-1))


def _norm_proj_kernel(x_ref, g_ref, wkp_ref, wqvt_ref, kp_ref, qvt_ref):
    h = _rms(x_ref[...], g_ref[...]).astype(jnp.bfloat16)
    kp_ref[...] = jnp.dot(h, wkp_ref[...], preferred_element_type=jnp.float32).astype(kp_ref.dtype)
    qvt_ref[...] = lax.dot_general(wqvt_ref[...], h, _NT,
                                   preferred_element_type=jnp.float32).astype(qvt_ref.dtype)


def _norm_proj(x2d, g, w_kp, w_qvt):
    n = x2d.shape[0]
    width = QK_WIDTH + POOL_WIDTH
    width_t = QK_WIDTH + ATT_WIDTH
    return pl.pallas_call(
        _norm_proj_kernel,
        out_shape=(jax.ShapeDtypeStruct((n, width), jnp.bfloat16),
                   jax.ShapeDtypeStruct((width_t, n), jnp.bfloat16)),
        grid=(n // TOK_TILE,),
        in_specs=[pl.BlockSpec((TOK_TILE, D_MODEL), lambda i: (i, 0)),
                  pl.BlockSpec((1, D_MODEL), lambda i: (0, 0)),
                  pl.BlockSpec((D_MODEL, width), lambda i: (0, 0)),
                  pl.BlockSpec((width_t, D_MODEL), lambda i: (0, 0))],
        out_specs=(pl.BlockSpec((TOK_TILE, width), lambda i: (i, 0)),
                   pl.BlockSpec((width_t, TOK_TILE), lambda i: (0, i))),
        compiler_params=pltpu.CompilerParams(dimension_semantics=("parallel",),
                                             vmem_limit_bytes=VMEM_LIMIT),
        name="norm_in_proj",
    )(x2d, g.reshape(1, D_MODEL), w_kp, w_qvt)


def _attn_kernel(qt_ref, k_ref, vt_ref, bias_ref, lq1_ref, lk1_ref, lq2_ref, lk2_ref, sg_ref,
                 o_ref, qs_ref, s0_ref, s1_ref, mc0_ref, mc1_ref, m_ref, l_ref, acc_ref, *, tile, n_kv, lam_init):
    qi = pl.program_id(2)
    qt = qt_ref[...]
    row = lax.broadcasted_iota(jnp.int32, qt.shape, 0)
    zero = jnp.zeros_like(qt)
    qs_ref[:, :tile] = jnp.where(row < QK_DIM, qt, zero)
    qs_ref[:, tile:] = jnp.where(row >= QK_DIM, qt, zero)

    m_ref[...] = jnp.full(m_ref.shape, -jnp.inf, jnp.float32)
    l_ref[...] = jnp.zeros(l_ref.shape, jnp.float32)
    acc_ref[...] = jnp.zeros(acc_ref.shape, jnp.float32)
    chunks = [slice(c * ATT_CHUNK, (c + 1) * ATT_CHUNK) for c in range(2 * tile // ATT_CHUNK)]

    def scores(ki, cols, s_ref, mc_ref):
        ki = jnp.minimum(ki, n_kv - 1)
        k = k_ref[pl.ds(pl.multiple_of(ki * tile, tile), tile), :]
        d = jnp.clip(ki - qi, -2, 2) + 2
        q0 = cols.start % tile
        s = (jnp.dot(k, qs_ref[:, cols], preferred_element_type=jnp.float32)
             + bias_ref[0, d, :, q0:q0 + ATT_CHUNK])
        s_ref[:, cols] = s
        mc_ref[:, cols] = jnp.max(s, axis=0, keepdims=True)

    def update(ki, cols, s_ref, mc_ref):
        vt = vt_ref[:, pl.ds(pl.multiple_of(ki * tile, tile), tile)]
        m_prev = m_ref[:, cols]
        m_new = jnp.maximum(m_prev, mc_ref[:, cols])
        alpha = jnp.exp2(m_prev - m_new)
        p = jnp.exp2(s_ref[:, cols] - m_new)
        l_ref[:, cols] = alpha * l_ref[:, cols] + jnp.sum(p, axis=0, keepdims=True)
        acc_ref[:, cols] = alpha * acc_ref[:, cols] + jnp.dot(vt, p.astype(jnp.bfloat16),
                                                              preferred_element_type=jnp.float32)
        m_ref[:, cols] = m_new

    for cols in chunks:
        scores(0, cols, s0_ref, mc0_ref)

    def pair(j, carry):
        a = 2 * j
        for cols in chunks:
            scores(a + 1, cols, s1_ref, mc1_ref)
            update(a, cols, s0_ref, mc0_ref)
        for cols in chunks:
            scores(a + 2, cols, s0_ref, mc0_ref)
            update(a + 1, cols, s1_ref, mc1_ref)
        return carry

    lax.fori_loop(0, n_kv // 2, pair, 0)

    o = acc_ref[...] / l_ref[...]
    lam = (jnp.exp(jnp.sum(lq1_ref[...] * lk1_ref[...], axis=1, keepdims=True))
           - jnp.exp(jnp.sum(lq2_ref[...] * lk2_ref[...], axis=1, keepdims=True)) + lam_init)
    o = o[:, :tile] - lam * o[:, tile:]
    o = o * lax.rsqrt(jnp.mean(o * o, axis=0, keepdims=True) + EPS) * (sg_ref[...] * (1.0 - lam_init))
    o_ref[...] = o.T.astype(o_ref.dtype)


def _attention(kp, qvt, bias, lq1, lk1, lq2, lk2, sub_g, *, batch, seq, lam_init):
    t = ATT_TILE
    nq = seq // t
    assert nq % 2 == 0
    vrow = QK_WIDTH // LANES
    vec = lambda width: pl.BlockSpec((1, width), lambda b, h, qi: (0, 0))
    return pl.pallas_call(
        functools.partial(_attn_kernel, tile=t, n_kv=nq, lam_init=lam_init),
        out_shape=jax.ShapeDtypeStruct((batch * seq, ATT_WIDTH), jnp.bfloat16),
        grid=(batch, N_HEADS, nq),
        in_specs=[pl.BlockSpec((LANES, t), lambda b, h, qi: (h, b * nq + qi)),
                  pl.BlockSpec((seq, LANES), lambda b, h, qi: (b, h)),
                  pl.BlockSpec((V_DIM, seq), lambda b, h, qi: (vrow + h, b)),
                  pl.BlockSpec((1, BIAS_TILES, t, t), lambda b, h, qi: (h, 0, 0, 0)),
                  vec(QK_DIM), vec(QK_DIM), vec(QK_DIM), vec(QK_DIM),
                  pl.BlockSpec((V_DIM, 1), lambda b, h, qi: (0, 0))],
        out_specs=pl.BlockSpec((t, V_DIM), lambda b, h, qi: (b * nq + qi, h)),
        scratch_shapes=[pltpu.VMEM((LANES, 2 * t), jnp.bfloat16),
                        pltpu.VMEM((t, 2 * t), jnp.float32),
                        pltpu.VMEM((t, 2 * t), jnp.float32),
                        pltpu.VMEM((1, 2 * t), jnp.float32),
                        pltpu.VMEM((1, 2 * t), jnp.float32),
                        pltpu.VMEM((1, 2 * t), jnp.float32),
                        pltpu.VMEM((1, 2 * t), jnp.float32),
                        pltpu.VMEM((V_DIM, 2 * t), jnp.float32)],
        compiler_params=pltpu.CompilerParams(dimension_semantics=("parallel", "parallel", "parallel"),
                                             vmem_limit_bytes=VMEM_LIMIT),
        name="diff_attention",
    )(qvt, kp, qvt, bias, lq1.reshape(1, -1), lk1.reshape(1, -1), lq2.reshape(1, -1),
      lk2.reshape(1, -1), sub_g.reshape(-1, 1))


def _mix_route_kernel(att_ref, p_ref, prev_ref, next_ref, x_ref, wpool_ref, pscale_ref, wout_ref, g2_ref,
                      wr_ref, br_ref, tri_ref,
                      x1_ref, hflat_ref, topi_ref, rank_ref, gate_ref, cnt_out_ref, cnt_ref,
                      *, tile, seq):
    i = pl.program_id(0)
    tiles_per_seq = seq // tile
    si = i % tiles_per_seq

    @pl.when(i == 0)
    def _():
        cnt_ref[...] = jnp.zeros(cnt_ref.shape, jnp.float32)

    pc = p_ref[...].astype(jnp.float32)
    pp = jnp.where(si == 0, 0.0, prev_ref[...].astype(jnp.float32))
    pn = jnp.where(si == tiles_per_seq - 1, 0.0, next_ref[...].astype(jnp.float32))
    pe = jnp.concatenate([pp, pc, pn], axis=0)
    n_ext = tile + 2 * HALO
    pos = si * tile + lax.broadcasted_iota(jnp.int32, (tile, 1), 0)
    pooled = []
    for g, win in enumerate(POOL_WINDOWS):
        xg = pe[:, g * POOL_GROUP:(g + 1) * POOL_GROUP]
        w = xg + pltpu.roll(xg, 1, 0)
        half = 1
        while 2 * half < win:
            w = pltpu.roll(w, half, 0) + pltpu.roll(w, n_ext - half, 0)
            half *= 2
        lo = jnp.maximum(pos - win // 2, 0)
        hi = jnp.minimum(pos - win // 2 + win - 1, seq - 1)
        mean = w[HALO:HALO + tile] / (hi - lo + 1).astype(jnp.float32)
        dg = (mean - pc[:, g * POOL_GROUP:(g + 1) * POOL_GROUP]).astype(jnp.bfloat16)
        pooled.append(jnp.dot(dg, wpool_ref[g], preferred_element_type=jnp.float32))
    pool = jnp.concatenate(pooled, axis=1) * pscale_ref[...]

    mix = jnp.concatenate([att_ref[...], pool.astype(jnp.bfloat16)], axis=1)
    x1 = x_ref[...] + jnp.dot(mix, wout_ref[...], preferred_element_type=jnp.float32)
    x1_ref[...] = x1

    h2 = _rms(x1, g2_ref[...])
    for c in range(ROW_CHUNKS):
        hflat_ref[pl.ds(c, tile, stride=ROW_CHUNKS), :] = h2[:, c * LANES:(c + 1) * LANES]

    logits = lax.dot_general(wr_ref[...], h2.astype(jnp.bfloat16), _NT,
                             preferred_element_type=jnp.float32) + br_ref[...]
    eio = lax.broadcasted_iota(jnp.int32, logits.shape, 0)
    work = logits
    sel = jnp.zeros(logits.shape, jnp.float32)
    top_v, top_i, hot = [], [], []
    for _ in range(TOP_K):
        mx = jnp.max(work, axis=0, keepdims=True)
        idx = jnp.min(jnp.where(work == mx, eio, N_EXPERTS), axis=0, keepdims=True)
        oh = eio == idx
        top_v.append(mx)
        top_i.append(idx)
        hot.append(oh)
        work = jnp.where(oh, -jnp.inf, work)
        sel = sel + oh.astype(jnp.float32)
    ex = [jnp.exp(v - top_v[0]) for v in top_v]
    den = ex[0] + ex[1] + ex[2] + ex[3]
    ahead = jnp.dot(sel.astype(jnp.bfloat16), tri_ref[...], preferred_element_type=jnp.float32) + cnt_ref[...]
    for j in range(TOP_K):
        topi_ref[j:j + 1, :] = top_i[j]
        gate_ref[j:j + 1, :] = ex[j] / den
        rank_ref[j:j + 1, :] = jnp.sum(jnp.where(hot[j], ahead, 0.0), axis=0, keepdims=True).astype(jnp.int32)
    cnt_ref[...] = cnt_ref[...] + jnp.sum(sel, axis=1, keepdims=True)
    cnt_out_ref[...] = jnp.broadcast_to(cnt_ref[...], cnt_out_ref.shape)


def _mix_route(att, proj, x2d, wpool, pscale, wout, g2, wr_t, br, tri, *, seq):
    n = x2d.shape[0]
    t = TOK_TILE
    nt = n // t
    pcol = QK_WIDTH // POOL_WIDTH
    hb = t // HALO
    last_halo = n // HALO - 1
    const = lambda shape: pl.BlockSpec(shape, lambda i: (0,) * len(shape))
    out_shape = (jax.ShapeDtypeStruct((n, D_MODEL), jnp.float32),
                 jax.ShapeDtypeStruct((n * ROW_CHUNKS, LANES), jnp.float32),
                 jax.ShapeDtypeStruct((TOP_K, n), jnp.int32),
                 jax.ShapeDtypeStruct((TOP_K, n), jnp.int32),
                 jax.ShapeDtypeStruct((TOP_K, n), jnp.float32),
                 jax.ShapeDtypeStruct((N_EXPERTS, LANES), jnp.float32))
    return pl.pallas_call(
        functools.partial(_mix_route_kernel, tile=t, seq=seq),
        out_shape=out_shape,
        grid=(nt,),
        in_specs=[pl.BlockSpec((t, ATT_WIDTH), lambda i: (i, 0)),
                  pl.BlockSpec((t, POOL_WIDTH), lambda i: (i, pcol)),
                  pl.BlockSpec((HALO, POOL_WIDTH), lambda i: (jnp.maximum(i * hb - 1, 0), pcol)),
                  pl.BlockSpec((HALO, POOL_WIDTH), lambda i: (jnp.minimum((i + 1) * hb, last_halo), pcol)),
                  pl.BlockSpec((t, D_MODEL), lambda i: (i, 0)),
                  const((len(POOL_WINDOWS), POOL_GROUP, POOL_GROUP)),
                  const((1, POOL_WIDTH)),
                  const((D_MODEL, D_MODEL)),
                  const((1, D_MODEL)),
                  const((N_EXPERTS, D_MODEL)),
                  const((N_EXPERTS, 1)),
                  const((t, t))],
        out_specs=(pl.BlockSpec((t, D_MODEL), lambda i: (i, 0)),
                   pl.BlockSpec((t * ROW_CHUNKS, LANES), lambda i: (i, 0)),
                   pl.BlockSpec((TOP_K, t), lambda i: (0, i)),
                   pl.BlockSpec((TOP_K, t), lambda i: (0, i)),
                   pl.BlockSpec((TOP_K, t), lambda i: (0, i)),
                   pl.BlockSpec((N_EXPERTS, LANES), lambda i: (0, 0))),
        scratch_shapes=[pltpu.VMEM((N_EXPERTS, 1), jnp.float32)],
        compiler_params=pltpu.CompilerParams(dimension_semantics=("arbitrary",),
                                             vmem_limit_bytes=VMEM_LIMIT),
        name="mix_out_route",
    )(att, proj, proj, proj, x2d, wpool, pscale.reshape(1, -1), wout, g2.reshape(1, -1), wr_t,
      br.reshape(-1, 1), tri)


def _row(ref, r):
    return ref.at[pl.ds(pl.multiple_of(r * ROW_CHUNKS, ROW_CHUNKS), ROW_CHUNKS), :]


def _dispatch_kernel(dest_hbm, h_ref, xs_in, xs_out, idx_ref, idx_sem, row_sem, *, tile):
    del xs_in
    i = pl.program_id(0)
    per = tile * TOP_K
    idx_copy = pltpu.make_async_copy(dest_hbm.at[pl.ds(pl.multiple_of(i * per, per), per)], idx_ref, idx_sem)
    idx_copy.start()
    idx_copy.wait()

    def body(t, c):
        src = _row(h_ref, t)
        for j in range(TOP_K):
            pltpu.make_async_copy(src, _row(xs_out, idx_ref[t * TOP_K + j]), row_sem).start()
        return c

    lax.fori_loop(0, tile, body, 0)
    for _ in range(TOP_K):
        pltpu.make_async_copy(h_ref, xs_out.at[pl.ds(0, tile * ROW_CHUNKS), :], row_sem).wait()


def _dispatch(dest_flat, hflat, xs_zero):
    t = MOVE_TILE
    per = t * TOP_K
    nt = dest_flat.shape[0] // per
    return pl.pallas_call(
        functools.partial(_dispatch_kernel, tile=t),
        out_shape=jax.ShapeDtypeStruct(xs_zero.shape, xs_zero.dtype),
        grid=(nt,),
        in_specs=[pl.BlockSpec(memory_space=pl.ANY),
                  pl.BlockSpec((t * ROW_CHUNKS, LANES), lambda i: (i, 0)),
                  pl.BlockSpec(memory_space=pl.ANY)],
        out_specs=pl.BlockSpec(memory_space=pl.ANY),
        scratch_shapes=[pltpu.SMEM((per,), jnp.int32),
                        pltpu.SemaphoreType.DMA,
                        pltpu.SemaphoreType.DMA],
        input_output_aliases={2: 0},
        compiler_params=pltpu.CompilerParams(dimension_semantics=("arbitrary",),
                                             vmem_limit_bytes=VMEM_LIMIT),
        name="moe_dispatch",
    )(dest_flat, hflat, xs_zero)


def _expert_kernel(be_ref, used_ref, x_ref, wg_ref, bg_ref, wu_ref, bu_ref, wd_ref, bd_ref, y_ref, *, rows):
    del be_ref
    live = pl.program_id(0) < used_ref[0]

    @pl.when(jnp.logical_not(live))
    def _():
        y_ref[...] = jnp.zeros(y_ref.shape, y_ref.dtype)

    @pl.when(live)
    def _():
        x = jnp.concatenate([x_ref[pl.ds(c, rows, stride=ROW_CHUNKS), :] for c in range(ROW_CHUNKS)],
                            axis=1).astype(jnp.bfloat16)
        g = jnp.dot(x, wg_ref[0], preferred_element_type=jnp.float32) + bg_ref[0]
        u = jnp.dot(x, wu_ref[0], preferred_element_type=jnp.float32) + bu_ref[0]
        g = jnp.minimum(g, SWIGLU_LIMIT)
        u = jnp.clip(u, -SWIGLU_LIMIT, SWIGLU_LIMIT)
        act = (u + 1.0) * (g * (1.0 / (1.0 + jnp.exp(-SWIGLU_ALPHA * g))))
        y = jnp.dot(act.astype(jnp.bfloat16), wd_ref[0], preferred_element_type=jnp.float32) + bd_ref[0]
        for c in range(ROW_CHUNKS):
            y_ref[pl.ds(c, rows, stride=ROW_CHUNKS), :] = y[:, c * LANES:(c + 1) * LANES]


def _experts(block_expert, n_used, xs, wg, bg, wu, bu, wd, bd):
    nb = block_expert.shape[0]
    rows = EXPERT_ROWS
    blk = lambda b, be, used: (jnp.minimum(b, used[0] - 1), 0)
    wsel = lambda b, be, used: (be[b], 0, 0)
    wspec = pl.BlockSpec((1, D_MODEL, D_MODEL), wsel)
    bspec = pl.BlockSpec((1, 1, D_MODEL), wsel)
    return pl.pallas_call(
        functools.partial(_expert_kernel, rows=rows),
        out_shape=jax.ShapeDtypeStruct(xs.shape, jnp.float32),
        grid_spec=pltpu.PrefetchScalarGridSpec(
            num_scalar_prefetch=2,
            grid=(nb,),
            in_specs=[pl.BlockSpec((rows * ROW_CHUNKS, LANES), blk),
                      wspec, bspec, wspec, bspec, wspec, bspec],
            out_specs=pl.BlockSpec((rows * ROW_CHUNKS, LANES), lambda b, be, used: (b, 0))),
        compiler_params=pltpu.CompilerParams(dimension_semantics=("arbitrary",),
                                             vmem_limit_bytes=VMEM_LIMIT),
        name="moe_experts",
    )(block_expert, n_used, xs, wg, bg.reshape(N_EXPERTS, 1, -1), wu, bu.reshape(N_EXPERTS, 1, -1),
      wd, bd.reshape(N_EXPERTS, 1, -1))


def _combine_kernel(dest_hbm, ys_hbm, x_ref, gate_ref, gf_ref, o_ref, idx_ref, buf_ref, idx_sem, row_sem,
                    *, tile, final):
    i = pl.program_id(0)
    per = tile * TOP_K
    idx_copy = pltpu.make_async_copy(dest_hbm.at[pl.ds(pl.multiple_of(i * per, per), per)], idx_ref, idx_sem)
    idx_copy.start()
    idx_copy.wait()

    def body(t, c):
        for j in range(TOP_K):
            pltpu.make_async_copy(_row(ys_hbm, idx_ref[t * TOP_K + j]), _row(buf_ref.at[j], t), row_sem).start()
        return c

    lax.fori_loop(0, tile, body, 0)
    for j in range(TOP_K):
        pltpu.make_async_copy(ys_hbm.at[pl.ds(0, tile * ROW_CHUNKS), :], buf_ref.at[j], row_sem).wait()

    gates = gate_ref[...]
    cols = []
    for c in range(ROW_CHUNKS):
        acc = x_ref[:, c * LANES:(c + 1) * LANES]
        for j in range(TOP_K):
            acc = acc + buf_ref[j, pl.ds(c, tile, stride=ROW_CHUNKS), :] * gates[:, j:j + 1]
        cols.append(acc)
    x2 = jnp.concatenate(cols, axis=1)
    o_ref[...] = _rms(x2, gf_ref[...]) if final else x2


def _combine(dest_flat, ys, x1, gates_tok, g_final, *, final):
    t = MOVE_TILE
    per = t * TOP_K
    nt = dest_flat.shape[0] // per
    n = x1.shape[0]
    return pl.pallas_call(
        functools.partial(_combine_kernel, tile=t, final=final),
        out_shape=jax.ShapeDtypeStruct((n, D_MODEL), jnp.float32),
        grid=(nt,),
        in_specs=[pl.BlockSpec(memory_space=pl.ANY),
                  pl.BlockSpec(memory_space=pl.ANY),
                  pl.BlockSpec((t, D_MODEL), lambda i: (i, 0)),
                  pl.BlockSpec((t, TOP_K), lambda i: (i, 0)),
                  pl.BlockSpec((1, D_MODEL), lambda i: (0, 0))],
        out_specs=pl.BlockSpec((t, D_MODEL), lambda i: (i, 0)),
        scratch_shapes=[pltpu.SMEM((per,), jnp.int32),
                        pltpu.VMEM((TOP_K, t * ROW_CHUNKS, LANES), jnp.float32),
                        pltpu.SemaphoreType.DMA,
                        pltpu.SemaphoreType.DMA],
        compiler_params=pltpu.CompilerParams(dimension_semantics=("arbitrary",),
                                             vmem_limit_bytes=VMEM_LIMIT),
        name="moe_combine",
    )(dest_flat, ys, x1, gates_tok, g_final.reshape(1, -1))


def kernel(x, rel_table, norm1, w_in, lambda_q1, lambda_k1, lambda_q2, lambda_k2, subln_g, w_pool, pool_scale,
           w_out, norm2, w_router, b_router, w_gate, b_gate, w_up, b_up, w_down, b_down, final_norm):
    batch, seq, d = x.shape
    depth = norm1.shape[0]
    n = batch * seq
    assert d == D_MODEL and seq % ATT_TILE == 0 and seq % TOK_TILE == 0 and n % MOVE_TILE == 0
    bf = jnp.bfloat16

    w_q = w_in[:, :, :QK_WIDTH] * (QK_DIM ** -0.5 * LOG2E)
    w_k = w_in[:, :, QK_WIDTH:2 * QK_WIDTH]
    w_v = w_in[:, :, 2 * QK_WIDTH:2 * QK_WIDTH + ATT_WIDTH]
    w_p = w_in[:, :, 2 * QK_WIDTH + ATT_WIDTH:]
    w_kp = jnp.concatenate([w_k, w_p], axis=2).astype(bf)
    w_qvt = jnp.swapaxes(jnp.concatenate([w_q, w_v], axis=2), 1, 2).astype(bf)
    w_out_b = w_out.astype(bf)
    w_pool_b = w_pool.astype(bf)
    wr_t = jnp.swapaxes(w_router, 1, 2).astype(bf)
    wg_b, wu_b, wd_b = w_gate.astype(bf), w_up.astype(bf), w_down.astype(bf)
    tri = (lax.broadcasted_iota(jnp.int32, (TOK_TILE, TOK_TILE), 0)
           < lax.broadcasted_iota(jnp.int32, (TOK_TILE, TOK_TILE), 1)).astype(bf)

    bias = _bias_tiles(rel_table, ATT_TILE)

    n_assign = n * TOP_K
    n_blocks = -(-(n_assign + N_EXPERTS * (EXPERT_ROWS - 1)) // EXPERT_ROWS)
    n_rows = n_blocks * EXPERT_ROWS
    block_start = jnp.arange(n_blocks, dtype=jnp.int32) * EXPERT_ROWS

    x2d = x.reshape(n, d)
    for l in range(depth):
        lam_init = 0.8 - 0.6 * math.exp(-0.3 * l)
        kp, qvt = _norm_proj(x2d, norm1[l], w_kp[l], w_qvt[l])
        att = _attention(kp, qvt, bias, lambda_q1[l], lambda_k1[l], lambda_q2[l], lambda_k2[l], subln_g[l],
                         batch=batch, seq=seq, lam_init=lam_init)
        x1, hflat, topi, rank, gates, counts = _mix_route(
            att, kp, x2d, w_pool_b[l], pool_scale[l], w_out_b[l], norm2[l], wr_t[l], b_router[l], tri, seq=seq)

        cnt = counts[:, 0].astype(jnp.int32)
        padded = ((cnt + EXPERT_ROWS - 1) // EXPERT_ROWS) * EXPERT_ROWS
        pad_end = jnp.cumsum(padded)
        experts = jnp.arange(N_EXPERTS, dtype=jnp.int32)
        seg_start = jnp.sum(jnp.where(topi[..., None] == experts, pad_end - padded, 0), axis=-1)
        dest_flat = (seg_start + rank).T.reshape(-1)
        block_expert = jnp.minimum(jnp.sum(pad_end[None, :] <= block_start[:, None], axis=1),
                                   N_EXPERTS - 1).astype(jnp.int32)
        n_used = (pad_end[-1:] // EXPERT_ROWS).astype(jnp.int32)

        xs = _dispatch(dest_flat, hflat, jnp.zeros((n_rows * ROW_CHUNKS, LANES), jnp.float32))
        ys = _experts(block_expert, n_used, xs, wg_b[l], b_gate[l], wu_b[l], b_up[l], wd_b[l], b_down[l])
        x2d = _combine(dest_flat, ys, x1, gates.T, final_norm, final=(l == depth - 1))
    return x2d.reshape(batch, seq, d)
```

```python
import functools
import math

import jax
import jax.numpy as jnp
from jax import lax
from jax.experimental import pallas as pl
from jax.experimental.pallas import tpu as pltpu

D_MODEL = 1024
N_HEADS = 4
QK_DIM = 64
V_DIM = 128
QK_WIDTH = N_HEADS * 2 * QK_DIM
ATT_WIDTH = N_HEADS * V_DIM
POOL_WINDOWS = (2, 4, 8, 16)
POOL_GROUP = 128
POOL_WIDTH = 512
IN_WIDTH = 2 * QK_WIDTH + ATT_WIDTH + POOL_WIDTH
N_BUCKETS = 32
MAX_DISTANCE = 128
N_EXPERTS = 32
TOP_K = 4
SWIGLU_LIMIT = 7.0
SWIGLU_ALPHA = 1.702
EPS = 1e-5

LANES = 128
SUBLANES = 8
ROW_CHUNKS = D_MODEL // LANES
VMEM_LIMIT = 56 * 1024 * 1024

TOK_TILE = 512
ATT_TILE = 512
ATT_CHUNK = 256
BIAS_TILES = 5
HALO = 16
EXPERT_ROWS = 256
DISPATCH_TILE = 1024
MOVE_TILE = 256

_NT = (((1,), (1,)), ((), ()))
LOG2E = math.log2(math.e)


def _rms(x, g):
    return x * lax.rsqrt(jnp.mean(x * x, axis=-1, keepdims=True) + EPS) * g


def _bias_kernel(tab_ref, out_ref, *, tile):
    h = pl.program_id(0)
    d = pl.program_id(1)
    key = lax.broadcasted_iota(jnp.int32, (tile, tile), 0)
    qry = lax.broadcasted_iota(jnp.int32, (tile, tile), 1)
    rel = (d - BIAS_TILES // 2) * tile + key - qry
    nb = N_BUCKETS // 2
    max_exact = nb // 2
    ret = jnp.where(rel > 0, nb, 0)
    n = jnp.abs(rel)
    n_f = jnp.maximum(n, 1).astype(jnp.float32)
    large = max_exact + (jnp.log(n_f / max_exact) / math.log(MAX_DISTANCE / max_exact)
                         * (nb - max_exact)).astype(jnp.int32)
    large = jnp.minimum(large, nb - 1)
    bucket = ret + jnp.where(n < max_exact, n, large)
    acc = jnp.zeros((tile, tile), jnp.float32)
    for b in range(N_BUCKETS):
        acc = jnp.where(bucket == b, tab_ref[b * N_HEADS + h], acc)
    out_ref[0, 0] = acc * LOG2E


def _bias_tiles(rel_table, tile):
    return pl.pallas_call(
        functools.partial(_bias_kernel, tile=tile),
        out_shape=jax.ShapeDtypeStruct((N_HEADS, BIAS_TILES, tile, tile), jnp.float32),
        grid=(N_HEADS, BIAS_TILES),
        in_specs=[pl.BlockSpec(memory_space=pltpu.SMEM)],
        out_specs=pl.BlockSpec((1, 1, tile, tile), lambda h, d: (h, d, 0, 0)),
        name="rel_bias_tiles",
    )(rel_table.reshape(-1))


def _norm_proj_kernel(x_ref, g_ref, wkp_ref, wqvt_ref, kp_ref, qvt_ref):
    h = _rms(x_ref[...], g_ref[...]).astype(jnp.bfloat16)
    kp_ref[...] = jnp.dot(h, wkp_ref[...], preferred_element_type=jnp.float32).astype(kp_ref.dtype)
    qvt_ref[...] = lax.dot_general(wqvt_ref[...], h, _NT,
                                   preferred_element_type=jnp.float32).astype(qvt_ref.dtype)


def _norm_proj(x2d, g, w_kp, w_qvt):
    n = x2d.shape[0]
    width = QK_WIDTH + POOL_WIDTH
    width_t = QK_WIDTH + ATT_WIDTH
    return pl.pallas_call(
        _norm_proj_kernel,
        out_shape=(jax.ShapeDtypeStruct((n, width), jnp.bfloat16),
                   jax.ShapeDtypeStruct((width_t, n), jnp.bfloat16)),
        grid=(n // TOK_TILE,),
        in_specs=[pl.BlockSpec((TOK_TILE, D_MODEL), lambda i: (i, 0)),
                  pl.BlockSpec((1, D_MODEL), lambda i: (0, 0)),
                  pl.BlockSpec((D_MODEL, width), lambda i: (0, 0)),
                  pl.BlockSpec((width_t, D_MODEL), lambda i: (0, 0))],
        out_specs=(pl.BlockSpec((TOK_TILE, width), lambda i: (i, 0)),
                   pl.BlockSpec((width_t, TOK_TILE), lambda i: (0, i))),
        compiler_params=pltpu.CompilerParams(dimension_semantics=("parallel",),
                                             vmem_limit_bytes=VMEM_LIMIT),
        name="norm_in_proj",
    )(x2d, g.reshape(1, D_MODEL), w_kp, w_qvt)


def _attn_kernel(qt_ref, k_ref, vt_ref, bias_ref, lq1_ref, lk1_ref, lq2_ref, lk2_ref, sg_ref,
                 o_ref, qs_ref, s0_ref, s1_ref, mc0_ref, mc1_ref, m_ref, l_ref, acc_ref, *, tile, n_kv, lam_init):
    qi = pl.program_id(2)
    qt = qt_ref[...]
    row = lax.broadcasted_iota(jnp.int32, qt.shape, 0)
    zero = jnp.zeros_like(qt)
    qs_ref[:, :tile] = jnp.where(row < QK_DIM, qt, zero)
    qs_ref[:, tile:] = jnp.where(row >= QK_DIM, qt, zero)

    m_ref[...] = jnp.full(m_ref.shape, -jnp.inf, jnp.float32)
    l_ref[...] = jnp.zeros(l_ref.shape, jnp.float32)
    acc_ref[...] = jnp.zeros(acc_ref.shape, jnp.float32)
    chunks = [slice(c * ATT_CHUNK, (c + 1) * ATT_CHUNK) for c in range(2 * tile // ATT_CHUNK)]

    def scores(ki, cols, s_ref, mc_ref):
        ki = jnp.minimum(ki, n_kv - 1)
        k = k_ref[pl.ds(pl.multiple_of(ki * tile, tile), tile), :]
        d = jnp.clip(ki - qi, -2, 2) + 2
        q0 = cols.start % tile
        s = (jnp.dot(k, qs_ref[:, cols], preferred_element_type=jnp.float32)
             + bias_ref[0, d, :, q0:q0 + ATT_CHUNK])
        s_ref[:, cols] = s
        mc_ref[:, cols] = jnp.max(s, axis=0, keepdims=True)

    def update(ki, cols, s_ref, mc_ref):
        vt = vt_ref[:, pl.ds(pl.multiple_of(ki * tile, tile), tile)]
        m_prev = m_ref[:, cols]
        m_new = jnp.maximum(m_prev, mc_ref[:, cols])
        alpha = jnp.exp2(m_prev - m_new)
        p = jnp.exp2(s_ref[:, cols] - m_new)
        l_ref[:, cols] = alpha * l_ref[:, cols] + jnp.sum(p, axis=0, keepdims=True)
        acc_ref[:, cols] = alpha * acc_ref[:, cols] + jnp.dot(vt, p.astype(jnp.bfloat16),
                                                              preferred_element_type=jnp.float32)
        m_ref[:, cols] = m_new

    for cols in chunks:
        scores(0, cols, s0_ref, mc0_ref)

    def pair(j, carry):
        a = 2 * j
        for cols in chunks:
            scores(a + 1, cols, s1_ref, mc1_ref)
            update(a, cols, s0_ref, mc0_ref)
        for cols in chunks:
            scores(a + 2, cols, s0_ref, mc0_ref)
            update(a + 1, cols, s1_ref, mc1_ref)
        return carry

    lax.fori_loop(0, n_kv // 2, pair, 0)

    o = acc_ref[...] / l_ref[...]
    lam = (jnp.exp(jnp.sum(lq1_ref[...] * lk1_ref[...], axis=1, keepdims=True))
           - jnp.exp(jnp.sum(lq2_ref[...] * lk2_ref[...], axis=1, keepdims=True)) + lam_init)
    o = o[:, :tile] - lam * o[:, tile:]
    o = o * lax.rsqrt(jnp.mean(o * o, axis=0, keepdims=True) + EPS) * (sg_ref[...] * (1.0 - lam_init))
    o_ref[...] = o.T.astype(o_ref.dtype)


def _attention(kp, qvt, bias, lq1, lk1, lq2, lk2, sub_g, *, batch, seq, lam_init):
    t = ATT_TILE
    nq = seq // t
    assert nq % 2 == 0
    vrow = QK_WIDTH // LANES
    vec = lambda width: pl.BlockSpec((1, width), lambda b, h, qi: (0, 0))
    return pl.pallas_call(
        functools.partial(_attn_kernel, tile=t, n_kv=nq, lam_init=lam_init),
        out_shape=jax.ShapeDtypeStruct((batch * seq, ATT_WIDTH), jnp.bfloat16),
        grid=(batch, N_HEADS, nq),
        in_specs=[pl.BlockSpec((LANES, t), lambda b, h, qi: (h, b * nq + qi)),
                  pl.BlockSpec((seq, LANES), lambda b, h, qi: (b, h)),
                  pl.BlockSpec((V_DIM, seq), lambda b, h, qi: (vrow + h, b)),
                  pl.BlockSpec((1, BIAS_TILES, t, t), lambda b, h, qi: (h, 0, 0, 0)),
                  vec(QK_DIM), vec(QK_DIM), vec(QK_DIM), vec(QK_DIM),
                  pl.BlockSpec((V_DIM, 1), lambda b, h, qi: (0, 0))],
        out_specs=pl.BlockSpec((t, V_DIM), lambda b, h, qi: (b * nq + qi, h)),
        scratch_shapes=[pltpu.VMEM((LANES, 2 * t), jnp.bfloat16),
                        pltpu.VMEM((t, 2 * t), jnp.float32),
                        pltpu.VMEM((t, 2 * t), jnp.float32),
                        pltpu.VMEM((1, 2 * t), jnp.float32),
                        pltpu.VMEM((1, 2 * t), jnp.float32),
                        pltpu.VMEM((1, 2 * t), jnp.float32),
                        pltpu.VMEM((1, 2 * t), jnp.float32),
                        pltpu.VMEM((V_DIM, 2 * t), jnp.float32)],
        compiler_params=pltpu.CompilerParams(dimension_semantics=("parallel", "parallel", "parallel"),
                                             vmem_limit_bytes=VMEM_LIMIT),
        name="diff_attention",
    )(qvt, kp, qvt, bias, lq1.reshape(1, -1), lk1.reshape(1, -1), lq2.reshape(1, -1),
      lk2.reshape(1, -1), sub_g.reshape(-1, 1))


def _mix_route_kernel(att_ref, p_ref, prev_ref, next_ref, x_ref, wpool_ref, pscale_ref, wout_ref, g2_ref,
                      wr_ref, br_ref, tri_ref,
                      x1_ref, hflat_ref, topi_ref, rank_ref, gate_ref, cnt_out_ref, cnt_ref,
                      *, tile, seq):
    i = pl.program_id(0)
    tiles_per_seq = seq // tile
    si = i % tiles_per_seq

    @pl.when(i == 0)
    def _():
        cnt_ref[...] = jnp.zeros(cnt_ref.shape, jnp.float32)

    pc = p_ref[...].astype(jnp.float32)
    pp = jnp.where(si == 0, 0.0, prev_ref[...].astype(jnp.float32))
    pn = jnp.where(si == tiles_per_seq - 1, 0.0, next_ref[...].astype(jnp.float32))
    pe = jnp.concatenate([pp, pc, pn], axis=0)
    n_ext = tile + 2 * HALO
    pos = si * tile + lax.broadcasted_iota(jnp.int32, (tile, 1), 0)
    pooled = []
    for g, win in enumerate(POOL_WINDOWS):
        xg = pe[:, g * POOL_GROUP:(g + 1) * POOL_GROUP]
        w = xg + pltpu.roll(xg, 1, 0)
        half = 1
        while 2 * half < win:
            w = pltpu.roll(w, half, 0) + pltpu.roll(w, n_ext - half, 0)
            half *= 2
        lo = jnp.maximum(pos - win // 2, 0)
        hi = jnp.minimum(pos - win // 2 + win - 1, seq - 1)
        mean = w[HALO:HALO + tile] / (hi - lo + 1).astype(jnp.float32)
        dg = (mean - pc[:, g * POOL_GROUP:(g + 1) * POOL_GROUP]).astype(jnp.bfloat16)
        pooled.append(jnp.dot(dg, wpool_ref[g], preferred_element_type=jnp.float32))
    pool = jnp.concatenate(pooled, axis=1) * pscale_ref[...]

    mix = jnp.concatenate([att_ref[...], pool.astype(jnp.bfloat16)], axis=1)
    x1 = x_ref[...] + jnp.dot(mix, wout_ref[...], preferred_element_type=jnp.float32)
    x1_ref[...] = x1

    h2 = _rms(x1, g2_ref[...])
    for c in range(ROW_CHUNKS):
        hflat_ref[pl.ds(c, tile, stride=ROW_CHUNKS), :] = h2[:, c * LANES:(c + 1) * LANES]

    logits = lax.dot_general(wr_ref[...], h2.astype(jnp.bfloat16), _NT,
                             preferred_element_type=jnp.float32) + br_ref[...]
    eio = lax.broadcasted_iota(jnp.int32, logits.shape, 0)
    work = logits
    sel = jnp.zeros(logits.shape, jnp.float32)
    top_v, top_i, hot = [], [], []
    for _ in range(TOP_K):
        mx = jnp.max(work, axis=0, keepdims=True)
        idx = jnp.min(jnp.where(work == mx, eio, N_EXPERTS), axis=0, keepdims=True)
        oh = eio == idx
        top_v.append(mx)
        top_i.append(idx)
        hot.append(oh)
        work = jnp.where(oh, -jnp.inf, work)
        sel = sel + oh.astype(jnp.float32)
    ex = [jnp.exp(v - top_v[0]) for v in top_v]
    den = ex[0] + ex[1] + ex[2] + ex[3]
    ahead = jnp.dot(sel.astype(jnp.bfloat16), tri_ref[...], preferred_element_type=jnp.float32) + cnt_ref[...]
    for j in range(TOP_K):
        topi_ref[j:j + 1, :] = top_i[j]
        gate_ref[j:j + 1, :] = ex[j] / den
        rank_ref[j:j + 1, :] = jnp.sum(jnp.where(hot[j], ahead, 0.0), axis=0, keepdims=True).astype(jnp.int32)
    cnt_ref[...] = cnt_ref[...] + jnp.sum(sel, axis=1, keepdims=True)
    cnt_out_ref[...] = jnp.broadcast_to(cnt_ref[...], cnt_out_ref.shape)


def _mix_route(att, proj, x2d, wpool, pscale, wout, g2, wr_t, br, tri, *, seq):
    n = x2d.shape[0]
    t = TOK_TILE
    nt = n // t
    pcol = QK_WIDTH // POOL_WIDTH
    hb = t // HALO
    last_halo = n // HALO - 1
    const = lambda shape: pl.BlockSpec(shape, lambda i: (0,) * len(shape))
    out_shape = (jax.ShapeDtypeStruct((n, D_MODEL), jnp.float32),
                 jax.ShapeDtypeStruct((n * ROW_CHUNKS, LANES), jnp.float32),
                 jax.ShapeDtypeStruct((TOP_K, n), jnp.int32),
                 jax.ShapeDtypeStruct((TOP_K, n), jnp.int32),
                 jax.ShapeDtypeStruct((TOP_K, n), jnp.float32),
                 jax.ShapeDtypeStruct((N_EXPERTS, LANES), jnp.float32))
    return pl.pallas_call(
        functools.partial(_mix_route_kernel, tile=t, seq=seq),
        out_shape=out_shape,
        grid=(nt,),
        in_specs=[pl.BlockSpec((t, ATT_WIDTH), lambda i: (i, 0)),
                  pl.BlockSpec((t, POOL_WIDTH), lambda i: (i, pcol)),
                  pl.BlockSpec((HALO, POOL_WIDTH), lambda i: (jnp.maximum(i * hb - 1, 0), pcol)),
                  pl.BlockSpec((HALO, POOL_WIDTH), lambda i: (jnp.minimum((i + 1) * hb, last_halo), pcol)),
                  pl.BlockSpec((t, D_MODEL), lambda i: (i, 0)),
                  const((len(POOL_WINDOWS), POOL_GROUP, POOL_GROUP)),
                  const((1, POOL_WIDTH)),
                  const((D_MODEL, D_MODEL)),
                  const((1, D_MODEL)),
                  const((N_EXPERTS, D_MODEL)),
                  const((N_EXPERTS, 1)),
                  const((t, t))],
        out_specs=(pl.BlockSpec((t, D_MODEL), lambda i: (i, 0)),
                   pl.BlockSpec((t * ROW_CHUNKS, LANES), lambda i: (i, 0)),
                   pl.BlockSpec((TOP_K, t), lambda i: (0, i)),
                   pl.BlockSpec((TOP_K, t), lambda i: (0, i)),
                   pl.BlockSpec((TOP_K, t), lambda i: (0, i)),
                   pl.BlockSpec((N_EXPERTS, LANES), lambda i: (0, 0))),
        scratch_shapes=[pltpu.VMEM((N_EXPERTS, 1), jnp.float32)],
        compiler_params=pltpu.CompilerParams(dimension_semantics=("arbitrary",),
                                             vmem_limit_bytes=VMEM_LIMIT),
        name="mix_out_route",
    )(att, proj, proj, proj, x2d, wpool, pscale.reshape(1, -1), wout, g2.reshape(1, -1), wr_t,
      br.reshape(-1, 1), tri)


def _row(ref, r):
    return ref.at[pl.ds(pl.multiple_of(r * ROW_CHUNKS, ROW_CHUNKS), ROW_CHUNKS), :]


def _dispatch_kernel(pad_lo_ref, pad_n_ref, dest_hbm, h_ref, xs_out, idx_ref, zero_ref, idx_sem, row_sem, pad_sem,
                     *, tile):
    i = pl.program_id(0)
    per = tile * TOP_K

    @pl.when(i == 0)
    def _():
        zero_ref[...] = jnp.zeros(zero_ref.shape, zero_ref.dtype)

        def each_pad_row(fn):
            def per_range(e, c):
                lax.fori_loop(0, pad_n_ref[e], lambda r, c2: (fn(pad_lo_ref[e] + r), c2)[1], 0)
                return c
            lax.fori_loop(0, N_EXPERTS + 1, per_range, 0)

        each_pad_row(lambda r: pltpu.make_async_copy(zero_ref, _row(xs_out, r), pad_sem).start())
        each_pad_row(lambda r: pltpu.make_async_copy(zero_ref, _row(xs_out, r), pad_sem).wait())

    idx_copy = pltpu.make_async_copy(dest_hbm.at[pl.ds(pl.multiple_of(i * per, per), per)], idx_ref, idx_sem)
    idx_copy.start()
    idx_copy.wait()

    def body(t, c):
        src = _row(h_ref, t)
        for j in range(TOP_K):
            pltpu.make_async_copy(src, _row(xs_out, idx_ref[t * TOP_K + j]), row_sem).start(priority=j % 2)
        return c

    lax.fori_loop(0, tile, body, 0)
    for _ in range(TOP_K):
        pltpu.make_async_copy(h_ref, xs_out.at[pl.ds(0, tile * ROW_CHUNKS), :], row_sem).wait()


def _dispatch(pad_lo, pad_n, dest_flat, hflat, n_rows):
    t = DISPATCH_TILE
    per = t * TOP_K
    nt = dest_flat.shape[0] // per
    return pl.pallas_call(
        functools.partial(_dispatch_kernel, tile=t),
        out_shape=jax.ShapeDtypeStruct((n_rows * ROW_CHUNKS, LANES), jnp.float32),
        grid_spec=pltpu.PrefetchScalarGridSpec(
            num_scalar_prefetch=2,
            grid=(nt,),
            in_specs=[pl.BlockSpec(memory_space=pl.ANY),
                      pl.BlockSpec((t * ROW_CHUNKS, LANES), lambda i, lo, cnt: (i, 0))],
            out_specs=pl.BlockSpec(memory_space=pl.ANY),
            scratch_shapes=[pltpu.SMEM((per,), jnp.int32),
                            pltpu.VMEM((ROW_CHUNKS, LANES), jnp.float32),
                            pltpu.SemaphoreType.DMA,
                            pltpu.SemaphoreType.DMA,
                            pltpu.SemaphoreType.DMA]),
        compiler_params=pltpu.CompilerParams(dimension_semantics=("arbitrary",),
                                             vmem_limit_bytes=VMEM_LIMIT),
        name="moe_dispatch",
    )(pad_lo, pad_n, dest_flat, hflat)


def _expert_kernel(be_ref, used_ref, x_ref, wg_ref, bg_ref, wu_ref, bu_ref, wd_ref, bd_ref, y_ref,
                   wg_bf, wu_bf, wd_bf, *, rows):
    b = pl.program_id(0)
    live = b < used_ref[0]

    @pl.when(jnp.logical_not(live))
    def _():
        y_ref[...] = jnp.zeros(y_ref.shape, y_ref.dtype)

    @pl.when(jnp.logical_and(live, jnp.logical_or(b == 0, be_ref[b] != be_ref[jnp.maximum(b - 1, 0)])))
    def _():
        wg_bf[...] = wg_ref[0, 0].astype(jnp.bfloat16)
        wu_bf[...] = wu_ref[0, 0].astype(jnp.bfloat16)
        wd_bf[...] = wd_ref[0, 0].astype(jnp.bfloat16)

    @pl.when(live)
    def _():
        x = jnp.concatenate([x_ref[pl.ds(c, rows, stride=ROW_CHUNKS), :] for c in range(ROW_CHUNKS)],
                            axis=1).astype(jnp.bfloat16)
        g = jnp.dot(x, wg_bf[...], preferred_element_type=jnp.float32) + bg_ref[0, 0]
        u = jnp.dot(x, wu_bf[...], preferred_element_type=jnp.float32) + bu_ref[0, 0]
        g = jnp.minimum(g, SWIGLU_LIMIT)
        u = jnp.clip(u, -SWIGLU_LIMIT, SWIGLU_LIMIT)
        act = (u + 1.0) * (g * (1.0 / (1.0 + jnp.exp(-SWIGLU_ALPHA * g))))
        y = jnp.dot(act.astype(jnp.bfloat16), wd_bf[...], preferred_element_type=jnp.float32) + bd_ref[0, 0]
        for c in range(ROW_CHUNKS):
            y_ref[pl.ds(c, rows, stride=ROW_CHUNKS), :] = y[:, c * LANES:(c + 1) * LANES]


def _experts(block_expert, n_used, xs, wg, bg, wu, bu, wd, bd, *, layer):
    nb = block_expert.shape[0]
    rows = EXPERT_ROWS
    depth = wg.shape[0]
    blk = lambda b, be, used: (jnp.minimum(b, used[0] - 1), 0)
    wsel = lambda b, be, used: (layer, be[b], 0, 0)
    wspec = pl.BlockSpec((1, 1, D_MODEL, D_MODEL), wsel)
    bspec = pl.BlockSpec((1, 1, 1, D_MODEL), wsel)
    bias4 = lambda a: a.reshape(depth, N_EXPERTS, 1, D_MODEL)
    return pl.pallas_call(
        functools.partial(_expert_kernel, rows=rows),
        out_shape=jax.ShapeDtypeStruct(xs.shape, jnp.float32),
        grid_spec=pltpu.PrefetchScalarGridSpec(
            num_scalar_prefetch=2,
            grid=(nb,),
            in_specs=[pl.BlockSpec((rows * ROW_CHUNKS, LANES), blk),
                      wspec, bspec, wspec, bspec, wspec, bspec],
            out_specs=pl.BlockSpec((rows * ROW_CHUNKS, LANES), lambda b, be, used: (b, 0)),
            scratch_shapes=[pltpu.VMEM((D_MODEL, D_MODEL), jnp.bfloat16)] * 3),
        compiler_params=pltpu.CompilerParams(dimension_semantics=("arbitrary",),
                                             vmem_limit_bytes=VMEM_LIMIT),
        name="moe_experts",
    )(block_expert, n_used, xs, wg, bias4(bg), wu, bias4(bu), wd, bias4(bd))


def _combine_kernel(dest_hbm, ys_hbm, x_ref, gate_ref, gf_ref, o_ref, idx_ref, buf_ref, idx_sem, row_sem,
                    *, tile, final):
    i = pl.program_id(0)
    per = tile * TOP_K

    def gather(step, slot):
        idx_copy = pltpu.make_async_copy(dest_hbm.at[pl.ds(pl.multiple_of(step * per, per), per)],
                                         idx_ref.at[slot], idx_sem)
        idx_copy.start()
        idx_copy.wait()

        def body(t, c):
            for j in range(TOP_K):
                pltpu.make_async_copy(_row(ys_hbm, idx_ref[slot, t * TOP_K + j]), _row(buf_ref.at[slot, j], t),
                                      row_sem.at[slot]).start(priority=j % 2)
            return c

        lax.fori_loop(0, tile, body, 0)

    @pl.when(i == 0)
    def _():
        gather(0, 0)

    @pl.when(i + 1 < pl.num_programs(0))
    def _():
        gather(i + 1, (i + 1) % 2)

    slot = i % 2
    for j in range(TOP_K):
        pltpu.make_async_copy(ys_hbm.at[pl.ds(0, tile * ROW_CHUNKS), :], buf_ref.at[slot, j],
                              row_sem.at[slot]).wait()

    gates = gate_ref[...]
    cols = []
    for c in range(ROW_CHUNKS):
        acc = x_ref[:, c * LANES:(c + 1) * LANES]
        for j in range(TOP_K):
            acc = acc + buf_ref[slot, j, pl.ds(c, tile, stride=ROW_CHUNKS), :] * gates[:, j:j + 1]
        cols.append(acc)
    x2 = jnp.concatenate(cols, axis=1)
    o_ref[...] = _rms(x2, gf_ref[...]) if final else x2


def _combine(dest_flat, ys, x1, gates_tok, g_final, *, final):
    t = MOVE_TILE
    per = t * TOP_K
    nt = dest_flat.shape[0] // per
    n = x1.shape[0]
    return pl.pallas_call(
        functools.partial(_combine_kernel, tile=t, final=final),
        out_shape=jax.ShapeDtypeStruct((n, D_MODEL), jnp.float32),
        grid=(nt,),
        in_specs=[pl.BlockSpec(memory_space=pl.ANY),
                  pl.BlockSpec(memory_space=pl.ANY),
                  pl.BlockSpec((t, D_MODEL), lambda i: (i, 0)),
                  pl.BlockSpec((t, TOP_K), lambda i: (i, 0)),
                  pl.BlockSpec((1, D_MODEL), lambda i: (0, 0))],
        out_specs=pl.BlockSpec((t, D_MODEL), lambda i: (i, 0)),
        scratch_shapes=[pltpu.SMEM((2, per), jnp.int32),
                        pltpu.VMEM((2, TOP_K, t * ROW_CHUNKS, LANES), jnp.float32),
                        pltpu.SemaphoreType.DMA,
                        pltpu.SemaphoreType.DMA((2,))],
        compiler_params=pltpu.CompilerParams(dimension_semantics=("arbitrary",),
                                             vmem_limit_bytes=VMEM_LIMIT),
        name="moe_combine",
    )(dest_flat, ys, x1, gates_tok, g_final.reshape(1, -1))


def kernel(x, rel_table, norm1, w_in, lambda_q1, lambda_k1, lambda_q2, lambda_k2, subln_g, w_pool, pool_scale,
           w_out, norm2, w_router, b_router, w_gate, b_gate, w_up, b_up, w_down, b_down, final_norm):
    batch, seq, d = x.shape
    depth = norm1.shape[0]
    n = batch * seq
    assert d == D_MODEL and seq % ATT_TILE == 0 and seq % TOK_TILE == 0
    assert n % MOVE_TILE == 0 and n % DISPATCH_TILE == 0
    bf = jnp.bfloat16

    w_q = w_in[:, :, :QK_WIDTH] * (QK_DIM ** -0.5 * LOG2E)
    w_k = w_in[:, :, QK_WIDTH:2 * QK_WIDTH]
    w_v = w_in[:, :, 2 * QK_WIDTH:2 * QK_WIDTH + ATT_WIDTH]
    w_p = w_in[:, :, 2 * QK_WIDTH + ATT_WIDTH:]
    w_kp = jnp.concatenate([w_k, w_p], axis=2).astype(bf)
    w_qvt = jnp.swapaxes(jnp.concatenate([w_q, w_v], axis=2), 1, 2).astype(bf)
    w_out_b = w_out.astype(bf)
    w_pool_b = w_pool.astype(bf)
    wr_t = jnp.swapaxes(w_router, 1, 2).astype(bf)
    tri = (lax.broadcasted_iota(jnp.int32, (TOK_TILE, TOK_TILE), 0)
           < lax.broadcasted_iota(jnp.int32, (TOK_TILE, TOK_TILE), 1)).astype(bf)

    bias = _bias_tiles(rel_table, ATT_TILE)

    n_assign = n * TOP_K
    n_blocks = -(-(n_assign + N_EXPERTS * (EXPERT_ROWS - 1)) // EXPERT_ROWS)
    n_rows = n_blocks * EXPERT_ROWS
    block_start = jnp.arange(n_blocks, dtype=jnp.int32) * EXPERT_ROWS

    x2d = x.reshape(n, d)
    for l in range(depth):
        lam_init = 0.8 - 0.6 * math.exp(-0.3 * l)
        kp, qvt = _norm_proj(x2d, norm1[l], w_kp[l], w_qvt[l])
        att = _attention(kp, qvt, bias, lambda_q1[l], lambda_k1[l], lambda_q2[l], lambda_k2[l], subln_g[l],
                         batch=batch, seq=seq, lam_init=lam_init)
        x1, hflat, topi, rank, gates, counts = _mix_route(
            att, kp, x2d, w_pool_b[l], pool_scale[l], w_out_b[l], norm2[l], wr_t[l], b_router[l], tri, seq=seq)

        cnt = counts[:, 0].astype(jnp.int32)
        padded = ((cnt + EXPERT_ROWS - 1) // EXPERT_ROWS) * EXPERT_ROWS
        pad_end = jnp.cumsum(padded)
        experts = jnp.arange(N_EXPERTS, dtype=jnp.int32)
        seg_start = jnp.sum(jnp.where(topi[..., None] == experts, pad_end - padded, 0), axis=-1)
        dest_flat = (seg_start + rank).T.reshape(-1)
        block_expert = jnp.minimum(jnp.sum(pad_end[None, :] <= block_start[:, None], axis=1),
                                   N_EXPERTS - 1).astype(jnp.int32)
        n_used = (pad_end[-1:] // EXPERT_ROWS).astype(jnp.int32)

        pad_lo = jnp.concatenate([pad_end - padded + cnt, pad_end[-1:]])
        pad_n = jnp.concatenate([padded - cnt, n_rows - pad_end[-1:]])
        xs = _dispatch(pad_lo, pad_n, dest_flat, hflat, n_rows)
        ys = _experts(block_expert, n_used, xs, w_gate, b_gate, w_up, b_up, w_down, b_down, layer=l)
        x2d = _combine(dest_flat, ys, x1, gates.T, final_norm, final=(l == depth - 1))
    return x2d.reshape(batch, seq, d)
```

```python
import functools
import math

import jax
import jax.numpy as jnp
from jax import lax
from jax.experimental import pallas as pl
from jax.experimental.pallas import tpu as pltpu

D_MODEL = 1024
N_HEADS = 4
QK_DIM = 64
V_DIM = 128
QK_WIDTH = N_HEADS * 2 * QK_DIM
ATT_WIDTH = N_HEADS * V_DIM
POOL_WINDOWS = (2, 4, 8, 16)
POOL_GROUP = 128
POOL_WIDTH = 512
IN_WIDTH = 2 * QK_WIDTH + ATT_WIDTH + POOL_WIDTH
N_BUCKETS = 32
MAX_DISTANCE = 128
N_EXPERTS = 32
TOP_K = 4
SWIGLU_LIMIT = 7.0
SWIGLU_ALPHA = 1.702
EPS = 1e-5

LANES = 128
SUBLANES = 8
ROW_CHUNKS = D_MODEL // LANES
VMEM_LIMIT = 56 * 1024 * 1024

TOK_TILE = 512
ATT_TILE = 512
ATT_CHUNK = 256
BIAS_TILES = 5
NEAR_TILES = 4
V_EXT = V_DIM + 16
HALO = 16
EXPERT_ROWS = 256
DISPATCH_TILE = 1024
MOVE_TILE = 256

_NT = (((1,), (1,)), ((), ()))
LOG2E = math.log2(math.e)


def _rms(x, g):
    return x * lax.rsqrt(jnp.mean(x * x, axis=-1, keepdims=True) + EPS) * g


def _bias_kernel(tab_ref, out_ref, *, tile):
    h = pl.program_id(0)
    d = pl.program_id(1)
    key = lax.broadcasted_iota(jnp.int32, (tile, tile), 0)
    qry = lax.broadcasted_iota(jnp.int32, (tile, tile), 1)
    rel = (d - BIAS_TILES // 2) * tile + key - qry
    nb = N_BUCKETS // 2
    max_exact = nb // 2
    ret = jnp.where(rel > 0, nb, 0)
    n = jnp.abs(rel)
    n_f = jnp.maximum(n, 1).astype(jnp.float32)
    large = max_exact + (jnp.log(n_f / max_exact) / math.log(MAX_DISTANCE / max_exact)
                         * (nb - max_exact)).astype(jnp.int32)
    large = jnp.minimum(large, nb - 1)
    bucket = ret + jnp.where(n < max_exact, n, large)
    acc = jnp.zeros((tile, tile), jnp.float32)
    for b in range(N_BUCKETS):
        acc = jnp.where(bucket == b, tab_ref[b * N_HEADS + h], acc)
    out_ref[0, 0] = acc * LOG2E


def _bias_tiles(rel_table, tile):
    return pl.pallas_call(
        functools.partial(_bias_kernel, tile=tile),
        out_shape=jax.ShapeDtypeStruct((N_HEADS, BIAS_TILES, tile, tile), jnp.float32),
        grid=(N_HEADS, BIAS_TILES),
        in_specs=[pl.BlockSpec(memory_space=pltpu.SMEM)],
        out_specs=pl.BlockSpec((1, 1, tile, tile), lambda h, d: (h, d, 0, 0)),
        name="rel_bias_tiles",
    )(rel_table.reshape(-1))


def _norm_proj_kernel(x_ref, g_ref, wkp_ref, wqvt_ref, kp_ref, qvt_ref):
    h = _rms(x_ref[...], g_ref[...]).astype(jnp.bfloat16)
    kp_ref[...] = jnp.dot(h, wkp_ref[...], preferred_element_type=jnp.float32).astype(kp_ref.dtype)
    qvt_ref[...] = lax.dot_general(wqvt_ref[...], h, _NT,
                                   preferred_element_type=jnp.float32).astype(qvt_ref.dtype)


def _norm_proj(x2d, g, w_kp, w_qvt):
    n = x2d.shape[0]
    width = QK_WIDTH + POOL_WIDTH
    width_t = QK_WIDTH + ATT_WIDTH
    return pl.pallas_call(
        _norm_proj_kernel,
        out_shape=(jax.ShapeDtypeStruct((n, width), jnp.bfloat16),
                   jax.ShapeDtypeStruct((width_t, n), jnp.bfloat16)),
        grid=(n // TOK_TILE,),
        in_specs=[pl.BlockSpec((TOK_TILE, D_MODEL), lambda i: (i, 0)),
                  pl.BlockSpec((1, D_MODEL), lambda i: (0, 0)),
                  pl.BlockSpec((D_MODEL, width), lambda i: (0, 0)),
                  pl.BlockSpec((width_t, D_MODEL), lambda i: (0, 0))],
        out_specs=(pl.BlockSpec((TOK_TILE, width), lambda i: (i, 0)),
                   pl.BlockSpec((width_t, TOK_TILE), lambda i: (0, i))),
        compiler_params=pltpu.CompilerParams(dimension_semantics=("parallel",),
                                             vmem_limit_bytes=VMEM_LIMIT),
        name="norm_in_proj",
    )(x2d, g.reshape(1, D_MODEL), w_kp, w_qvt)


def _attn_kernel(far_ref, qt_ref, k_ref, vt_ref, bias_ref, lq1_ref, lk1_ref, lq2_ref, lk2_ref, sg_ref,
                 o_ref, qs_ref, vte_ref, s0_ref, s1_ref, mc0_ref, mc1_ref, m_ref, acc_ref,
                 *, tile, n_kv, lam_init):
    h = pl.program_id(1)
    qi = pl.program_id(2)
    n_far = n_kv - NEAR_TILES

    @pl.when(qi == 0)
    def _():
        vte_ref[:V_DIM, :] = vt_ref[...]
        vte_ref[V_DIM:, :] = jnp.ones((V_EXT - V_DIM, vte_ref.shape[1]), vte_ref.dtype)

    qt = qt_ref[...]
    row = lax.broadcasted_iota(jnp.int32, qt.shape, 0)
    zero = jnp.zeros_like(qt)
    qs_ref[:, :tile] = jnp.where(row < QK_DIM, qt, zero)
    qs_ref[:, tile:] = jnp.where(row >= QK_DIM, qt, zero)

    m_ref[...] = jnp.full(m_ref.shape, -jnp.inf, jnp.float32)
    acc_ref[...] = jnp.zeros(acc_ref.shape, jnp.float32)
    chunks = [slice(c * ATT_CHUNK, (c + 1) * ATT_CHUNK) for c in range(2 * tile // ATT_CHUNK)]
    w0 = jnp.clip(qi - 1, 0, n_kv - NEAR_TILES)
    c_below = far_ref[2 * h] * LOG2E
    c_above = far_ref[2 * h + 1] * LOG2E

    def far_tile(f):
        f = jnp.minimum(f, n_far - 1)
        return jnp.where(f < w0, f, f + NEAR_TILES)

    def far_const(ki):
        return jnp.where(ki < qi, c_below, c_above)

    def logits(ki, cols):
        k = k_ref[pl.ds(pl.multiple_of(ki * tile, tile), tile), :]
        return jnp.dot(k, qs_ref[:, cols], preferred_element_type=jnp.float32)

    def scores_near(ki, cols, s_ref, mc_ref):
        d = jnp.clip(ki - qi, -2, 2) + 2
        q0 = cols.start % tile
        s = logits(ki, cols) + bias_ref[0, d, :, q0:q0 + ATT_CHUNK]
        s_ref[:, cols] = s
        mc_ref[:, cols] = jnp.max(s, axis=0, keepdims=True)

    def scores_far(ki, cols, s_ref, mc_ref):
        s = logits(ki, cols)
        s_ref[:, cols] = s
        mc_ref[:, cols] = jnp.max(s, axis=0, keepdims=True) + far_const(ki)

    def update(ki, cols, s_ref, mc_ref, const):
        vt = vte_ref[:, pl.ds(pl.multiple_of(ki * tile, tile), tile)]
        m_prev = m_ref[:, cols]
        m_new = jnp.maximum(m_prev, mc_ref[:, cols])
        alpha = jnp.exp2(m_prev - m_new)
        p = jnp.exp2(s_ref[:, cols] - (m_new - const))
        acc_ref[:, cols] = alpha * acc_ref[:, cols] + jnp.dot(vt, p.astype(jnp.bfloat16),
                                                              preferred_element_type=jnp.float32)
        m_ref[:, cols] = m_new

    bufs = ((s0_ref, mc0_ref), (s1_ref, mc1_ref))
    for cols in chunks:
        scores_near(w0, cols, *bufs[0])
    for i in range(NEAR_TILES):
        for cols in chunks:
            if i + 1 < NEAR_TILES:
                scores_near(w0 + i + 1, cols, *bufs[(i + 1) % 2])
            elif n_far > 0:
                scores_far(far_tile(0), cols, *bufs[(i + 1) % 2])
            update(w0 + i, cols, *bufs[i % 2], 0.0)

    def pair(j, carry):
        ka, kb, kc = far_tile(2 * j), far_tile(2 * j + 1), far_tile(2 * j + 2)
        for cols in chunks:
            scores_far(kb, cols, *bufs[(NEAR_TILES + 1) % 2])
            update(ka, cols, *bufs[NEAR_TILES % 2], far_const(ka))
        for cols in chunks:
            scores_far(kc, cols, *bufs[NEAR_TILES % 2])
            update(kb, cols, *bufs[(NEAR_TILES + 1) % 2], far_const(kb))
        return carry

    lax.fori_loop(0, n_far // 2, pair, 0)

    acc = acc_ref[...]
    o = acc[:V_DIM] / acc[V_DIM:V_DIM + 1]
    lam = (jnp.exp(jnp.sum(lq1_ref[...] * lk1_ref[...], axis=1, keepdims=True))
           - jnp.exp(jnp.sum(lq2_ref[...] * lk2_ref[...], axis=1, keepdims=True)) + lam_init)
    o = o[:, :tile] - lam * o[:, tile:]
    o = o * lax.rsqrt(jnp.mean(o * o, axis=0, keepdims=True) + EPS) * (sg_ref[...] * (1.0 - lam_init))
    o_ref[...] = o.T.astype(o_ref.dtype)


def _attention(kp, qvt, bias, far, lq1, lk1, lq2, lk2, sub_g, *, batch, seq, lam_init):
    t = ATT_TILE
    nq = seq // t
    assert nq >= NEAR_TILES and (nq - NEAR_TILES) % 2 == 0
    vrow = QK_WIDTH // LANES
    vec = lambda width: pl.BlockSpec((1, width), lambda b, h, qi: (0, 0))
    return pl.pallas_call(
        functools.partial(_attn_kernel, tile=t, n_kv=nq, lam_init=lam_init),
        out_shape=jax.ShapeDtypeStruct((batch * seq, ATT_WIDTH), jnp.bfloat16),
        grid=(batch, N_HEADS, nq),
        in_specs=[pl.BlockSpec(memory_space=pltpu.SMEM),
                  pl.BlockSpec((LANES, t), lambda b, h, qi: (h, b * nq + qi)),
                  pl.BlockSpec((seq, LANES), lambda b, h, qi: (b, h)),
                  pl.BlockSpec((V_DIM, seq), lambda b, h, qi: (vrow + h, b)),
                  pl.BlockSpec((1, BIAS_TILES, t, t), lambda b, h, qi: (h, 0, 0, 0)),
                  vec(QK_DIM), vec(QK_DIM), vec(QK_DIM), vec(QK_DIM),
                  pl.BlockSpec((V_DIM, 1), lambda b, h, qi: (0, 0))],
        out_specs=pl.BlockSpec((t, V_DIM), lambda b, h, qi: (b * nq + qi, h)),
        scratch_shapes=[pltpu.VMEM((LANES, 2 * t), jnp.bfloat16),
                        pltpu.VMEM((V_EXT, seq), jnp.bfloat16),
                        pltpu.VMEM((t, 2 * t), jnp.float32),
                        pltpu.VMEM((t, 2 * t), jnp.float32),
                        pltpu.VMEM((1, 2 * t), jnp.float32),
                        pltpu.VMEM((1, 2 * t), jnp.float32),
                        pltpu.VMEM((1, 2 * t), jnp.float32),
                        pltpu.VMEM((V_EXT, 2 * t), jnp.float32)],
        compiler_params=pltpu.CompilerParams(dimension_semantics=("arbitrary", "arbitrary", "arbitrary"),
                                             vmem_limit_bytes=VMEM_LIMIT),
        name="diff_attention",
    )(far, qvt, kp, qvt, bias, lq1.reshape(1, -1), lk1.reshape(1, -1), lq2.reshape(1, -1),
      lk2.reshape(1, -1), sub_g.reshape(-1, 1))


def _mix_route_kernel(att_ref, p_ref, prev_ref, next_ref, x_ref, wpool_ref, pscale_ref, wout_ref, g2_ref,
                      wr_ref, br_ref, tri_ref,
                      x1_ref, hflat_ref, topi_ref, rank_ref, gate_ref, cnt_out_ref, cnt_ref,
                      *, tile, seq):
    i = pl.program_id(0)
    tiles_per_seq = seq // tile
    si = i % tiles_per_seq

    @pl.when(i == 0)
    def _():
        cnt_ref[...] = jnp.zeros(cnt_ref.shape, jnp.float32)

    pc = p_ref[...].astype(jnp.float32)
    pp = jnp.where(si == 0, 0.0, prev_ref[...].astype(jnp.float32))
    pn = jnp.where(si == tiles_per_seq - 1, 0.0, next_ref[...].astype(jnp.float32))
    pe = jnp.concatenate([pp, pc, pn], axis=0)
    n_ext = tile + 2 * HALO
    pos = si * tile + lax.broadcasted_iota(jnp.int32, (tile, 1), 0)
    pooled = []
    for g, win in enumerate(POOL_WINDOWS):
        xg = pe[:, g * POOL_GROUP:(g + 1) * POOL_GROUP]
        w = xg + pltpu.roll(xg, 1, 0)
        half = 1
        while 2 * half < win:
            w = pltpu.roll(w, half, 0) + pltpu.roll(w, n_ext - half, 0)
            half *= 2
        lo = jnp.maximum(pos - win // 2, 0)
        hi = jnp.minimum(pos - win // 2 + win - 1, seq - 1)
        mean = w[HALO:HALO + tile] / (hi - lo + 1).astype(jnp.float32)
        dg = (mean - pc[:, g * POOL_GROUP:(g + 1) * POOL_GROUP]).astype(jnp.bfloat16)
        pooled.append(jnp.dot(dg, wpool_ref[g], preferred_element_type=jnp.float32))
    pool = jnp.concatenate(pooled, axis=1) * pscale_ref[...]

    mix = jnp.concatenate([att_ref[...], pool.astype(jnp.bfloat16)], axis=1)
    x1 = x_ref[...] + jnp.dot(mix, wout_ref[...], preferred_element_type=jnp.float32)
    x1_ref[...] = x1

    h2 = _rms(x1, g2_ref[...])
    for c in range(ROW_CHUNKS):
        hflat_ref[pl.ds(c, tile, stride=ROW_CHUNKS), :] = h2[:, c * LANES:(c + 1) * LANES]

    logits = lax.dot_general(wr_ref[...], h2.astype(jnp.bfloat16), _NT,
                             preferred_element_type=jnp.float32) + br_ref[...]
    eio = lax.broadcasted_iota(jnp.int32, logits.shape, 0)
    work = logits
    sel = jnp.zeros(logits.shape, jnp.float32)
    top_v, top_i, hot = [], [], []
    for _ in range(TOP_K):
        mx = jnp.max(work, axis=0, keepdims=True)
        idx = jnp.min(jnp.where(work == mx, eio, N_EXPERTS), axis=0, keepdims=True)
        oh = eio == idx
        top_v.append(mx)
        top_i.append(idx)
        hot.append(oh)
        work = jnp.where(oh, -jnp.inf, work)
        sel = sel + oh.astype(jnp.float32)
    ex = [jnp.exp(v - top_v[0]) for v in top_v]
    den = ex[0] + ex[1] + ex[2] + ex[3]
    ahead = jnp.dot(sel.astype(jnp.bfloat16), tri_ref[...], preferred_element_type=jnp.float32) + cnt_ref[...]
    for j in range(TOP_K):
        topi_ref[j:j + 1, :] = top_i[j]
        gate_ref[j:j + 1, :] = ex[j] / den
        rank_ref[j:j + 1, :] = jnp.sum(jnp.where(hot[j], ahead, 0.0), axis=0, keepdims=True).astype(jnp.int32)
    cnt_ref[...] = cnt_ref[...] + jnp.sum(sel, axis=1, keepdims=True)
    cnt_out_ref[...] = jnp.broadcast_to(cnt_ref[...], cnt_out_ref.shape)


def _mix_route(att, proj, x2d, wpool, pscale, wout, g2, wr_t, br, tri, *, seq):
    n = x2d.shape[0]
    t = TOK_TILE
    nt = n // t
    pcol = QK_WIDTH // POOL_WIDTH
    hb = t // HALO
    last_halo = n // HALO - 1
    const = lambda shape: pl.BlockSpec(shape, lambda i: (0,) * len(shape))
    out_shape = (jax.ShapeDtypeStruct((n, D_MODEL), jnp.float32),
                 jax.ShapeDtypeStruct((n * ROW_CHUNKS, LANES), jnp.float32),
                 jax.ShapeDtypeStruct((TOP_K, n), jnp.int32),
                 jax.ShapeDtypeStruct((TOP_K, n), jnp.int32),
                 jax.ShapeDtypeStruct((TOP_K, n), jnp.float32),
                 jax.ShapeDtypeStruct((N_EXPERTS, LANES), jnp.float32))
    return pl.pallas_call(
        functools.partial(_mix_route_kernel, tile=t, seq=seq),
        out_shape=out_shape,
        grid=(nt,),
        in_specs=[pl.BlockSpec((t, ATT_WIDTH), lambda i: (i, 0)),
                  pl.BlockSpec((t, POOL_WIDTH), lambda i: (i, pcol)),
                  pl.BlockSpec((HALO, POOL_WIDTH), lambda i: (jnp.maximum(i * hb - 1, 0), pcol)),
                  pl.BlockSpec((HALO, POOL_WIDTH), lambda i: (jnp.minimum((i + 1) * hb, last_halo), pcol)),
                  pl.BlockSpec((t, D_MODEL), lambda i: (i, 0)),
                  const((len(POOL_WINDOWS), POOL_GROUP, POOL_GROUP)),
                  const((1, POOL_WIDTH)),
                  const((D_MODEL, D_MODEL)),
                  const((1, D_MODEL)),
                  const((N_EXPERTS, D_MODEL)),
                  const((N_EXPERTS, 1)),
                  const((t, t))],
        out_specs=(pl.BlockSpec((t, D_MODEL), lambda i: (i, 0)),
                   pl.BlockSpec((t * ROW_CHUNKS, LANES), lambda i: (i, 0)),
                   pl.BlockSpec((TOP_K, t), lambda i: (0, i)),
                   pl.BlockSpec((TOP_K, t), lambda i: (0, i)),
                   pl.BlockSpec((TOP_K, t), lambda i: (0, i)),
                   pl.BlockSpec((N_EXPERTS, LANES), lambda i: (0, 0))),
        scratch_shapes=[pltpu.VMEM((N_EXPERTS, 1), jnp.float32)],
        compiler_params=pltpu.CompilerParams(dimension_semantics=("arbitrary",),
                                             vmem_limit_bytes=VMEM_LIMIT),
        name="mix_out_route",
    )(att, proj, proj, proj, x2d, wpool, pscale.reshape(1, -1), wout, g2.reshape(1, -1), wr_t,
      br.reshape(-1, 1), tri)


def _row(ref, r):
    return ref.at[pl.ds(pl.multiple_of(r * ROW_CHUNKS, ROW_CHUNKS), ROW_CHUNKS), :]


def _dispatch_kernel(pad_lo_ref, pad_n_ref, dest_hbm, h_ref, xs_out, idx_ref, zero_ref, idx_sem, row_sem, pad_sem,
                     *, tile):
    i = pl.program_id(0)
    per = tile * TOP_K

    @pl.when(i == 0)
    def _():
        zero_ref[...] = jnp.zeros(zero_ref.shape, zero_ref.dtype)

        def each_pad_row(fn):
            def per_range(e, c):
                lax.fori_loop(0, pad_n_ref[e], lambda r, c2: (fn(pad_lo_ref[e] + r), c2)[1], 0)
                return c
            lax.fori_loop(0, N_EXPERTS + 1, per_range, 0)

        each_pad_row(lambda r: pltpu.make_async_copy(zero_ref, _row(xs_out, r), pad_sem).start())
        each_pad_row(lambda r: pltpu.make_async_copy(zero_ref, _row(xs_out, r), pad_sem).wait())

    idx_copy = pltpu.make_async_copy(dest_hbm.at[pl.ds(pl.multiple_of(i * per, per), per)], idx_ref, idx_sem)
    idx_copy.start()
    idx_copy.wait()

    def body(t, c):
        src = _row(h_ref, t)
        for j in range(TOP_K):
            pltpu.make_async_copy(src, _row(xs_out, idx_ref[t * TOP_K + j]), row_sem).start(priority=j % 2)
        return c

    lax.fori_loop(0, tile, body, 0)
    for _ in range(TOP_K):
        pltpu.make_async_copy(h_ref, xs_out.at[pl.ds(0, tile * ROW_CHUNKS), :], row_sem).wait()


def _dispatch(pad_lo, pad_n, dest_flat, hflat, n_rows):
    t = DISPATCH_TILE
    per = t * TOP_K
    nt = dest_flat.shape[0] // per
    return pl.pallas_call(
        functools.partial(_dispatch_kernel, tile=t),
        out_shape=jax.ShapeDtypeStruct((n_rows * ROW_CHUNKS, LANES), jnp.float32),
        grid_spec=pltpu.PrefetchScalarGridSpec(
            num_scalar_prefetch=2,
            grid=(nt,),
            in_specs=[pl.BlockSpec(memory_space=pl.ANY),
                      pl.BlockSpec((t * ROW_CHUNKS, LANES), lambda i, lo, cnt: (i, 0))],
            out_specs=pl.BlockSpec(memory_space=pl.ANY),
            scratch_shapes=[pltpu.SMEM((per,), jnp.int32),
                            pltpu.VMEM((ROW_CHUNKS, LANES), jnp.float32),
                            pltpu.SemaphoreType.DMA,
                            pltpu.SemaphoreType.DMA,
                            pltpu.SemaphoreType.DMA]),
        compiler_params=pltpu.CompilerParams(dimension_semantics=("arbitrary",),
                                             vmem_limit_bytes=VMEM_LIMIT),
        name="moe_dispatch",
    )(pad_lo, pad_n, dest_flat, hflat)


def _expert_kernel(be_ref, used_ref, x_ref, wg_ref, bg_ref, wu_ref, bu_ref, wd_ref, bd_ref, y_ref,
                   wg_bf, wu_bf, wd_bf, *, rows):
    b = pl.program_id(0)
    live = b < used_ref[0]

    @pl.when(jnp.logical_not(live))
    def _():
        y_ref[...] = jnp.zeros(y_ref.shape, y_ref.dtype)

    @pl.when(jnp.logical_and(live, jnp.logical_or(b == 0, be_ref[b] != be_ref[jnp.maximum(b - 1, 0)])))
    def _():
        wg_bf[...] = wg_ref[0, 0].astype(jnp.bfloat16)
        wu_bf[...] = wu_ref[0, 0].astype(jnp.bfloat16)
        wd_bf[...] = wd_ref[0, 0].astype(jnp.bfloat16)

    @pl.when(live)
    def _():
        x = jnp.concatenate([x_ref[pl.ds(c, rows, stride=ROW_CHUNKS), :] for c in range(ROW_CHUNKS)],
                            axis=1).astype(jnp.bfloat16)
        g = jnp.dot(x, wg_bf[...], preferred_element_type=jnp.float32) + bg_ref[0, 0]
        u = jnp.dot(x, wu_bf[...], preferred_element_type=jnp.float32) + bu_ref[0, 0]
        g = jnp.minimum(g, SWIGLU_LIMIT)
        u = jnp.clip(u, -SWIGLU_LIMIT, SWIGLU_LIMIT)
        act = (u + 1.0) * (g * (1.0 / (1.0 + jnp.exp(-SWIGLU_ALPHA * g))))
        y = jnp.dot(act.astype(jnp.bfloat16), wd_bf[...], preferred_element_type=jnp.float32) + bd_ref[0, 0]
        for c in range(ROW_CHUNKS):
            y_ref[pl.ds(c, rows, stride=ROW_CHUNKS), :] = y[:, c * LANES:(c + 1) * LANES]


def _experts(block_expert, n_used, xs, wg, bg, wu, bu, wd, bd, *, layer):
    nb = block_expert.shape[0]
    rows = EXPERT_ROWS
    depth = wg.shape[0]
    blk = lambda b, be, used: (jnp.minimum(b, used[0] - 1), 0)
    wsel = lambda b, be, used: (layer, be[b], 0, 0)
    wspec = pl.BlockSpec((1, 1, D_MODEL, D_MODEL), wsel)
    bspec = pl.BlockSpec((1, 1, 1, D_MODEL), wsel)
    bias4 = lambda a: a.reshape(depth, N_EXPERTS, 1, D_MODEL)
    return pl.pallas_call(
        functools.partial(_expert_kernel, rows=rows),
        out_shape=jax.ShapeDtypeStruct(xs.shape, jnp.float32),
        grid_spec=pltpu.PrefetchScalarGridSpec(
            num_scalar_prefetch=2,
            grid=(nb,),
            in_specs=[pl.BlockSpec((rows * ROW_CHUNKS, LANES), blk),
                      wspec, bspec, wspec, bspec, wspec, bspec],
            out_specs=pl.BlockSpec((rows * ROW_CHUNKS, LANES), lambda b, be, used: (b, 0)),
            scratch_shapes=[pltpu.VMEM((D_MODEL, D_MODEL), jnp.bfloat16)] * 3),
        compiler_params=pltpu.CompilerParams(dimension_semantics=("arbitrary",),
                                             vmem_limit_bytes=VMEM_LIMIT),
        name="moe_experts",
    )(block_expert, n_used, xs, wg, bias4(bg), wu, bias4(bu), wd, bias4(bd))


def _combine_kernel(dest_hbm, ys_hbm, x_ref, gate_ref, gf_ref, o_ref, idx_ref, buf_ref, idx_sem, row_sem,
                    *, tile, final):
    i = pl.program_id(0)
    per = tile * TOP_K

    def gather(step, slot):
        idx_copy = pltpu.make_async_copy(dest_hbm.at[pl.ds(pl.multiple_of(step * per, per), per)],
                                         idx_ref.at[slot], idx_sem)
        idx_copy.start()
        idx_copy.wait()

        def body(t, c):
            for j in range(TOP_K):
                pltpu.make_async_copy(_row(ys_hbm, idx_ref[slot, t * TOP_K + j]), _row(buf_ref.at[slot, j], t),
                                      row_sem.at[slot]).start(priority=j % 2)
            return c

        lax.fori_loop(0, tile, body, 0)

    @pl.when(i == 0)
    def _():
        gather(0, 0)

    @pl.when(i + 1 < pl.num_programs(0))
    def _():
        gather(i + 1, (i + 1) % 2)

    slot = i % 2
    for j in range(TOP_K):
        pltpu.make_async_copy(ys_hbm.at[pl.ds(0, tile * ROW_CHUNKS), :], buf_ref.at[slot, j],
                              row_sem.at[slot]).wait()

    gates = gate_ref[...]
    cols = []
    for c in range(ROW_CHUNKS):
        acc = x_ref[:, c * LANES:(c + 1) * LANES]
        for j in range(TOP_K):
            acc = acc + buf_ref[slot, j, pl.ds(c, tile, stride=ROW_CHUNKS), :] * gates[:, j:j + 1]
        cols.append(acc)
    x2 = jnp.concatenate(cols, axis=1)
    o_ref[...] = _rms(x2, gf_ref[...]) if final else x2


def _combine(dest_flat, ys, x1, gates_tok, g_final, *, final):
    t = MOVE_TILE
    per = t * TOP_K
    nt = dest_flat.shape[0] // per
    n = x1.shape[0]
    return pl.pallas_call(
        functools.partial(_combine_kernel, tile=t, final=final),
        out_shape=jax.ShapeDtypeStruct((n, D_MODEL), jnp.float32),
        grid=(nt,),
        in_specs=[pl.BlockSpec(memory_space=pl.ANY),
                  pl.BlockSpec(memory_space=pl.ANY),
                  pl.BlockSpec((t, D_MODEL), lambda i: (i, 0)),
                  pl.BlockSpec((t, TOP_K), lambda i: (i, 0)),
                  pl.BlockSpec((1, D_MODEL), lambda i: (0, 0))],
        out_specs=pl.BlockSpec((t, D_MODEL), lambda i: (i, 0)),
        scratch_shapes=[pltpu.SMEM((2, per), jnp.int32),
                        pltpu.VMEM((2, TOP_K, t * ROW_CHUNKS, LANES), jnp.float32),
                        pltpu.SemaphoreType.DMA,
                        pltpu.SemaphoreType.DMA((2,))],
        compiler_params=pltpu.CompilerParams(dimension_semantics=("arbitrary",),
                                             vmem_limit_bytes=VMEM_LIMIT),
        name="moe_combine",
    )(dest_flat, ys, x1, gates_tok, g_final.reshape(1, -1))


def kernel(x, rel_table, norm1, w_in, lambda_q1, lambda_k1, lambda_q2, lambda_k2, subln_g, w_pool, pool_scale,
           w_out, norm2, w_router, b_router, w_gate, b_gate, w_up, b_up, w_down, b_down, final_norm):
    batch, seq, d = x.shape
    depth = norm1.shape[0]
    n = batch * seq
    assert d == D_MODEL and seq % ATT_TILE == 0 and seq % TOK_TILE == 0
    assert n % MOVE_TILE == 0 and n % DISPATCH_TILE == 0
    bf = jnp.bfloat16

    w_q = w_in[:, :, :QK_WIDTH] * (QK_DIM ** -0.5 * LOG2E)
    w_k = w_in[:, :, QK_WIDTH:2 * QK_WIDTH]
    w_v = w_in[:, :, 2 * QK_WIDTH:2 * QK_WIDTH + ATT_WIDTH]
    w_p = w_in[:, :, 2 * QK_WIDTH + ATT_WIDTH:]
    w_kp = jnp.concatenate([w_k, w_p], axis=2).astype(bf)
    w_qvt = jnp.swapaxes(jnp.concatenate([w_q, w_v], axis=2), 1, 2).astype(bf)
    w_out_b = w_out.astype(bf)
    w_pool_b = w_pool.astype(bf)
    wr_t = jnp.swapaxes(w_router, 1, 2).astype(bf)
    tri = (lax.broadcasted_iota(jnp.int32, (TOK_TILE, TOK_TILE), 0)
           < lax.broadcasted_iota(jnp.int32, (TOK_TILE, TOK_TILE), 1)).astype(bf)

    bias = _bias_tiles(rel_table, ATT_TILE)
    far = jnp.stack([rel_table[N_BUCKETS // 2 - 1], rel_table[N_BUCKETS - 1]], axis=1).reshape(-1)

    n_assign = n * TOP_K
    n_blocks = -(-(n_assign + N_EXPERTS * (EXPERT_ROWS - 1)) // EXPERT_ROWS)
    n_rows = n_blocks * EXPERT_ROWS
    block_start = jnp.arange(n_blocks, dtype=jnp.int32) * EXPERT_ROWS

    x2d = x.reshape(n, d)
    for l in range(depth):
        lam_init = 0.8 - 0.6 * math.exp(-0.3 * l)
        kp, qvt = _norm_proj(x2d, norm1[l], w_kp[l], w_qvt[l])
        att = _attention(kp, qvt, bias, far, lambda_q1[l], lambda_k1[l], lambda_q2[l], lambda_k2[l], subln_g[l],
                         batch=batch, seq=seq, lam_init=lam_init)
        x1, hflat, topi, rank, gates, counts = _mix_route(
            att, kp, x2d, w_pool_b[l], pool_scale[l], w_out_b[l], norm2[l], wr_t[l], b_router[l], tri, seq=seq)

        cnt = counts[:, 0].astype(jnp.int32)
        padded = ((cnt + EXPERT_ROWS - 1) // EXPERT_ROWS) * EXPERT_ROWS
        pad_end = jnp.cumsum(padded)
        experts = jnp.arange(N_EXPERTS, dtype=jnp.int32)
        seg_start = jnp.sum(jnp.where(topi[..., None] == experts, pad_end - padded, 0), axis=-1)
        dest_flat = (seg_start + rank).T.reshape(-1)
        block_expert = jnp.minimum(jnp.sum(pad_end[None, :] <= block_start[:, None], axis=1),
                                   N_EXPERTS - 1).astype(jnp.int32)
        n_used = (pad_end[-1:] // EXPERT_ROWS).astype(jnp.int32)

        pad_lo = jnp.concatenate([pad_end - padded + cnt, pad_end[-1:]])
        pad_n = jnp.concatenate([padded - cnt, n_rows - pad_end[-1:]])
        xs = _dispatch(pad_lo, pad_n, dest_flat, hflat, n_rows)
        ys = _experts(block_expert, n_used, xs, w_gate, b_gate, w_up, b_up, w_down, b_down, layer=l)
        x2d = _combine(dest_flat, ys, x1, gates.T, final_norm, final=(l == depth - 1))
    return x2d.reshape(batch, seq, d)
```

```python
import functools
import math

import jax
import jax.numpy as jnp
from jax import lax
from jax.experimental import pallas as pl
from jax.experimental.pallas import tpu as pltpu

D_MODEL = 1024
N_HEADS = 4
QK_DIM = 64
V_DIM = 128
QK_WIDTH = N_HEADS * 2 * QK_DIM
ATT_WIDTH = N_HEADS * V_DIM
POOL_WINDOWS = (2, 4, 8, 16)
POOL_GROUP = 128
POOL_WIDTH = 512
IN_WIDTH = 2 * QK_WIDTH + ATT_WIDTH + POOL_WIDTH
N_BUCKETS = 32
MAX_DISTANCE = 128
N_EXPERTS = 32
TOP_K = 4
SWIGLU_LIMIT = 7.0
SWIGLU_ALPHA = 1.702
EPS = 1e-5

LANES = 128
SUBLANES = 8
ROW_CHUNKS = D_MODEL // LANES
VMEM_LIMIT = 56 * 1024 * 1024

TOK_TILE = 512
ATT_TILE = 512
ATT_CHUNK = 256
BIAS_TILES = 5
NEAR_TILES = 4
V_EXT = V_DIM + 16
HALO = 16
EXPERT_ROWS = 256
DISPATCH_TILE = 1024
COMBINE_CHUNK = 64
SLAB_CHUNKS = 4
MAX_CHUNKS = TOK_TILE * TOP_K // COMBINE_CHUNK + N_EXPERTS

_NT = (((1,), (1,)), ((), ()))
LOG2E = math.log2(math.e)


def _rms(x, g):
    return x * lax.rsqrt(jnp.mean(x * x, axis=-1, keepdims=True) + EPS) * g


def _bias_kernel(tab_ref, out_ref, *, tile):
    h = pl.program_id(0)
    d = pl.program_id(1)
    key = lax.broadcasted_iota(jnp.int32, (tile, tile), 0)
    qry = lax.broadcasted_iota(jnp.int32, (tile, tile), 1)
    rel = (d - BIAS_TILES // 2) * tile + key - qry
    nb = N_BUCKETS // 2
    max_exact = nb // 2
    ret = jnp.where(rel > 0, nb, 0)
    n = jnp.abs(rel)
    n_f = jnp.maximum(n, 1).astype(jnp.float32)
    large = max_exact + (jnp.log(n_f / max_exact) / math.log(MAX_DISTANCE / max_exact)
                         * (nb - max_exact)).astype(jnp.int32)
    large = jnp.minimum(large, nb - 1)
    bucket = ret + jnp.where(n < max_exact, n, large)
    acc = jnp.zeros((tile, tile), jnp.float32)
    for b in range(N_BUCKETS):
        acc = jnp.where(bucket == b, tab_ref[b * N_HEADS + h], acc)
    out_ref[0, 0] = acc * LOG2E


def _bias_tiles(rel_table, tile):
    return pl.pallas_call(
        functools.partial(_bias_kernel, tile=tile),
        out_shape=jax.ShapeDtypeStruct((N_HEADS, BIAS_TILES, tile, tile), jnp.float32),
        grid=(N_HEADS, BIAS_TILES),
        in_specs=[pl.BlockSpec(memory_space=pltpu.SMEM)],
        out_specs=pl.BlockSpec((1, 1, tile, tile), lambda h, d: (h, d, 0, 0)),
        name="rel_bias_tiles",
    )(rel_table.reshape(-1))


def _norm_proj_kernel(x_ref, g_ref, wkp_ref, wqvt_ref, kp_ref, qvt_ref):
    h = _rms(x_ref[...], g_ref[...]).astype(jnp.bfloat16)
    kp_ref[...] = jnp.dot(h, wkp_ref[...], preferred_element_type=jnp.float32).astype(kp_ref.dtype)
    qvt_ref[...] = lax.dot_general(wqvt_ref[...], h, _NT,
                                   preferred_element_type=jnp.float32).astype(qvt_ref.dtype)


def _norm_proj(x2d, g, w_kp, w_qvt):
    n = x2d.shape[0]
    width = QK_WIDTH + POOL_WIDTH
    width_t = QK_WIDTH + ATT_WIDTH
    return pl.pallas_call(
        _norm_proj_kernel,
        out_shape=(jax.ShapeDtypeStruct((n, width), jnp.bfloat16),
                   jax.ShapeDtypeStruct((width_t, n), jnp.bfloat16)),
        grid=(n // TOK_TILE,),
        in_specs=[pl.BlockSpec((TOK_TILE, D_MODEL), lambda i: (i, 0)),
                  pl.BlockSpec((1, D_MODEL), lambda i: (0, 0)),
                  pl.BlockSpec((D_MODEL, width), lambda i: (0, 0)),
                  pl.BlockSpec((width_t, D_MODEL), lambda i: (0, 0))],
        out_specs=(pl.BlockSpec((TOK_TILE, width), lambda i: (i, 0)),
                   pl.BlockSpec((width_t, TOK_TILE), lambda i: (0, i))),
        compiler_params=pltpu.CompilerParams(dimension_semantics=("parallel",),
                                             vmem_limit_bytes=VMEM_LIMIT),
        name="norm_in_proj",
    )(x2d, g.reshape(1, D_MODEL), w_kp, w_qvt)


def _attn_kernel(far_ref, qt_ref, k_ref, vt_ref, bias_ref, lq1_ref, lk1_ref, lq2_ref, lk2_ref, sg_ref,
                 o_ref, qs_ref, vte_ref, s0_ref, s1_ref, mc0_ref, mc1_ref, m_ref, acc_ref,
                 *, tile, n_kv, lam_init):
    h = pl.program_id(1)
    qi = pl.program_id(2)
    n_far = n_kv - NEAR_TILES

    @pl.when(qi == 0)
    def _():
        vte_ref[:V_DIM, :] = vt_ref[...]
        vte_ref[V_DIM:, :] = jnp.ones((V_EXT - V_DIM, vte_ref.shape[1]), vte_ref.dtype)

    qt = qt_ref[...]
    row = lax.broadcasted_iota(jnp.int32, qt.shape, 0)
    zero = jnp.zeros_like(qt)
    qs_ref[:, :tile] = jnp.where(row < QK_DIM, qt, zero)
    qs_ref[:, tile:] = jnp.where(row >= QK_DIM, qt, zero)

    m_ref[...] = jnp.full(m_ref.shape, -jnp.inf, jnp.float32)
    acc_ref[...] = jnp.zeros(acc_ref.shape, jnp.float32)
    chunks = [slice(c * ATT_CHUNK, (c + 1) * ATT_CHUNK) for c in range(2 * tile // ATT_CHUNK)]
    w0 = jnp.clip(qi - 1, 0, n_kv - NEAR_TILES)
    c_below = far_ref[2 * h] * LOG2E
    c_above = far_ref[2 * h + 1] * LOG2E

    def far_tile(f):
        f = jnp.minimum(f, n_far - 1)
        return jnp.where(f < w0, f, f + NEAR_TILES)

    def far_const(ki):
        return jnp.where(ki < qi, c_below, c_above)

    def logits(ki, cols):
        k = k_ref[pl.ds(pl.multiple_of(ki * tile, tile), tile), :]
        return jnp.dot(k, qs_ref[:, cols], preferred_element_type=jnp.float32)

    def scores_near(ki, cols, s_ref, mc_ref):
        d = jnp.clip(ki - qi, -2, 2) + 2
        q0 = cols.start % tile
        s = logits(ki, cols) + bias_ref[0, d, :, q0:q0 + ATT_CHUNK]
        s_ref[:, cols] = s
        mc_ref[:, cols] = jnp.max(s, axis=0, keepdims=True)

    def scores_far(ki, cols, s_ref, mc_ref):
        s = logits(ki, cols)
        s_ref[:, cols] = s
        mc_ref[:, cols] = jnp.max(s, axis=0, keepdims=True) + far_const(ki)

    def update(ki, cols, s_ref, mc_ref, const):
        vt = vte_ref[:, pl.ds(pl.multiple_of(ki * tile, tile), tile)]
        m_prev = m_ref[:, cols]
        m_new = jnp.maximum(m_prev, mc_ref[:, cols])
        alpha = jnp.exp2(m_prev - m_new)
        p = jnp.exp2(s_ref[:, cols] - (m_new - const))
        acc_ref[:, cols] = alpha * acc_ref[:, cols] + jnp.dot(vt, p.astype(jnp.bfloat16),
                                                              preferred_element_type=jnp.float32)
        m_ref[:, cols] = m_new

    bufs = ((s0_ref, mc0_ref), (s1_ref, mc1_ref))
    for cols in chunks:
        scores_near(w0, cols, *bufs[0])
    for i in range(NEAR_TILES):
        for cols in chunks:
            if i + 1 < NEAR_TILES:
                scores_near(w0 + i + 1, cols, *bufs[(i + 1) % 2])
            elif n_far > 0:
                scores_far(far_tile(0), cols, *bufs[(i + 1) % 2])
            update(w0 + i, cols, *bufs[i % 2], 0.0)

    def pair(j, carry):
        ka, kb, kc = far_tile(2 * j), far_tile(2 * j + 1), far_tile(2 * j + 2)
        for cols in chunks:
            scores_far(kb, cols, *bufs[(NEAR_TILES + 1) % 2])
            update(ka, cols, *bufs[NEAR_TILES % 2], far_const(ka))
        for cols in chunks:
            scores_far(kc, cols, *bufs[NEAR_TILES % 2])
            update(kb, cols, *bufs[(NEAR_TILES + 1) % 2], far_const(kb))
        return carry

    lax.fori_loop(0, n_far // 2, pair, 0)

    acc = acc_ref[...]
    o = acc[:V_DIM] / acc[V_DIM:V_DIM + 1]
    lam = (jnp.exp(jnp.sum(lq1_ref[...] * lk1_ref[...], axis=1, keepdims=True))
           - jnp.exp(jnp.sum(lq2_ref[...] * lk2_ref[...], axis=1, keepdims=True)) + lam_init)
    o = o[:, :tile] - lam * o[:, tile:]
    o = o * lax.rsqrt(jnp.mean(o * o, axis=0, keepdims=True) + EPS) * (sg_ref[...] * (1.0 - lam_init))
    o_ref[...] = o.T.astype(o_ref.dtype)


def _attention(kp, qvt, bias, far, lq1, lk1, lq2, lk2, sub_g, *, batch, seq, lam_init):
    t = ATT_TILE
    nq = seq // t
    assert nq >= NEAR_TILES and (nq - NEAR_TILES) % 2 == 0
    vrow = QK_WIDTH // LANES
    vec = lambda width: pl.BlockSpec((1, width), lambda b, h, qi: (0, 0))
    return pl.pallas_call(
        functools.partial(_attn_kernel, tile=t, n_kv=nq, lam_init=lam_init),
        out_shape=jax.ShapeDtypeStruct((batch * seq, ATT_WIDTH), jnp.bfloat16),
        grid=(batch, N_HEADS, nq),
        in_specs=[pl.BlockSpec(memory_space=pltpu.SMEM),
                  pl.BlockSpec((LANES, t), lambda b, h, qi: (h, b * nq + qi)),
                  pl.BlockSpec((seq, LANES), lambda b, h, qi: (b, h)),
                  pl.BlockSpec((V_DIM, seq), lambda b, h, qi: (vrow + h, b)),
                  pl.BlockSpec((1, BIAS_TILES, t, t), lambda b, h, qi: (h, 0, 0, 0)),
                  vec(QK_DIM), vec(QK_DIM), vec(QK_DIM), vec(QK_DIM),
                  pl.BlockSpec((V_DIM, 1), lambda b, h, qi: (0, 0))],
        out_specs=pl.BlockSpec((t, V_DIM), lambda b, h, qi: (b * nq + qi, h)),
        scratch_shapes=[pltpu.VMEM((LANES, 2 * t), jnp.bfloat16),
                        pltpu.VMEM((V_EXT, seq), jnp.bfloat16),
                        pltpu.VMEM((t, 2 * t), jnp.float32),
                        pltpu.VMEM((t, 2 * t), jnp.float32),
                        pltpu.VMEM((1, 2 * t), jnp.float32),
                        pltpu.VMEM((1, 2 * t), jnp.float32),
                        pltpu.VMEM((1, 2 * t), jnp.float32),
                        pltpu.VMEM((V_EXT, 2 * t), jnp.float32)],
        compiler_params=pltpu.CompilerParams(dimension_semantics=("arbitrary", "arbitrary", "arbitrary"),
                                             vmem_limit_bytes=VMEM_LIMIT),
        name="diff_attention",
    )(far, qvt, kp, qvt, bias, lq1.reshape(1, -1), lk1.reshape(1, -1), lq2.reshape(1, -1),
      lk2.reshape(1, -1), sub_g.reshape(-1, 1))


def _mix_route_kernel(att_ref, p_ref, prev_ref, next_ref, x_ref, wpool_ref, pscale_ref, wout_ref, g2_ref,
                      wr_ref, br_ref, tri_ref,
                      x1_ref, hflat_ref, topi_ref, rank_ref, gate_ref, base_ref, cnt_out_ref, cnt_ref,
                      *, tile, seq):
    i = pl.program_id(0)
    tiles_per_seq = seq // tile
    si = i % tiles_per_seq

    @pl.when(i == 0)
    def _():
        cnt_ref[...] = jnp.zeros(cnt_ref.shape, jnp.float32)

    pc = p_ref[...].astype(jnp.float32)
    pp = jnp.where(si == 0, 0.0, prev_ref[...].astype(jnp.float32))
    pn = jnp.where(si == tiles_per_seq - 1, 0.0, next_ref[...].astype(jnp.float32))
    pe = jnp.concatenate([pp, pc, pn], axis=0)
    n_ext = tile + 2 * HALO
    pos = si * tile + lax.broadcasted_iota(jnp.int32, (tile, 1), 0)
    pooled = []
    for g, win in enumerate(POOL_WINDOWS):
        xg = pe[:, g * POOL_GROUP:(g + 1) * POOL_GROUP]
        w = xg + pltpu.roll(xg, 1, 0)
        half = 1
        while 2 * half < win:
            w = pltpu.roll(w, half, 0) + pltpu.roll(w, n_ext - half, 0)
            half *= 2
        lo = jnp.maximum(pos - win // 2, 0)
        hi = jnp.minimum(pos - win // 2 + win - 1, seq - 1)
        mean = w[HALO:HALO + tile] / (hi - lo + 1).astype(jnp.float32)
        dg = (mean - pc[:, g * POOL_GROUP:(g + 1) * POOL_GROUP]).astype(jnp.bfloat16)
        pooled.append(jnp.dot(dg, wpool_ref[g], preferred_element_type=jnp.float32))
    pool = jnp.concatenate(pooled, axis=1) * pscale_ref[...]

    mix = jnp.concatenate([att_ref[...], pool.astype(jnp.bfloat16)], axis=1)
    x1 = x_ref[...] + jnp.dot(mix, wout_ref[...], preferred_element_type=jnp.float32)
    x1_ref[...] = x1

    h2 = _rms(x1, g2_ref[...])
    for c in range(ROW_CHUNKS):
        hflat_ref[pl.ds(c, tile, stride=ROW_CHUNKS), :] = h2[:, c * LANES:(c + 1) * LANES]

    logits = lax.dot_general(wr_ref[...], h2.astype(jnp.bfloat16), _NT,
                             preferred_element_type=jnp.float32) + br_ref[...]
    eio = lax.broadcasted_iota(jnp.int32, logits.shape, 0)
    work = logits
    sel = jnp.zeros(logits.shape, jnp.float32)
    top_v, top_i, hot = [], [], []
    for _ in range(TOP_K):
        mx = jnp.max(work, axis=0, keepdims=True)
        idx = jnp.min(jnp.where(work == mx, eio, N_EXPERTS), axis=0, keepdims=True)
        oh = eio == idx
        top_v.append(mx)
        top_i.append(idx)
        hot.append(oh)
        work = jnp.where(oh, -jnp.inf, work)
        sel = sel + oh.astype(jnp.float32)
    ex = [jnp.exp(v - top_v[0]) for v in top_v]
    den = ex[0] + ex[1] + ex[2] + ex[3]
    ahead = jnp.dot(sel.astype(jnp.bfloat16), tri_ref[...], preferred_element_type=jnp.float32) + cnt_ref[...]
    for j in range(TOP_K):
        topi_ref[j:j + 1, :] = top_i[j]
        gate_ref[j:j + 1, :] = ex[j] / den
        rank_ref[j:j + 1, :] = jnp.sum(jnp.where(hot[j], ahead, 0.0), axis=0, keepdims=True).astype(jnp.int32)
    base_ref[0] = jnp.broadcast_to(cnt_ref[...], base_ref.shape[1:])
    cnt_ref[...] = cnt_ref[...] + jnp.sum(sel, axis=1, keepdims=True)
    cnt_out_ref[...] = jnp.broadcast_to(cnt_ref[...], cnt_out_ref.shape)


def _mix_route(att, proj, x2d, wpool, pscale, wout, g2, wr_t, br, tri, *, seq):
    n = x2d.shape[0]
    t = TOK_TILE
    nt = n // t
    pcol = QK_WIDTH // POOL_WIDTH
    hb = t // HALO
    last_halo = n // HALO - 1
    const = lambda shape: pl.BlockSpec(shape, lambda i: (0,) * len(shape))
    out_shape = (jax.ShapeDtypeStruct((n, D_MODEL), jnp.float32),
                 jax.ShapeDtypeStruct((n * ROW_CHUNKS, LANES), jnp.float32),
                 jax.ShapeDtypeStruct((TOP_K, n), jnp.int32),
                 jax.ShapeDtypeStruct((TOP_K, n), jnp.int32),
                 jax.ShapeDtypeStruct((TOP_K, n), jnp.float32),
                 jax.ShapeDtypeStruct((nt, N_EXPERTS, LANES), jnp.float32),
                 jax.ShapeDtypeStruct((N_EXPERTS, LANES), jnp.float32))
    return pl.pallas_call(
        functools.partial(_mix_route_kernel, tile=t, seq=seq),
        out_shape=out_shape,
        grid=(nt,),
        in_specs=[pl.BlockSpec((t, ATT_WIDTH), lambda i: (i, 0)),
                  pl.BlockSpec((t, POOL_WIDTH), lambda i: (i, pcol)),
                  pl.BlockSpec((HALO, POOL_WIDTH), lambda i: (jnp.maximum(i * hb - 1, 0), pcol)),
                  pl.BlockSpec((HALO, POOL_WIDTH), lambda i: (jnp.minimum((i + 1) * hb, last_halo), pcol)),
                  pl.BlockSpec((t, D_MODEL), lambda i: (i, 0)),
                  const((len(POOL_WINDOWS), POOL_GROUP, POOL_GROUP)),
                  const((1, POOL_WIDTH)),
                  const((D_MODEL, D_MODEL)),
                  const((1, D_MODEL)),
                  const((N_EXPERTS, D_MODEL)),
                  const((N_EXPERTS, 1)),
                  const((t, t))],
        out_specs=(pl.BlockSpec((t, D_MODEL), lambda i: (i, 0)),
                   pl.BlockSpec((t * ROW_CHUNKS, LANES), lambda i: (i, 0)),
                   pl.BlockSpec((TOP_K, t), lambda i: (0, i)),
                   pl.BlockSpec((TOP_K, t), lambda i: (0, i)),
                   pl.BlockSpec((TOP_K, t), lambda i: (0, i)),
                   pl.BlockSpec((1, N_EXPERTS, LANES), lambda i: (i, 0, 0)),
                   pl.BlockSpec((N_EXPERTS, LANES), lambda i: (0, 0))),
        scratch_shapes=[pltpu.VMEM((N_EXPERTS, 1), jnp.float32)],
        compiler_params=pltpu.CompilerParams(dimension_semantics=("arbitrary",),
                                             vmem_limit_bytes=VMEM_LIMIT),
        name="mix_out_route",
    )(att, proj, proj, proj, x2d, wpool, pscale.reshape(1, -1), wout, g2.reshape(1, -1), wr_t,
      br.reshape(-1, 1), tri)


def _row(ref, r):
    return ref.at[pl.ds(pl.multiple_of(r * ROW_CHUNKS, ROW_CHUNKS), ROW_CHUNKS), :]


def _dispatch_kernel(pad_lo_ref, pad_n_ref, dest_hbm, h_ref, xs_out, idx_ref, zero_ref, idx_sem, row_sem, pad_sem,
                     *, tile):
    i = pl.program_id(0)
    per = tile * TOP_K

    @pl.when(i == 0)
    def _():
        zero_ref[...] = jnp.zeros(zero_ref.shape, zero_ref.dtype)

        def each_pad_row(fn):
            def per_range(e, c):
                lax.fori_loop(0, pad_n_ref[e], lambda r, c2: (fn(pad_lo_ref[e] + r), c2)[1], 0)
                return c
            lax.fori_loop(0, N_EXPERTS + 1, per_range, 0)

        each_pad_row(lambda r: pltpu.make_async_copy(zero_ref, _row(xs_out, r), pad_sem).start())
        each_pad_row(lambda r: pltpu.make_async_copy(zero_ref, _row(xs_out, r), pad_sem).wait())

    idx_copy = pltpu.make_async_copy(dest_hbm.at[pl.ds(pl.multiple_of(i * per, per), per)], idx_ref, idx_sem)
    idx_copy.start()
    idx_copy.wait()

    def body(t, c):
        src = _row(h_ref, t)
        for j in range(TOP_K):
            pltpu.make_async_copy(src, _row(xs_out, idx_ref[t * TOP_K + j]), row_sem).start(priority=j % 2)
        return c

    lax.fori_loop(0, tile, body, 0)
    for _ in range(TOP_K):
        pltpu.make_async_copy(h_ref, xs_out.at[pl.ds(0, tile * ROW_CHUNKS), :], row_sem).wait()


def _dispatch(pad_lo, pad_n, dest_flat, hflat, n_rows):
    t = DISPATCH_TILE
    per = t * TOP_K
    nt = dest_flat.shape[0] // per
    return pl.pallas_call(
        functools.partial(_dispatch_kernel, tile=t),
        out_shape=jax.ShapeDtypeStruct((n_rows * ROW_CHUNKS, LANES), jnp.float32),
        grid_spec=pltpu.PrefetchScalarGridSpec(
            num_scalar_prefetch=2,
            grid=(nt,),
            in_specs=[pl.BlockSpec(memory_space=pl.ANY),
                      pl.BlockSpec((t * ROW_CHUNKS, LANES), lambda i, lo, cnt: (i, 0))],
            out_specs=pl.BlockSpec(memory_space=pl.ANY),
            scratch_shapes=[pltpu.SMEM((per,), jnp.int32),
                            pltpu.VMEM((ROW_CHUNKS, LANES), jnp.float32),
                            pltpu.SemaphoreType.DMA,
                            pltpu.SemaphoreType.DMA,
                            pltpu.SemaphoreType.DMA]),
        compiler_params=pltpu.CompilerParams(dimension_semantics=("arbitrary",),
                                             vmem_limit_bytes=VMEM_LIMIT),
        name="moe_dispatch",
    )(pad_lo, pad_n, dest_flat, hflat)


def _expert_kernel(be_ref, used_ref, x_ref, wg_ref, bg_ref, wu_ref, bu_ref, wd_ref, bd_ref, y_ref,
                   wg_bf, wu_bf, wd_bf, *, rows):
    b = pl.program_id(0)
    live = b < used_ref[0]

    @pl.when(jnp.logical_not(live))
    def _():
        y_ref[...] = jnp.zeros(y_ref.shape, y_ref.dtype)

    @pl.when(jnp.logical_and(live, jnp.logical_or(b == 0, be_ref[b] != be_ref[jnp.maximum(b - 1, 0)])))
    def _():
        wg_bf[...] = wg_ref[0, 0].astype(jnp.bfloat16)
        wu_bf[...] = wu_ref[0, 0].astype(jnp.bfloat16)
        wd_bf[...] = wd_ref[0, 0].astype(jnp.bfloat16)

    @pl.when(live)
    def _():
        x = jnp.concatenate([x_ref[pl.ds(c, rows, stride=ROW_CHUNKS), :] for c in range(ROW_CHUNKS)],
                            axis=1).astype(jnp.bfloat16)
        g = jnp.dot(x, wg_bf[...], preferred_element_type=jnp.float32) + bg_ref[0, 0]
        u = jnp.dot(x, wu_bf[...], preferred_element_type=jnp.float32) + bu_ref[0, 0]
        g = jnp.minimum(g, SWIGLU_LIMIT)
        u = jnp.clip(u, -SWIGLU_LIMIT, SWIGLU_LIMIT)
        act = (u + 1.0) * (g * (1.0 / (1.0 + jnp.exp(-SWIGLU_ALPHA * g))))
        y = jnp.dot(act.astype(jnp.bfloat16), wd_bf[...], preferred_element_type=jnp.float32) + bd_ref[0, 0]
        for c in range(ROW_CHUNKS):
            y_ref[pl.ds(c, rows, stride=ROW_CHUNKS), :] = y[:, c * LANES:(c + 1) * LANES]


def _experts(block_expert, n_used, xs, wg, bg, wu, bu, wd, bd, *, layer):
    nb = block_expert.shape[0]
    rows = EXPERT_ROWS
    depth = wg.shape[0]
    blk = lambda b, be, used: (jnp.minimum(b, used[0] - 1), 0)
    wsel = lambda b, be, used: (layer, be[b], 0, 0)
    wspec = pl.BlockSpec((1, 1, D_MODEL, D_MODEL), wsel)
    bspec = pl.BlockSpec((1, 1, 1, D_MODEL), wsel)
    bias4 = lambda a: a.reshape(depth, N_EXPERTS, 1, D_MODEL)
    return pl.pallas_call(
        functools.partial(_expert_kernel, rows=rows),
        out_shape=jax.ShapeDtypeStruct(xs.shape, jnp.float32),
        grid_spec=pltpu.PrefetchScalarGridSpec(
            num_scalar_prefetch=2,
            grid=(nb,),
            in_specs=[pl.BlockSpec((rows * ROW_CHUNKS, LANES), blk),
                      wspec, bspec, wspec, bspec, wspec, bspec],
            out_specs=pl.BlockSpec((rows * ROW_CHUNKS, LANES), lambda b, be, used: (b, 0)),
            scratch_shapes=[pltpu.VMEM((D_MODEL, D_MODEL), jnp.bfloat16)] * 3),
        compiler_params=pltpu.CompilerParams(dimension_semantics=("arbitrary",),
                                             vmem_limit_bytes=VMEM_LIMIT),
        name="moe_experts",
    )(block_expert, n_used, xs, wg, bias4(bg), wu, bias4(bu), wd, bias4(bd))


def _combine_kernel(cstart_ref, clen_ref, npair_ref, ys_hbm, x_ref, dest_ref, gate_ref, gf_ref,
                    o_ref, zbuf_ref, acc_ref, destb_ref, gateb_ref, sem, *, tile, final):
    i = pl.program_id(0)
    n_tiles = pl.num_programs(0)
    slab_rows = SLAB_CHUNKS * COMBINE_CHUNK
    chunk_flat = COMBINE_CHUNK * ROW_CHUNKS

    def slab_copies(first_chunk, slot):
        return [pltpu.make_async_copy(
            ys_hbm.at[pl.ds(pl.multiple_of(cstart_ref[first_chunk + q] * ROW_CHUNKS, ROW_CHUNKS), chunk_flat), :],
            zbuf_ref.at[slot, pl.ds(q * chunk_flat, chunk_flat), :], sem.at[slot]) for q in range(SLAB_CHUNKS)]

    def pair_copies(first_chunk):
        return slab_copies(first_chunk, 0) + slab_copies(first_chunk + SLAB_CHUNKS, 1)

    @pl.when(i == 0)
    def _():
        for cp in pair_copies(0):
            cp.start()

    n_pairs = npair_ref[i]
    acc_ref[...] = x_ref[...]
    for j in range(TOP_K):
        destb_ref[j] = jnp.broadcast_to(dest_ref[:, j:j + 1], (tile, LANES))
        gateb_ref[j] = jnp.broadcast_to(gate_ref[:, j:j + 1], (tile, LANES))
    col = lax.broadcasted_iota(jnp.int32, (1, slab_rows), 1)
    col_chunk = col // COMBINE_CHUNK
    col_row = col % COMBINE_CHUNK

    def gate_matrix(first):
        rows = jnp.full((1, slab_rows), -1, jnp.int32)
        for q in range(SLAB_CHUNKS):
            ok = jnp.logical_and(col_chunk == q, col_row < clen_ref[first + q])
            rows = jnp.where(ok, cstart_ref[first + q] + col_row, rows)
        g_cols = []
        for c0 in range(0, slab_rows, LANES):
            rows_c = rows[:, c0:c0 + LANES]
            g_c = jnp.zeros((tile, LANES), jnp.float32)
            for j in range(TOP_K):
                g_c = jnp.where(destb_ref[j] == rows_c, gateb_ref[j], g_c)
            g_cols.append(g_c.astype(jnp.bfloat16))
        return jnp.concatenate(g_cols, axis=1)

    def pair(p, carry):
        first = i * MAX_CHUNKS + p * (2 * SLAB_CHUNKS)
        for cp in pair_copies(first):
            cp.wait()
        ys_pair = [jnp.concatenate([zbuf_ref[slot, pl.ds(c, slab_rows, stride=ROW_CHUNKS), :]
                                    for c in range(ROW_CHUNKS)], axis=1).astype(jnp.bfloat16)
                   for slot in range(2)]
        more_here = p + 1 < n_pairs
        nxt = jnp.where(more_here, first + 2 * SLAB_CHUNKS, (i + 1) * MAX_CHUNKS)

        @pl.when(jnp.logical_or(more_here, i + 1 < n_tiles))
        def _():
            for cp in pair_copies(nxt):
                cp.start()

        for slot in range(2):
            acc_ref[...] += jnp.dot(gate_matrix(first + slot * SLAB_CHUNKS), ys_pair[slot],
                                    preferred_element_type=jnp.float32)
        return carry

    lax.fori_loop(0, n_pairs, pair, 0)
    x2 = acc_ref[...]
    o_ref[...] = _rms(x2, gf_ref[...]) if final else x2


def _combine(chunk_start, chunk_len, n_pairs, ys, x1, dest_tok, gates_tok, g_final, *, final):
    t = TOK_TILE
    n = x1.shape[0]
    tok = lambda width: pl.BlockSpec((t, width), lambda i, *_: (i, 0))
    return pl.pallas_call(
        functools.partial(_combine_kernel, tile=t, final=final),
        out_shape=jax.ShapeDtypeStruct((n, D_MODEL), jnp.float32),
        grid_spec=pltpu.PrefetchScalarGridSpec(
            num_scalar_prefetch=3,
            grid=(n // t,),
            in_specs=[pl.BlockSpec(memory_space=pl.ANY),
                      tok(D_MODEL), tok(TOP_K), tok(TOP_K),
                      pl.BlockSpec((1, D_MODEL), lambda i, *_: (0, 0))],
            out_specs=tok(D_MODEL),
            scratch_shapes=[pltpu.VMEM((2, SLAB_CHUNKS * COMBINE_CHUNK * ROW_CHUNKS, LANES), jnp.float32),
                            pltpu.VMEM((t, D_MODEL), jnp.float32),
                            pltpu.VMEM((TOP_K, t, LANES), jnp.int32),
                            pltpu.VMEM((TOP_K, t, LANES), jnp.float32),
                            pltpu.SemaphoreType.DMA((2,))]),
        compiler_params=pltpu.CompilerParams(dimension_semantics=("arbitrary",),
                                             vmem_limit_bytes=VMEM_LIMIT),
        name="moe_combine",
    )(chunk_start, chunk_len, n_pairs, ys, x1, dest_tok, gates_tok, g_final.reshape(1, -1))


def _combine_chunks(base, cnt, seg_start):
    n_run = jnp.concatenate([base[1:], cnt[None]], axis=0) - base
    run_start = seg_start[None, :] + base
    n_chunks = (n_run + COMBINE_CHUNK - 1) // COMBINE_CHUNK
    cum = jnp.cumsum(n_chunks, axis=1)
    total = cum[:, -1]
    slots = jnp.arange(MAX_CHUNKS, dtype=jnp.int32)
    expert = jnp.minimum(jnp.sum(cum[:, None, :] <= slots[None, :, None], axis=2), N_EXPERTS - 1)
    hot = expert[..., None] == jnp.arange(N_EXPERTS, dtype=jnp.int32)
    pick = lambda a: jnp.sum(jnp.where(hot, a[:, None, :], 0), axis=2)
    k = slots[None, :] - (pick(cum) - pick(n_chunks))
    valid = slots[None, :] < total[:, None]
    start = jnp.where(valid, pick(run_start) + COMBINE_CHUNK * k, 0)
    length = jnp.where(valid, jnp.clip(pick(n_run) - COMBINE_CHUNK * k, 0, COMBINE_CHUNK), 0)
    n_pairs = (total + 2 * SLAB_CHUNKS - 1) // (2 * SLAB_CHUNKS)
    return (start.reshape(-1).astype(jnp.int32), length.reshape(-1).astype(jnp.int32),
            n_pairs.astype(jnp.int32))


def kernel(x, rel_table, norm1, w_in, lambda_q1, lambda_k1, lambda_q2, lambda_k2, subln_g, w_pool, pool_scale,
           w_out, norm2, w_router, b_router, w_gate, b_gate, w_up, b_up, w_down, b_down, final_norm):
    batch, seq, d = x.shape
    depth = norm1.shape[0]
    n = batch * seq
    assert d == D_MODEL and seq % ATT_TILE == 0 and seq % TOK_TILE == 0
    assert n % DISPATCH_TILE == 0
    bf = jnp.bfloat16

    w_q = w_in[:, :, :QK_WIDTH] * (QK_DIM ** -0.5 * LOG2E)
    w_k = w_in[:, :, QK_WIDTH:2 * QK_WIDTH]
    w_v = w_in[:, :, 2 * QK_WIDTH:2 * QK_WIDTH + ATT_WIDTH]
    w_p = w_in[:, :, 2 * QK_WIDTH + ATT_WIDTH:]
    w_kp = jnp.concatenate([w_k, w_p], axis=2).astype(bf)
    w_qvt = jnp.swapaxes(jnp.concatenate([w_q, w_v], axis=2), 1, 2).astype(bf)
    w_out_b = w_out.astype(bf)
    w_pool_b = w_pool.astype(bf)
    wr_t = jnp.swapaxes(w_router, 1, 2).astype(bf)
    tri = (lax.broadcasted_iota(jnp.int32, (TOK_TILE, TOK_TILE), 0)
           < lax.broadcasted_iota(jnp.int32, (TOK_TILE, TOK_TILE), 1)).astype(bf)

    bias = _bias_tiles(rel_table, ATT_TILE)
    far = jnp.stack([rel_table[N_BUCKETS // 2 - 1], rel_table[N_BUCKETS - 1]], axis=1).reshape(-1)

    n_assign = n * TOP_K
    n_blocks = -(-(n_assign + N_EXPERTS * (EXPERT_ROWS - 1)) // EXPERT_ROWS) + 1
    n_rows = n_blocks * EXPERT_ROWS
    block_start = jnp.arange(n_blocks, dtype=jnp.int32) * EXPERT_ROWS

    x2d = x.reshape(n, d)
    for l in range(depth):
        lam_init = 0.8 - 0.6 * math.exp(-0.3 * l)
        kp, qvt = _norm_proj(x2d, norm1[l], w_kp[l], w_qvt[l])
        att = _attention(kp, qvt, bias, far, lambda_q1[l], lambda_k1[l], lambda_q2[l], lambda_k2[l], subln_g[l],
                         batch=batch, seq=seq, lam_init=lam_init)
        x1, hflat, topi, rank, gates, base, counts = _mix_route(
            att, kp, x2d, w_pool_b[l], pool_scale[l], w_out_b[l], norm2[l], wr_t[l], b_router[l], tri, seq=seq)

        cnt = counts[:, 0].astype(jnp.int32)
        padded = ((cnt + EXPERT_ROWS - 1) // EXPERT_ROWS) * EXPERT_ROWS
        pad_end = jnp.cumsum(padded)
        experts = jnp.arange(N_EXPERTS, dtype=jnp.int32)
        seg_start = pad_end - padded
        dest_tok = (jnp.sum(jnp.where(topi[..., None] == experts, seg_start, 0), axis=-1) + rank).T
        dest_flat = dest_tok.reshape(-1)
        block_expert = jnp.minimum(jnp.sum(pad_end[None, :] <= block_start[:, None], axis=1),
                                   N_EXPERTS - 1).astype(jnp.int32)
        n_used = (pad_end[-1:] // EXPERT_ROWS).astype(jnp.int32)

        pad_lo = jnp.concatenate([pad_end - padded + cnt, pad_end[-1:]])
        pad_n = jnp.concatenate([padded - cnt, n_rows - pad_end[-1:]])
        xs = _dispatch(pad_lo, pad_n, dest_flat, hflat, n_rows)
        ys = _experts(block_expert, n_used, xs, w_gate, b_gate, w_up, b_up, w_down, b_down, layer=l)
        chunk_start, chunk_len, n_pairs = _combine_chunks(base[:, :, 0].astype(jnp.int32), cnt, seg_start)
        x2d = _combine(chunk_start, chunk_len, n_pairs, ys, x1, dest_tok, gates.T, final_norm,
                       final=(l == depth - 1))
    return x2d.reshape(batch, seq, d)
```

```python
import functools
import math

import jax
import jax.numpy as jnp
from jax import lax
from jax.experimental import pallas as pl
from jax.experimental.pallas import tpu as pltpu

D_MODEL = 1024
N_HEADS = 4
QK_DIM = 64
V_DIM = 128
QK_WIDTH = N_HEADS * 2 * QK_DIM
ATT_WIDTH = N_HEADS * V_DIM
POOL_WINDOWS = (2, 4, 8, 16)
POOL_GROUP = 128
POOL_WIDTH = 512
IN_WIDTH = 2 * QK_WIDTH + ATT_WIDTH + POOL_WIDTH
N_BUCKETS = 32
MAX_DISTANCE = 128
N_EXPERTS = 32
TOP_K = 4
SWIGLU_LIMIT = 7.0
SWIGLU_ALPHA = 1.702
EPS = 1e-5

LANES = 128
SUBLANES = 8
ROW_CHUNKS = D_MODEL // LANES
VMEM_LIMIT = 56 * 1024 * 1024

TOK_TILE = 512
ATT_TILE = 512
ATT_CHUNK = 256
BIAS_TILES = 5
NEAR_TILES = 4
V_EXT = V_DIM + 16
HALO = 16
EXPERT_ROWS = 256
DISPATCH_TILE = 1024
COMBINE_CHUNK = 64
SLAB_CHUNKS = 4
MAX_CHUNKS = TOK_TILE * TOP_K // COMBINE_CHUNK + N_EXPERTS

_NT = (((1,), (1,)), ((), ()))
LOG2E = math.log2(math.e)


def _rms(x, g):
    return x * lax.rsqrt(jnp.mean(x * x, axis=-1, keepdims=True) + EPS) * g


def _bias_kernel(tab_ref, out_ref, *, tile):
    h = pl.program_id(0)
    d = pl.program_id(1)
    key = lax.broadcasted_iota(jnp.int32, (tile, tile), 0)
    qry = lax.broadcasted_iota(jnp.int32, (tile, tile), 1)
    rel = (d - BIAS_TILES // 2) * tile + key - qry
    nb = N_BUCKETS // 2
    max_exact = nb // 2
    ret = jnp.where(rel > 0, nb, 0)
    n = jnp.abs(rel)
    n_f = jnp.maximum(n, 1).astype(jnp.float32)
    large = max_exact + (jnp.log(n_f / max_exact) / math.log(MAX_DISTANCE / max_exact)
                         * (nb - max_exact)).astype(jnp.int32)
    large = jnp.minimum(large, nb - 1)
    bucket = ret + jnp.where(n < max_exact, n, large)
    acc = jnp.zeros((tile, tile), jnp.float32)
    for b in range(N_BUCKETS):
        acc = jnp.where(bucket == b, tab_ref[b * N_HEADS + h], acc)
    out_ref[0, 0] = acc * LOG2E


def _bias_tiles(rel_table, tile):
    return pl.pallas_call(
        functools.partial(_bias_kernel, tile=tile),
        out_shape=jax.ShapeDtypeStruct((N_HEADS, BIAS_TILES, tile, tile), jnp.float32),
        grid=(N_HEADS, BIAS_TILES),
        in_specs=[pl.BlockSpec(memory_space=pltpu.SMEM)],
        out_specs=pl.BlockSpec((1, 1, tile, tile), lambda h, d: (h, d, 0, 0)),
        name="rel_bias_tiles",
    )(rel_table.reshape(-1))


def _norm_proj_kernel(x_ref, g_ref, wkp_ref, wqvt_ref, kp_ref, qvt_ref):
    h = _rms(x_ref[...], g_ref[...]).astype(jnp.bfloat16)
    kp_ref[...] = jnp.dot(h, wkp_ref[...], preferred_element_type=jnp.float32).astype(kp_ref.dtype)
    qvt_ref[...] = lax.dot_general(wqvt_ref[...], h, _NT,
                                   preferred_element_type=jnp.float32).astype(qvt_ref.dtype)


def _norm_proj(x2d, g, w_kp, w_qvt):
    n = x2d.shape[0]
    width = QK_WIDTH + POOL_WIDTH
    width_t = QK_WIDTH + ATT_WIDTH
    return pl.pallas_call(
        _norm_proj_kernel,
        out_shape=(jax.ShapeDtypeStruct((n, width), jnp.bfloat16),
                   jax.ShapeDtypeStruct((width_t, n), jnp.bfloat16)),
        grid=(n // TOK_TILE,),
        in_specs=[pl.BlockSpec((TOK_TILE, D_MODEL), lambda i: (i, 0)),
                  pl.BlockSpec((1, D_MODEL), lambda i: (0, 0)),
                  pl.BlockSpec((D_MODEL, width), lambda i: (0, 0)),
                  pl.BlockSpec((width_t, D_MODEL), lambda i: (0, 0))],
        out_specs=(pl.BlockSpec((TOK_TILE, width), lambda i: (i, 0)),
                   pl.BlockSpec((width_t, TOK_TILE), lambda i: (0, i))),
        compiler_params=pltpu.CompilerParams(dimension_semantics=("parallel",),
                                             vmem_limit_bytes=VMEM_LIMIT),
        name="norm_in_proj",
    )(x2d, g.reshape(1, D_MODEL), w_kp, w_qvt)


def _attn_kernel(far_ref, qt_ref, k_ref, vt_ref, bias_ref, lq1_ref, lk1_ref, lq2_ref, lk2_ref, sg_ref,
                 o_ref, qs_ref, vte_ref, s0_ref, s1_ref, mc0_ref, mc1_ref, m_ref, acc_ref,
                 *, tile, n_kv, lam_init):
    h = pl.program_id(1)
    qi = pl.program_id(2)
    n_far = n_kv - NEAR_TILES

    @pl.when(qi == 0)
    def _():
        vte_ref[:V_DIM, :] = vt_ref[...]
        vte_ref[V_DIM:, :] = jnp.ones((V_EXT - V_DIM, vte_ref.shape[1]), vte_ref.dtype)

    qt = qt_ref[...]
    row = lax.broadcasted_iota(jnp.int32, qt.shape, 0)
    zero = jnp.zeros_like(qt)
    qs_ref[:, :tile] = jnp.where(row < QK_DIM, qt, zero)
    qs_ref[:, tile:] = jnp.where(row >= QK_DIM, qt, zero)

    m_ref[...] = jnp.full(m_ref.shape, -jnp.inf, jnp.float32)
    acc_ref[...] = jnp.zeros(acc_ref.shape, jnp.float32)
    chunks = [slice(c * ATT_CHUNK, (c + 1) * ATT_CHUNK) for c in range(2 * tile // ATT_CHUNK)]
    w0 = jnp.clip(qi - 1, 0, n_kv - NEAR_TILES)
    c_below = far_ref[2 * h] * LOG2E
    c_above = far_ref[2 * h + 1] * LOG2E

    def far_tile(f):
        f = jnp.minimum(f, n_far - 1)
        return jnp.where(f < w0, f, f + NEAR_TILES)

    def far_const(ki):
        return jnp.where(ki < qi, c_below, c_above)

    def logits(ki, cols):
        k = k_ref[pl.ds(pl.multiple_of(ki * tile, tile), tile), :]
        return jnp.dot(k, qs_ref[:, cols], preferred_element_type=jnp.float32)

    def scores_near(ki, cols, s_ref, mc_ref):
        d = jnp.clip(ki - qi, -2, 2) + 2
        q0 = cols.start % tile
        s = logits(ki, cols) + bias_ref[0, d, :, q0:q0 + ATT_CHUNK]
        s_ref[:, cols] = s
        mc_ref[:, cols] = jnp.max(s, axis=0, keepdims=True)

    def scores_far(ki, cols, s_ref, mc_ref):
        s = logits(ki, cols)
        s_ref[:, cols] = s
        mc_ref[:, cols] = jnp.max(s, axis=0, keepdims=True) + far_const(ki)

    def update(ki, cols, s_ref, mc_ref, const):
        vt = vte_ref[:, pl.ds(pl.multiple_of(ki * tile, tile), tile)]
        m_prev = m_ref[:, cols]
        m_new = jnp.maximum(m_prev, mc_ref[:, cols])
        alpha = jnp.exp2(m_prev - m_new)
        p = jnp.exp2(s_ref[:, cols] - (m_new - const))
        acc_ref[:, cols] = alpha * acc_ref[:, cols] + jnp.dot(vt, p.astype(jnp.bfloat16),
                                                              preferred_element_type=jnp.float32)
        m_ref[:, cols] = m_new

    bufs = ((s0_ref, mc0_ref), (s1_ref, mc1_ref))
    for cols in chunks:
        scores_near(w0, cols, *bufs[0])
    for i in range(NEAR_TILES):
        for cols in chunks:
            if i + 1 < NEAR_TILES:
                scores_near(w0 + i + 1, cols, *bufs[(i + 1) % 2])
            elif n_far > 0:
                scores_far(far_tile(0), cols, *bufs[(i + 1) % 2])
            update(w0 + i, cols, *bufs[i % 2], 0.0)

    def pair(j, carry):
        ka, kb, kc = far_tile(2 * j), far_tile(2 * j + 1), far_tile(2 * j + 2)
        for cols in chunks:
            scores_far(kb, cols, *bufs[(NEAR_TILES + 1) % 2])
            update(ka, cols, *bufs[NEAR_TILES % 2], far_const(ka))
        for cols in chunks:
            scores_far(kc, cols, *bufs[NEAR_TILES % 2])
            update(kb, cols, *bufs[(NEAR_TILES + 1) % 2], far_const(kb))
        return carry

    lax.fori_loop(0, n_far // 2, pair, 0)

    acc = acc_ref[...]
    o = acc[:V_DIM] / acc[V_DIM:V_DIM + 1]
    lam = (jnp.exp(jnp.sum(lq1_ref[...] * lk1_ref[...], axis=1, keepdims=True))
           - jnp.exp(jnp.sum(lq2_ref[...] * lk2_ref[...], axis=1, keepdims=True)) + lam_init)
    o = o[:, :tile] - lam * o[:, tile:]
    o = o * lax.rsqrt(jnp.mean(o * o, axis=0, keepdims=True) + EPS) * (sg_ref[...] * (1.0 - lam_init))
    o_ref[...] = o.T.astype(o_ref.dtype)


def _attention(kp, qvt, bias, far, lq1, lk1, lq2, lk2, sub_g, *, batch, seq, lam_init):
    t = ATT_TILE
    nq = seq // t
    assert nq >= NEAR_TILES and (nq - NEAR_TILES) % 2 == 0
    vrow = QK_WIDTH // LANES
    vec = lambda width: pl.BlockSpec((1, width), lambda b, h, qi: (0, 0))
    return pl.pallas_call(
        functools.partial(_attn_kernel, tile=t, n_kv=nq, lam_init=lam_init),
        out_shape=jax.ShapeDtypeStruct((batch * seq, ATT_WIDTH), jnp.bfloat16),
        grid=(batch, N_HEADS, nq),
        in_specs=[pl.BlockSpec(memory_space=pltpu.SMEM),
                  pl.BlockSpec((LANES, t), lambda b, h, qi: (h, b * nq + qi)),
                  pl.BlockSpec((seq, LANES), lambda b, h, qi: (b, h)),
                  pl.BlockSpec((V_DIM, seq), lambda b, h, qi: (vrow + h, b)),
                  pl.BlockSpec((1, BIAS_TILES, t, t), lambda b, h, qi: (h, 0, 0, 0)),
                  vec(QK_DIM), vec(QK_DIM), vec(QK_DIM), vec(QK_DIM),
                  pl.BlockSpec((V_DIM, 1), lambda b, h, qi: (0, 0))],
        out_specs=pl.BlockSpec((t, V_DIM), lambda b, h, qi: (b * nq + qi, h)),
        scratch_shapes=[pltpu.VMEM((LANES, 2 * t), jnp.bfloat16),
                        pltpu.VMEM((V_EXT, seq), jnp.bfloat16),
                        pltpu.VMEM((t, 2 * t), jnp.float32),
                        pltpu.VMEM((t, 2 * t), jnp.float32),
                        pltpu.VMEM((1, 2 * t), jnp.float32),
                        pltpu.VMEM((1, 2 * t), jnp.float32),
                        pltpu.VMEM((1, 2 * t), jnp.float32),
                        pltpu.VMEM((V_EXT, 2 * t), jnp.float32)],
        compiler_params=pltpu.CompilerParams(dimension_semantics=("arbitrary", "arbitrary", "arbitrary"),
                                             vmem_limit_bytes=VMEM_LIMIT),
        name="diff_attention",
    )(far, qvt, kp, qvt, bias, lq1.reshape(1, -1), lk1.reshape(1, -1), lq2.reshape(1, -1),
      lk2.reshape(1, -1), sub_g.reshape(-1, 1))


def _mix_route_kernel(att_ref, p_ref, prev_ref, next_ref, x_ref, wpool_ref, pscale_ref, wout_ref, g2_ref,
                      wr_ref, br_ref, tri_ref,
                      x1_ref, hflat_ref, topi_ref, rank_ref, gate_ref, base_ref, cnt_out_ref, cnt_ref,
                      *, tile, seq):
    i = pl.program_id(0)
    tiles_per_seq = seq // tile
    si = i % tiles_per_seq

    @pl.when(i == 0)
    def _():
        cnt_ref[...] = jnp.zeros(cnt_ref.shape, jnp.float32)

    pc = p_ref[...].astype(jnp.float32)
    pp = jnp.where(si == 0, 0.0, prev_ref[...].astype(jnp.float32))
    pn = jnp.where(si == tiles_per_seq - 1, 0.0, next_ref[...].astype(jnp.float32))
    pe = jnp.concatenate([pp, pc, pn], axis=0)
    n_ext = tile + 2 * HALO
    pos = si * tile + lax.broadcasted_iota(jnp.int32, (tile, 1), 0)
    pooled = []
    for g, win in enumerate(POOL_WINDOWS):
        xg = pe[:, g * POOL_GROUP:(g + 1) * POOL_GROUP]
        w = xg + pltpu.roll(xg, 1, 0)
        half = 1
        while 2 * half < win:
            w = pltpu.roll(w, half, 0) + pltpu.roll(w, n_ext - half, 0)
            half *= 2
        lo = jnp.maximum(pos - win // 2, 0)
        hi = jnp.minimum(pos - win // 2 + win - 1, seq - 1)
        mean = w[HALO:HALO + tile] / (hi - lo + 1).astype(jnp.float32)
        dg = (mean - pc[:, g * POOL_GROUP:(g + 1) * POOL_GROUP]).astype(jnp.bfloat16)
        pooled.append(jnp.dot(dg, wpool_ref[g], preferred_element_type=jnp.float32))
    pool = jnp.concatenate(pooled, axis=1) * pscale_ref[...]

    mix = jnp.concatenate([att_ref[...], pool.astype(jnp.bfloat16)], axis=1)
    x1 = x_ref[...] + jnp.dot(mix, wout_ref[...], preferred_element_type=jnp.float32)
    x1_ref[...] = x1

    h2 = _rms(x1, g2_ref[...])
    for c in range(ROW_CHUNKS):
        hflat_ref[pl.ds(c, tile, stride=ROW_CHUNKS), :] = h2[:, c * LANES:(c + 1) * LANES]

    logits = lax.dot_general(wr_ref[...], h2.astype(jnp.bfloat16), _NT,
                             preferred_element_type=jnp.float32) + br_ref[...]
    eio = lax.broadcasted_iota(jnp.int32, logits.shape, 0)
    work = logits
    sel = jnp.zeros(logits.shape, jnp.float32)
    top_v, top_i, hot = [], [], []
    for _ in range(TOP_K):
        mx = jnp.max(work, axis=0, keepdims=True)
        idx = jnp.min(jnp.where(work == mx, eio, N_EXPERTS), axis=0, keepdims=True)
        oh = eio == idx
        top_v.append(mx)
        top_i.append(idx)
        hot.append(oh)
        work = jnp.where(oh, -jnp.inf, work)
        sel = sel + oh.astype(jnp.float32)
    ex = [jnp.exp(v - top_v[0]) for v in top_v]
    den = ex[0] + ex[1] + ex[2] + ex[3]
    ahead = jnp.dot(sel.astype(jnp.bfloat16), tri_ref[...], preferred_element_type=jnp.float32) + cnt_ref[...]
    for j in range(TOP_K):
        topi_ref[j:j + 1, :] = top_i[j]
        gate_ref[j:j + 1, :] = ex[j] / den
        rank_ref[j:j + 1, :] = jnp.sum(jnp.where(hot[j], ahead, 0.0), axis=0, keepdims=True).astype(jnp.int32)
    base_ref[0] = jnp.broadcast_to(cnt_ref[...], base_ref.shape[1:])
    cnt_ref[...] = cnt_ref[...] + jnp.sum(sel, axis=1, keepdims=True)
    cnt_out_ref[...] = jnp.broadcast_to(cnt_ref[...], cnt_out_ref.shape)


def _mix_route(att, proj, x2d, wpool, pscale, wout, g2, wr_t, br, tri, *, seq):
    n = x2d.shape[0]
    t = TOK_TILE
    nt = n // t
    pcol = QK_WIDTH // POOL_WIDTH
    hb = t // HALO
    last_halo = n // HALO - 1
    const = lambda shape: pl.BlockSpec(shape, lambda i: (0,) * len(shape))
    out_shape = (jax.ShapeDtypeStruct((n, D_MODEL), jnp.float32),
                 jax.ShapeDtypeStruct((n * ROW_CHUNKS, LANES), jnp.float32),
                 jax.ShapeDtypeStruct((TOP_K, n), jnp.int32),
                 jax.ShapeDtypeStruct((TOP_K, n), jnp.int32),
                 jax.ShapeDtypeStruct((TOP_K, n), jnp.float32),
                 jax.ShapeDtypeStruct((nt, N_EXPERTS, LANES), jnp.float32),
                 jax.ShapeDtypeStruct((N_EXPERTS, LANES), jnp.float32))
    return pl.pallas_call(
        functools.partial(_mix_route_kernel, tile=t, seq=seq),
        out_shape=out_shape,
        grid=(nt,),
        in_specs=[pl.BlockSpec((t, ATT_WIDTH), lambda i: (i, 0)),
                  pl.BlockSpec((t, POOL_WIDTH), lambda i: (i, pcol)),
                  pl.BlockSpec((HALO, POOL_WIDTH), lambda i: (jnp.maximum(i * hb - 1, 0), pcol)),
                  pl.BlockSpec((HALO, POOL_WIDTH), lambda i: (jnp.minimum((i + 1) * hb, last_halo), pcol)),
                  pl.BlockSpec((t, D_MODEL), lambda i: (i, 0)),
                  const((len(POOL_WINDOWS), POOL_GROUP, POOL_GROUP)),
                  const((1, POOL_WIDTH)),
                  const((D_MODEL, D_MODEL)),
                  const((1, D_MODEL)),
                  const((N_EXPERTS, D_MODEL)),
                  const((N_EXPERTS, 1)),
                  const((t, t))],
        out_specs=(pl.BlockSpec((t, D_MODEL), lambda i: (i, 0)),
                   pl.BlockSpec((t * ROW_CHUNKS, LANES), lambda i: (i, 0)),
                   pl.BlockSpec((TOP_K, t), lambda i: (0, i)),
                   pl.BlockSpec((TOP_K, t), lambda i: (0, i)),
                   pl.BlockSpec((TOP_K, t), lambda i: (0, i)),
                   pl.BlockSpec((1, N_EXPERTS, LANES), lambda i: (i, 0, 0)),
                   pl.BlockSpec((N_EXPERTS, LANES), lambda i: (0, 0))),
        scratch_shapes=[pltpu.VMEM((N_EXPERTS, 1), jnp.float32)],
        compiler_params=pltpu.CompilerParams(dimension_semantics=("arbitrary",),
                                             vmem_limit_bytes=VMEM_LIMIT),
        name="mix_out_route",
    )(att, proj, proj, proj, x2d, wpool, pscale.reshape(1, -1), wout, g2.reshape(1, -1), wr_t,
      br.reshape(-1, 1), tri)


def _row(ref, r):
    return ref.at[pl.ds(pl.multiple_of(r * ROW_CHUNKS, ROW_CHUNKS), ROW_CHUNKS), :]


def _dispatch_kernel(pad_lo_ref, pad_n_ref, dest_hbm, h_ref, xs_out, idx_ref, zero_ref, idx_sem, row_sem, pad_sem,
                     *, tile):
    i = pl.program_id(0)
    per = tile * TOP_K

    @pl.when(i == 0)
    def _():
        zero_ref[...] = jnp.zeros(zero_ref.shape, zero_ref.dtype)

        def each_pad_row(fn):
            def per_range(e, c):
                lax.fori_loop(0, pad_n_ref[e], lambda r, c2: (fn(pad_lo_ref[e] + r), c2)[1], 0)
                return c
            lax.fori_loop(0, N_EXPERTS + 1, per_range, 0)

        each_pad_row(lambda r: pltpu.make_async_copy(zero_ref, _row(xs_out, r), pad_sem).start())
        each_pad_row(lambda r: pltpu.make_async_copy(zero_ref, _row(xs_out, r), pad_sem).wait())

    idx_copy = pltpu.make_async_copy(dest_hbm.at[pl.ds(pl.multiple_of(i * per, per), per)], idx_ref, idx_sem)
    idx_copy.start()
    idx_copy.wait()

    def body(t, c):
        src = _row(h_ref, t)
        for j in range(TOP_K):
            pltpu.make_async_copy(src, _row(xs_out, idx_ref[t * TOP_K + j]), row_sem).start(priority=j % 2)
        return c

    lax.fori_loop(0, tile, body, 0)
    for _ in range(TOP_K):
        pltpu.make_async_copy(h_ref, xs_out.at[pl.ds(0, tile * ROW_CHUNKS), :], row_sem).wait()


def _dispatch(pad_lo, pad_n, dest_flat, hflat, n_rows):
    t = DISPATCH_TILE
    per = t * TOP_K
    nt = dest_flat.shape[0] // per
    return pl.pallas_call(
        functools.partial(_dispatch_kernel, tile=t),
        out_shape=jax.ShapeDtypeStruct((n_rows * ROW_CHUNKS, LANES), jnp.float32),
        grid_spec=pltpu.PrefetchScalarGridSpec(
            num_scalar_prefetch=2,
            grid=(nt,),
            in_specs=[pl.BlockSpec(memory_space=pl.ANY),
                      pl.BlockSpec((t * ROW_CHUNKS, LANES), lambda i, lo, cnt: (i, 0))],
            out_specs=pl.BlockSpec(memory_space=pl.ANY),
            scratch_shapes=[pltpu.SMEM((per,), jnp.int32),
                            pltpu.VMEM((ROW_CHUNKS, LANES), jnp.float32),
                            pltpu.SemaphoreType.DMA,
                            pltpu.SemaphoreType.DMA,
                            pltpu.SemaphoreType.DMA]),
        compiler_params=pltpu.CompilerParams(dimension_semantics=("arbitrary",),
                                             vmem_limit_bytes=VMEM_LIMIT),
        name="moe_dispatch",
    )(pad_lo, pad_n, dest_flat, hflat)


def _expert_kernel(be_ref, used_ref, x_ref, wg_ref, bg_ref, wu_ref, bu_ref, wd_ref, bd_ref, y_ref,
                   wg_bf, wu_bf, wd_bf, *, rows):
    b = pl.program_id(0)
    live = b < used_ref[0]

    @pl.when(jnp.logical_not(live))
    def _():
        y_ref[...] = jnp.zeros(y_ref.shape, y_ref.dtype)

    @pl.when(jnp.logical_and(live, jnp.logical_or(b == 0, be_ref[b] != be_ref[jnp.maximum(b - 1, 0)])))
    def _():
        wg_bf[...] = wg_ref[0, 0].astype(jnp.bfloat16)
        wu_bf[...] = wu_ref[0, 0].astype(jnp.bfloat16)
        wd_bf[...] = wd_ref[0, 0].astype(jnp.bfloat16)

    @pl.when(live)
    def _():
        x = jnp.concatenate([x_ref[pl.ds(c, rows, stride=ROW_CHUNKS), :] for c in range(ROW_CHUNKS)],
                            axis=1).astype(jnp.bfloat16)
        g = jnp.dot(x, wg_bf[...], preferred_element_type=jnp.float32) + bg_ref[0, 0]
        u = jnp.dot(x, wu_bf[...], preferred_element_type=jnp.float32) + bu_ref[0, 0]
        g = jnp.minimum(g, SWIGLU_LIMIT)
        u = jnp.clip(u, -SWIGLU_LIMIT, SWIGLU_LIMIT)
        act = (u + 1.0) * (g * (1.0 / (1.0 + jnp.exp(-SWIGLU_ALPHA * g))))
        y = jnp.dot(act.astype(jnp.bfloat16), wd_bf[...], preferred_element_type=jnp.float32) + bd_ref[0, 0]
        for c in range(ROW_CHUNKS):
            y_ref[pl.ds(c, rows, stride=ROW_CHUNKS), :] = y[:, c * LANES:(c + 1) * LANES]


def _experts(block_expert, n_used, xs, wg, bg, wu, bu, wd, bd, *, layer):
    nb = block_expert.shape[0]
    rows = EXPERT_ROWS
    depth = wg.shape[0]
    blk = lambda b, be, used: (jnp.minimum(b, used[0] - 1), 0)
    wsel = lambda b, be, used: (layer, be[b], 0, 0)
    wspec = pl.BlockSpec((1, 1, D_MODEL, D_MODEL), wsel)
    bspec = pl.BlockSpec((1, 1, 1, D_MODEL), wsel)
    bias4 = lambda a: a.reshape(depth, N_EXPERTS, 1, D_MODEL)
    return pl.pallas_call(
        functools.partial(_expert_kernel, rows=rows),
        out_shape=jax.ShapeDtypeStruct(xs.shape, jnp.float32),
        grid_spec=pltpu.PrefetchScalarGridSpec(
            num_scalar_prefetch=2,
            grid=(nb,),
            in_specs=[pl.BlockSpec((rows * ROW_CHUNKS, LANES), blk),
                      wspec, bspec, wspec, bspec, wspec, bspec],
            out_specs=pl.BlockSpec((rows * ROW_CHUNKS, LANES), lambda b, be, used: (b, 0)),
            scratch_shapes=[pltpu.VMEM((D_MODEL, D_MODEL), jnp.bfloat16)] * 3),
        compiler_params=pltpu.CompilerParams(dimension_semantics=("arbitrary",),
                                             vmem_limit_bytes=VMEM_LIMIT),
        name="moe_experts",
    )(block_expert, n_used, xs, wg, bias4(bg), wu, bias4(bu), wd, bias4(bd))


def _combine_kernel(cstart_ref, clen_ref, pchunk_ref, gbound_ref, ys_hbm, x_ref, dest_ref, gate_ref, gf_ref,
                    o_ref, zbuf_ref, acc_ref, destb_ref, gateb_ref, sem, *, tile, final):
    i = pl.program_id(0)
    n_tiles = pl.num_programs(0)
    slab_rows = SLAB_CHUNKS * COMBINE_CHUNK
    chunk_flat = COMBINE_CHUNK * ROW_CHUNKS

    def pair_copies(first_chunk, ring):
        return [pltpu.make_async_copy(
            ys_hbm.at[pl.ds(pl.multiple_of(cstart_ref[first_chunk + q] * ROW_CHUNKS, ROW_CHUNKS), chunk_flat), :],
            zbuf_ref.at[ring, q // SLAB_CHUNKS, pl.ds((q % SLAB_CHUNKS) * chunk_flat, chunk_flat), :],
            sem.at[ring]) for q in range(2 * SLAB_CHUNKS)]

    g0 = gbound_ref[i]
    n_pairs = gbound_ref[i + 1] - g0
    g_total = gbound_ref[n_tiles]

    @pl.when(i == 0)
    def _():
        for cp in pair_copies(pchunk_ref[0], 0):
            cp.start()

        @pl.when(g_total > 1)
        def _():
            for cp in pair_copies(pchunk_ref[1], 1):
                cp.start()

    acc_ref[...] = x_ref[...]
    for j in range(TOP_K):
        destb_ref[j] = jnp.broadcast_to(dest_ref[:, j:j + 1], (tile, LANES))
        gateb_ref[j] = jnp.broadcast_to(gate_ref[:, j:j + 1], (tile, LANES))
    col = lax.broadcasted_iota(jnp.int32, (1, slab_rows), 1)
    col_chunk = col // COMBINE_CHUNK
    col_row = col % COMBINE_CHUNK

    def gate_matrix(first):
        rows = jnp.full((1, slab_rows), -1, jnp.int32)
        for q in range(SLAB_CHUNKS):
            ok = jnp.logical_and(col_chunk == q, col_row < clen_ref[first + q])
            rows = jnp.where(ok, cstart_ref[first + q] + col_row, rows)
        g_cols = []
        for c0 in range(0, slab_rows, LANES):
            rows_c = rows[:, c0:c0 + LANES]
            g_c = jnp.zeros((tile, LANES), jnp.float32)
            for j in range(TOP_K):
                g_c = jnp.where(destb_ref[j] == rows_c, gateb_ref[j], g_c)
            g_cols.append(g_c.astype(jnp.bfloat16))
        return jnp.concatenate(g_cols, axis=1)

    def pair(p, carry):
        g = g0 + p
        ring = g % 2
        first = pchunk_ref[g]
        for cp in pair_copies(first, ring):
            cp.wait()
        ys_pair = [jnp.concatenate([zbuf_ref[ring, slot, pl.ds(c, slab_rows, stride=ROW_CHUNKS), :]
                                    for c in range(ROW_CHUNKS)], axis=1).astype(jnp.bfloat16)
                   for slot in range(2)]

        @pl.when(g + 2 < g_total)
        def _():
            for cp in pair_copies(pchunk_ref[g + 2], ring):
                cp.start()

        for slot in range(2):
            acc_ref[...] += jnp.dot(gate_matrix(first + slot * SLAB_CHUNKS), ys_pair[slot],
                                    preferred_element_type=jnp.float32)
        return carry

    lax.fori_loop(0, n_pairs, pair, 0)
    x2 = acc_ref[...]
    o_ref[...] = _rms(x2, gf_ref[...]) if final else x2


def _combine(chunk_start, chunk_len, pair_chunk, pair_bound, ys, x1, dest_tok, gates_tok, g_final, *, final):
    t = TOK_TILE
    n = x1.shape[0]
    tok = lambda width: pl.BlockSpec((t, width), lambda i, *_: (i, 0))
    return pl.pallas_call(
        functools.partial(_combine_kernel, tile=t, final=final),
        out_shape=jax.ShapeDtypeStruct((n, D_MODEL), jnp.float32),
        grid_spec=pltpu.PrefetchScalarGridSpec(
            num_scalar_prefetch=4,
            grid=(n // t,),
            in_specs=[pl.BlockSpec(memory_space=pl.ANY),
                      tok(D_MODEL), tok(TOP_K), tok(TOP_K),
                      pl.BlockSpec((1, D_MODEL), lambda i, *_: (0, 0))],
            out_specs=tok(D_MODEL),
            scratch_shapes=[pltpu.VMEM((2, 2, SLAB_CHUNKS * COMBINE_CHUNK * ROW_CHUNKS, LANES), jnp.float32),
                            pltpu.VMEM((t, D_MODEL), jnp.float32),
                            pltpu.VMEM((TOP_K, t, LANES), jnp.int32),
                            pltpu.VMEM((TOP_K, t, LANES), jnp.float32),
                            pltpu.SemaphoreType.DMA((2,))]),
        compiler_params=pltpu.CompilerParams(dimension_semantics=("arbitrary",),
                                             vmem_limit_bytes=VMEM_LIMIT),
        name="moe_combine",
    )(chunk_start, chunk_len, pair_chunk, pair_bound, ys, x1, dest_tok, gates_tok, g_final.reshape(1, -1))


def _combine_chunks(base, cnt, seg_start):
    n_run = jnp.concatenate([base[1:], cnt[None]], axis=0) - base
    run_start = seg_start[None, :] + base
    n_chunks = (n_run + COMBINE_CHUNK - 1) // COMBINE_CHUNK
    cum = jnp.cumsum(n_chunks, axis=1)
    total = cum[:, -1]
    slots = jnp.arange(MAX_CHUNKS, dtype=jnp.int32)
    expert = jnp.minimum(jnp.sum(cum[:, None, :] <= slots[None, :, None], axis=2), N_EXPERTS - 1)
    hot = expert[..., None] == jnp.arange(N_EXPERTS, dtype=jnp.int32)
    pick = lambda a: jnp.sum(jnp.where(hot, a[:, None, :], 0), axis=2)
    k = slots[None, :] - (pick(cum) - pick(n_chunks))
    valid = slots[None, :] < total[:, None]
    start = jnp.where(valid, pick(run_start) + COMBINE_CHUNK * k, 0)
    length = jnp.where(valid, jnp.clip(pick(n_run) - COMBINE_CHUNK * k, 0, COMBINE_CHUNK), 0)
    pair_size = 2 * SLAB_CHUNKS
    n_tiles = base.shape[0]
    n_pairs = (total + pair_size - 1) // pair_size
    pair_end = jnp.cumsum(n_pairs)
    pair_bound = jnp.concatenate([jnp.zeros((1,), pair_end.dtype), pair_end])
    g = jnp.arange(n_tiles * (MAX_CHUNKS // pair_size), dtype=jnp.int32)
    tile_of = jnp.minimum(jnp.sum(pair_end[None, :] <= g[:, None], axis=1), n_tiles - 1)
    tile_first = jnp.sum(jnp.where(tile_of[:, None] == jnp.arange(n_tiles), (pair_end - n_pairs)[None, :], 0), axis=1)
    pair_chunk = tile_of * MAX_CHUNKS + pair_size * (g - tile_first)
    i32 = lambda a: a.reshape(-1).astype(jnp.int32)
    return i32(start), i32(length), i32(pair_chunk), i32(pair_bound)


def kernel(x, rel_table, norm1, w_in, lambda_q1, lambda_k1, lambda_q2, lambda_k2, subln_g, w_pool, pool_scale,
           w_out, norm2, w_router, b_router, w_gate, b_gate, w_up, b_up, w_down, b_down, final_norm):
    batch, seq, d = x.shape
    depth = norm1.shape[0]
    n = batch * seq
    assert d == D_MODEL and seq % ATT_TILE == 0 and seq % TOK_TILE == 0
    assert n % DISPATCH_TILE == 0
    bf = jnp.bfloat16

    w_q = w_in[:, :, :QK_WIDTH] * (QK_DIM ** -0.5 * LOG2E)
    w_k = w_in[:, :, QK_WIDTH:2 * QK_WIDTH]
    w_v = w_in[:, :, 2 * QK_WIDTH:2 * QK_WIDTH + ATT_WIDTH]
    w_p = w_in[:, :, 2 * QK_WIDTH + ATT_WIDTH:]
    w_kp = jnp.concatenate([w_k, w_p], axis=2).astype(bf)
    w_qvt = jnp.swapaxes(jnp.concatenate([w_q, w_v], axis=2), 1, 2).astype(bf)
    w_out_b = w_out.astype(bf)
    w_pool_b = w_pool.astype(bf)
    wr_t = jnp.swapaxes(w_router, 1, 2).astype(bf)
    tri = (lax.broadcasted_iota(jnp.int32, (TOK_TILE, TOK_TILE), 0)
           < lax.broadcasted_iota(jnp.int32, (TOK_TILE, TOK_TILE), 1)).astype(bf)

    bias = _bias_tiles(rel_table, ATT_TILE)
    far = jnp.stack([rel_table[N_BUCKETS // 2 - 1], rel_table[N_BUCKETS - 1]], axis=1).reshape(-1)

    n_assign = n * TOP_K
    n_blocks = -(-(n_assign + N_EXPERTS * (EXPERT_ROWS - 1)) // EXPERT_ROWS) + 1
    n_rows = n_blocks * EXPERT_ROWS
    block_start = jnp.arange(n_blocks, dtype=jnp.int32) * EXPERT_ROWS

    x2d = x.reshape(n, d)
    for l in range(depth):
        lam_init = 0.8 - 0.6 * math.exp(-0.3 * l)
        kp, qvt = _norm_proj(x2d, norm1[l], w_kp[l], w_qvt[l])
        att = _attention(kp, qvt, bias, far, lambda_q1[l], lambda_k1[l], lambda_q2[l], lambda_k2[l], subln_g[l],
                         batch=batch, seq=seq, lam_init=lam_init)
        x1, hflat, topi, rank, gates, base, counts = _mix_route(
            att, kp, x2d, w_pool_b[l], pool_scale[l], w_out_b[l], norm2[l], wr_t[l], b_router[l], tri, seq=seq)

        cnt = counts[:, 0].astype(jnp.int32)
        padded = ((cnt + EXPERT_ROWS - 1) // EXPERT_ROWS) * EXPERT_ROWS
        pad_end = jnp.cumsum(padded)
        experts = jnp.arange(N_EXPERTS, dtype=jnp.int32)
        seg_start = pad_end - padded
        dest_tok = (jnp.sum(jnp.where(topi[..., None] == experts, seg_start, 0), axis=-1) + rank).T
        dest_flat = dest_tok.reshape(-1)
        block_expert = jnp.minimum(jnp.sum(pad_end[None, :] <= block_start[:, None], axis=1),
                                   N_EXPERTS - 1).astype(jnp.int32)
        n_used = (pad_end[-1:] // EXPERT_ROWS).astype(jnp.int32)

        pad_lo = jnp.concatenate([pad_end - padded + cnt, pad_end[-1:]])
        pad_n = jnp.concatenate([padded - cnt, n_rows - pad_end[-1:]])
        xs = _dispatch(pad_lo, pad_n, dest_flat, hflat, n_rows)
        ys = _experts(block_expert, n_used, xs, w_gate, b_gate, w_up, b_up, w_down, b_down, layer=l)
        chunk_lists = _combine_chunks(base[:, :, 0].astype(jnp.int32), cnt, seg_start)
        x2d = _combine(*chunk_lists, ys, x1, dest_tok, gates.T, final_norm,
                       final=(l == depth - 1))
    return x2d.reshape(batch, seq, d)
```

```python
import functools
import math

import jax
import jax.numpy as jnp
from jax import lax
from jax.experimental import pallas as pl
from jax.experimental.pallas import tpu as pltpu

D_MODEL = 1024
N_HEADS = 4
QK_DIM = 64
V_DIM = 128
QK_WIDTH = N_HEADS * 2 * QK_DIM
ATT_WIDTH = N_HEADS * V_DIM
POOL_WINDOWS = (2, 4, 8, 16)
POOL_GROUP = 128
POOL_WIDTH = 512
IN_WIDTH = 2 * QK_WIDTH + ATT_WIDTH + POOL_WIDTH
N_BUCKETS = 32
MAX_DISTANCE = 128
N_EXPERTS = 32
TOP_K = 4
SWIGLU_LIMIT = 7.0
SWIGLU_ALPHA = 1.702
EPS = 1e-5

LANES = 128
SUBLANES = 8
ROW_CHUNKS = D_MODEL // LANES
VMEM_LIMIT = 56 * 1024 * 1024

TOK_TILE = 512
ATT_TILE = 512
ATT_CHUNK = 256
BIAS_TILES = 5
NEAR_TILES = 4
V_EXT = V_DIM + 16
HALO = 16
EXPERT_ROWS = 512
DISPATCH_TILE = 1024
COMBINE_CHUNK = 64
SLAB_CHUNKS = 4
MAX_CHUNKS = TOK_TILE * TOP_K // COMBINE_CHUNK + N_EXPERTS

_NT = (((1,), (1,)), ((), ()))
LOG2E = math.log2(math.e)


def _rms(x, g):
    return x * lax.rsqrt(jnp.mean(x * x, axis=-1, keepdims=True) + EPS) * g


def _bias_kernel(tab_ref, out_ref, *, tile):
    h = pl.program_id(0)
    d = pl.program_id(1)
    key = lax.broadcasted_iota(jnp.int32, (tile, tile), 0)
    qry = lax.broadcasted_iota(jnp.int32, (tile, tile), 1)
    rel = (d - BIAS_TILES // 2) * tile + key - qry
    nb = N_BUCKETS // 2
    max_exact = nb // 2
    ret = jnp.where(rel > 0, nb, 0)
    n = jnp.abs(rel)
    n_f = jnp.maximum(n, 1).astype(jnp.float32)
    large = max_exact + (jnp.log(n_f / max_exact) / math.log(MAX_DISTANCE / max_exact)
                         * (nb - max_exact)).astype(jnp.int32)
    large = jnp.minimum(large, nb - 1)
    bucket = ret + jnp.where(n < max_exact, n, large)
    acc = jnp.zeros((tile, tile), jnp.float32)
    for b in range(N_BUCKETS):
        acc = jnp.where(bucket == b, tab_ref[b * N_HEADS + h], acc)
    out_ref[0, 0] = acc * LOG2E


def _bias_tiles(rel_table, tile):
    return pl.pallas_call(
        functools.partial(_bias_kernel, tile=tile),
        out_shape=jax.ShapeDtypeStruct((N_HEADS, BIAS_TILES, tile, tile), jnp.float32),
        grid=(N_HEADS, BIAS_TILES),
        in_specs=[pl.BlockSpec(memory_space=pltpu.SMEM)],
        out_specs=pl.BlockSpec((1, 1, tile, tile), lambda h, d: (h, d, 0, 0)),
        name="rel_bias_tiles",
    )(rel_table.reshape(-1))


def _norm_proj_kernel(x_ref, g_ref, wkp_ref, wqvt_ref, kp_ref, qvt_ref):
    h = _rms(x_ref[...], g_ref[...]).astype(jnp.bfloat16)
    kp_ref[...] = jnp.dot(h, wkp_ref[...], preferred_element_type=jnp.float32).astype(kp_ref.dtype)
    qvt_ref[...] = lax.dot_general(wqvt_ref[...], h, _NT,
                                   preferred_element_type=jnp.float32).astype(qvt_ref.dtype)


def _norm_proj(x2d, g, w_kp, w_qvt):
    n = x2d.shape[0]
    width = QK_WIDTH + POOL_WIDTH
    width_t = QK_WIDTH + ATT_WIDTH
    return pl.pallas_call(
        _norm_proj_kernel,
        out_shape=(jax.ShapeDtypeStruct((n, width), jnp.bfloat16),
                   jax.ShapeDtypeStruct((width_t, n), jnp.bfloat16)),
        grid=(n // TOK_TILE,),
        in_specs=[pl.BlockSpec((TOK_TILE, D_MODEL), lambda i: (i, 0)),
                  pl.BlockSpec((1, D_MODEL), lambda i: (0, 0)),
                  pl.BlockSpec((D_MODEL, width), lambda i: (0, 0)),
                  pl.BlockSpec((width_t, D_MODEL), lambda i: (0, 0))],
        out_specs=(pl.BlockSpec((TOK_TILE, width), lambda i: (i, 0)),
                   pl.BlockSpec((width_t, TOK_TILE), lambda i: (0, i))),
        compiler_params=pltpu.CompilerParams(dimension_semantics=("parallel",),
                                             vmem_limit_bytes=VMEM_LIMIT),
        name="norm_in_proj",
    )(x2d, g.reshape(1, D_MODEL), w_kp, w_qvt)


def _attn_kernel(far_ref, qt_ref, k_ref, vt_ref, bias_ref, lq1_ref, lk1_ref, lq2_ref, lk2_ref, sg_ref,
                 o_ref, qs_ref, vte_ref, s0_ref, s1_ref, mc0_ref, mc1_ref, m_ref, acc_ref,
                 *, tile, n_kv, lam_init):
    h = pl.program_id(1)
    qi = pl.program_id(2)
    n_far = n_kv - NEAR_TILES

    @pl.when(qi == 0)
    def _():
        vte_ref[:V_DIM, :] = vt_ref[...]
        vte_ref[V_DIM:, :] = jnp.ones((V_EXT - V_DIM, vte_ref.shape[1]), vte_ref.dtype)

    qt = qt_ref[...]
    row = lax.broadcasted_iota(jnp.int32, qt.shape, 0)
    zero = jnp.zeros_like(qt)
    qs_ref[:, :tile] = jnp.where(row < QK_DIM, qt, zero)
    qs_ref[:, tile:] = jnp.where(row >= QK_DIM, qt, zero)

    m_ref[...] = jnp.full(m_ref.shape, -jnp.inf, jnp.float32)
    acc_ref[...] = jnp.zeros(acc_ref.shape, jnp.float32)
    chunks = [slice(c * ATT_CHUNK, (c + 1) * ATT_CHUNK) for c in range(2 * tile // ATT_CHUNK)]
    w0 = jnp.clip(qi - 1, 0, n_kv - NEAR_TILES)
    c_below = far_ref[2 * h] * LOG2E
    c_above = far_ref[2 * h + 1] * LOG2E

    def far_tile(f):
        f = jnp.minimum(f, n_far - 1)
        return jnp.where(f < w0, f, f + NEAR_TILES)

    def far_const(ki):
        return jnp.where(ki < qi, c_below, c_above)

    def logits(ki, cols):
        k = k_ref[pl.ds(pl.multiple_of(ki * tile, tile), tile), :]
        return jnp.dot(k, qs_ref[:, cols], preferred_element_type=jnp.float32)

    def scores_near(ki, cols, s_ref, mc_ref):
        d = jnp.clip(ki - qi, -2, 2) + 2
        q0 = cols.start % tile
        s = logits(ki, cols) + bias_ref[0, d, :, q0:q0 + ATT_CHUNK]
        s_ref[:, cols] = s
        mc_ref[:, cols] = jnp.max(s, axis=0, keepdims=True)

    def scores_far(ki, cols, s_ref, mc_ref):
        s = logits(ki, cols)
        s_ref[:, cols] = s
        mc_ref[:, cols] = jnp.max(s, axis=0, keepdims=True) + far_const(ki)

    def update(ki, cols, s_ref, mc_ref, const):
        vt = vte_ref[:, pl.ds(pl.multiple_of(ki * tile, tile), tile)]
        m_prev = m_ref[:, cols]
        m_new = jnp.maximum(m_prev, mc_ref[:, cols])
        alpha = jnp.exp2(m_prev - m_new)
        p = jnp.exp2(s_ref[:, cols] - (m_new - const))
        acc_ref[:, cols] = alpha * acc_ref[:, cols] + jnp.dot(vt, p.astype(jnp.bfloat16),
                                                              preferred_element_type=jnp.float32)
        m_ref[:, cols] = m_new

    bufs = ((s0_ref, mc0_ref), (s1_ref, mc1_ref))
    for cols in chunks:
        scores_near(w0, cols, *bufs[0])
    for i in range(NEAR_TILES):
        for cols in chunks:
            if i + 1 < NEAR_TILES:
                scores_near(w0 + i + 1, cols, *bufs[(i + 1) % 2])
            elif n_far > 0:
                scores_far(far_tile(0), cols, *bufs[(i + 1) % 2])
            update(w0 + i, cols, *bufs[i % 2], 0.0)

    def pair(j, carry):
        ka, kb, kc = far_tile(2 * j), far_tile(2 * j + 1), far_tile(2 * j + 2)
        for cols in chunks:
            scores_far(kb, cols, *bufs[(NEAR_TILES + 1) % 2])
            update(ka, cols, *bufs[NEAR_TILES % 2], far_const(ka))
        for cols in chunks:
            scores_far(kc, cols, *bufs[NEAR_TILES % 2])
            update(kb, cols, *bufs[(NEAR_TILES + 1) % 2], far_const(kb))
        return carry

    lax.fori_loop(0, n_far // 2, pair, 0)

    acc = acc_ref[...]
    o = acc[:V_DIM] / acc[V_DIM:V_DIM + 1]
    lam = (jnp.exp(jnp.sum(lq1_ref[...] * lk1_ref[...], axis=1, keepdims=True))
           - jnp.exp(jnp.sum(lq2_ref[...] * lk2_ref[...], axis=1, keepdims=True)) + lam_init)
    o = o[:, :tile] - lam * o[:, tile:]
    o = o * lax.rsqrt(jnp.mean(o * o, axis=0, keepdims=True) + EPS) * (sg_ref[...] * (1.0 - lam_init))
    o_ref[...] = o.T.astype(o_ref.dtype)


def _attention(kp, qvt, bias, far, lq1, lk1, lq2, lk2, sub_g, *, batch, seq, lam_init):
    t = ATT_TILE
    nq = seq // t
    assert nq >= NEAR_TILES and (nq - NEAR_TILES) % 2 == 0
    vrow = QK_WIDTH // LANES
    vec = lambda width: pl.BlockSpec((1, width), lambda b, h, qi: (0, 0))
    return pl.pallas_call(
        functools.partial(_attn_kernel, tile=t, n_kv=nq, lam_init=lam_init),
        out_shape=jax.ShapeDtypeStruct((batch * seq, ATT_WIDTH), jnp.bfloat16),
        grid=(batch, N_HEADS, nq),
        in_specs=[pl.BlockSpec(memory_space=pltpu.SMEM),
                  pl.BlockSpec((LANES, t), lambda b, h, qi: (h, b * nq + qi)),
                  pl.BlockSpec((seq, LANES), lambda b, h, qi: (b, h)),
                  pl.BlockSpec((V_DIM, seq), lambda b, h, qi: (vrow + h, b)),
                  pl.BlockSpec((1, BIAS_TILES, t, t), lambda b, h, qi: (h, 0, 0, 0)),
                  vec(QK_DIM), vec(QK_DIM), vec(QK_DIM), vec(QK_DIM),
                  pl.BlockSpec((V_DIM, 1), lambda b, h, qi: (0, 0))],
        out_specs=pl.BlockSpec((t, V_DIM), lambda b, h, qi: (b * nq + qi, h)),
        scratch_shapes=[pltpu.VMEM((LANES, 2 * t), jnp.bfloat16),
                        pltpu.VMEM((V_EXT, seq), jnp.bfloat16),
                        pltpu.VMEM((t, 2 * t), jnp.float32),
                        pltpu.VMEM((t, 2 * t), jnp.float32),
                        pltpu.VMEM((1, 2 * t), jnp.float32),
                        pltpu.VMEM((1, 2 * t), jnp.float32),
                        pltpu.VMEM((1, 2 * t), jnp.float32),
                        pltpu.VMEM((V_EXT, 2 * t), jnp.float32)],
        compiler_params=pltpu.CompilerParams(dimension_semantics=("arbitrary", "arbitrary", "arbitrary"),
                                             vmem_limit_bytes=VMEM_LIMIT),
        name="diff_attention",
    )(far, qvt, kp, qvt, bias, lq1.reshape(1, -1), lk1.reshape(1, -1), lq2.reshape(1, -1),
      lk2.reshape(1, -1), sub_g.reshape(-1, 1))


def _mix_route_kernel(att_ref, p_ref, prev_ref, next_ref, x_ref, wpool_ref, pscale_ref, wout_ref, g2_ref,
                      wr_ref, br_ref, tri_ref,
                      x1_ref, hflat_ref, topi_ref, rank_ref, gate_ref, base_ref, cnt_out_ref, cnt_ref,
                      *, tile, seq):
    i = pl.program_id(0)
    tiles_per_seq = seq // tile
    si = i % tiles_per_seq

    @pl.when(i == 0)
    def _():
        cnt_ref[...] = jnp.zeros(cnt_ref.shape, jnp.float32)

    pc = p_ref[...].astype(jnp.float32)
    pp = jnp.where(si == 0, 0.0, prev_ref[...].astype(jnp.float32))
    pn = jnp.where(si == tiles_per_seq - 1, 0.0, next_ref[...].astype(jnp.float32))
    pe = jnp.concatenate([pp, pc, pn], axis=0)
    n_ext = tile + 2 * HALO
    pos = si * tile + lax.broadcasted_iota(jnp.int32, (tile, 1), 0)
    pooled = []
    for g, win in enumerate(POOL_WINDOWS):
        xg = pe[:, g * POOL_GROUP:(g + 1) * POOL_GROUP]
        w = xg + pltpu.roll(xg, 1, 0)
        half = 1
        while 2 * half < win:
            w = pltpu.roll(w, half, 0) + pltpu.roll(w, n_ext - half, 0)
            half *= 2
        lo = jnp.maximum(pos - win // 2, 0)
        hi = jnp.minimum(pos - win // 2 + win - 1, seq - 1)
        mean = w[HALO:HALO + tile] / (hi - lo + 1).astype(jnp.float32)
        dg = (mean - pc[:, g * POOL_GROUP:(g + 1) * POOL_GROUP]).astype(jnp.bfloat16)
        pooled.append(jnp.dot(dg, wpool_ref[g], preferred_element_type=jnp.float32))
    pool = jnp.concatenate(pooled, axis=1) * pscale_ref[...]

    mix = jnp.concatenate([att_ref[...], pool.astype(jnp.bfloat16)], axis=1)
    x1 = x_ref[...] + jnp.dot(mix, wout_ref[...], preferred_element_type=jnp.float32)
    x1_ref[...] = x1

    h2 = _rms(x1, g2_ref[...])
    for c in range(ROW_CHUNKS):
        hflat_ref[pl.ds(c, tile, stride=ROW_CHUNKS), :] = h2[:, c * LANES:(c + 1) * LANES]

    logits = lax.dot_general(wr_ref[...], h2.astype(jnp.bfloat16), _NT,
                             preferred_element_type=jnp.float32) + br_ref[...]
    eio = lax.broadcasted_iota(jnp.int32, logits.shape, 0)
    work = logits
    sel = jnp.zeros(logits.shape, jnp.float32)
    top_v, top_i, hot = [], [], []
    for _ in range(TOP_K):
        mx = jnp.max(work, axis=0, keepdims=True)
        idx = jnp.min(jnp.where(work == mx, eio, N_EXPERTS), axis=0, keepdims=True)
        oh = eio == idx
        top_v.append(mx)
        top_i.append(idx)
        hot.append(oh)
        work = jnp.where(oh, -jnp.inf, work)
        sel = sel + oh.astype(jnp.float32)
    ex = [jnp.exp(v - top_v[0]) for v in top_v]
    den = ex[0] + ex[1] + ex[2] + ex[3]
    ahead = jnp.dot(sel.astype(jnp.bfloat16), tri_ref[...], preferred_element_type=jnp.float32) + cnt_ref[...]
    for j in range(TOP_K):
        topi_ref[j:j + 1, :] = top_i[j]
        gate_ref[j:j + 1, :] = ex[j] / den
        rank_ref[j:j + 1, :] = jnp.sum(jnp.where(hot[j], ahead, 0.0), axis=0, keepdims=True).astype(jnp.int32)
    base_ref[0] = jnp.broadcast_to(cnt_ref[...], base_ref.shape[1:])
    cnt_ref[...] = cnt_ref[...] + jnp.sum(sel, axis=1, keepdims=True)
    cnt_out_ref[...] = jnp.broadcast_to(cnt_ref[...], cnt_out_ref.shape)


def _mix_route(att, proj, x2d, wpool, pscale, wout, g2, wr_t, br, tri, *, seq):
    n = x2d.shape[0]
    t = TOK_TILE
    nt = n // t
    pcol = QK_WIDTH // POOL_WIDTH
    hb = t // HALO
    last_halo = n // HALO - 1
    const = lambda shape: pl.BlockSpec(shape, lambda i: (0,) * len(shape))
    out_shape = (jax.ShapeDtypeStruct((n, D_MODEL), jnp.float32),
                 jax.ShapeDtypeStruct((n * ROW_CHUNKS, LANES), jnp.float32),
                 jax.ShapeDtypeStruct((TOP_K, n), jnp.int32),
                 jax.ShapeDtypeStruct((TOP_K, n), jnp.int32),
                 jax.ShapeDtypeStruct((TOP_K, n), jnp.float32),
                 jax.ShapeDtypeStruct((nt, N_EXPERTS, LANES), jnp.float32),
                 jax.ShapeDtypeStruct((N_EXPERTS, LANES), jnp.float32))
    return pl.pallas_call(
        functools.partial(_mix_route_kernel, tile=t, seq=seq),
        out_shape=out_shape,
        grid=(nt,),
        in_specs=[pl.BlockSpec((t, ATT_WIDTH), lambda i: (i, 0)),
                  pl.BlockSpec((t, POOL_WIDTH), lambda i: (i, pcol)),
                  pl.BlockSpec((HALO, POOL_WIDTH), lambda i: (jnp.maximum(i * hb - 1, 0), pcol)),
                  pl.BlockSpec((HALO, POOL_WIDTH), lambda i: (jnp.minimum((i + 1) * hb, last_halo), pcol)),
                  pl.BlockSpec((t, D_MODEL), lambda i: (i, 0)),
                  const((len(POOL_WINDOWS), POOL_GROUP, POOL_GROUP)),
                  const((1, POOL_WIDTH)),
                  const((D_MODEL, D_MODEL)),
                  const((1, D_MODEL)),
                  const((N_EXPERTS, D_MODEL)),
                  const((N_EXPERTS, 1)),
                  const((t, t))],
        out_specs=(pl.BlockSpec((t, D_MODEL), lambda i: (i, 0)),
                   pl.BlockSpec((t * ROW_CHUNKS, LANES), lambda i: (i, 0)),
                   pl.BlockSpec((TOP_K, t), lambda i: (0, i)),
                   pl.BlockSpec((TOP_K, t), lambda i: (0, i)),
                   pl.BlockSpec((TOP_K, t), lambda i: (0, i)),
                   pl.BlockSpec((1, N_EXPERTS, LANES), lambda i: (i, 0, 0)),
                   pl.BlockSpec((N_EXPERTS, LANES), lambda i: (0, 0))),
        scratch_shapes=[pltpu.VMEM((N_EXPERTS, 1), jnp.float32)],
        compiler_params=pltpu.CompilerParams(dimension_semantics=("arbitrary",),
                                             vmem_limit_bytes=VMEM_LIMIT),
        name="mix_out_route",
    )(att, proj, proj, proj, x2d, wpool, pscale.reshape(1, -1), wout, g2.reshape(1, -1), wr_t,
      br.reshape(-1, 1), tri)


def _row(ref, r):
    return ref.at[pl.ds(pl.multiple_of(r * ROW_CHUNKS, ROW_CHUNKS), ROW_CHUNKS), :]


def _dispatch_kernel(pad_lo_ref, pad_n_ref, dest_hbm, h_ref, xs_out, idx_ref, zero_ref, idx_sem, row_sem, pad_sem,
                     *, tile):
    i = pl.program_id(0)
    per = tile * TOP_K

    @pl.when(i == 0)
    def _():
        zero_ref[...] = jnp.zeros(zero_ref.shape, zero_ref.dtype)

        def zero_rows(first_row, n_rows, act):
            dst = xs_out.at[pl.ds(pl.multiple_of(first_row * ROW_CHUNKS, ROW_CHUNKS), n_rows * ROW_CHUNKS), :]
            act(pltpu.make_async_copy(zero_ref.at[pl.ds(0, n_rows * ROW_CHUNKS), :], dst, pad_sem))

        def each_piece(act):
            def per_expert(e, c):
                row, left = pad_lo_ref[e], pad_n_ref[e]
                piece = EXPERT_ROWS // 2
                while piece >= 1:
                    take = left >= piece

                    @pl.when(take)
                    def _(row=row, piece=piece):
                        zero_rows(row, piece, act)

                    row = row + jnp.where(take, piece, 0)
                    left = left - jnp.where(take, piece, 0)
                    piece //= 2
                return c

            lax.fori_loop(0, N_EXPERTS, per_expert, 0)
            lax.fori_loop(0, pad_n_ref[N_EXPERTS],
                          lambda k, c: (zero_rows(pad_lo_ref[N_EXPERTS] + k * EXPERT_ROWS, EXPERT_ROWS, act), c)[1], 0)

        each_piece(lambda cp: cp.start())
        each_piece(lambda cp: cp.wait())

    idx_copy = pltpu.make_async_copy(dest_hbm.at[pl.ds(pl.multiple_of(i * per, per), per)], idx_ref, idx_sem)
    idx_copy.start()
    idx_copy.wait()

    def body(t, c):
        src = _row(h_ref, t)
        for j in range(TOP_K):
            pltpu.make_async_copy(src, _row(xs_out, idx_ref[t * TOP_K + j]), row_sem).start(priority=j % 2)
        return c

    lax.fori_loop(0, tile, body, 0)
    for _ in range(TOP_K):
        pltpu.make_async_copy(h_ref, xs_out.at[pl.ds(0, tile * ROW_CHUNKS), :], row_sem).wait()


def _dispatch(pad_lo, pad_n, dest_flat, hflat, n_rows):
    t = DISPATCH_TILE
    per = t * TOP_K
    nt = dest_flat.shape[0] // per
    return pl.pallas_call(
        functools.partial(_dispatch_kernel, tile=t),
        out_shape=jax.ShapeDtypeStruct((n_rows * ROW_CHUNKS, LANES), jnp.float32),
        grid_spec=pltpu.PrefetchScalarGridSpec(
            num_scalar_prefetch=2,
            grid=(nt,),
            in_specs=[pl.BlockSpec(memory_space=pl.ANY),
                      pl.BlockSpec((t * ROW_CHUNKS, LANES), lambda i, lo, cnt: (i, 0))],
            out_specs=pl.BlockSpec(memory_space=pl.ANY),
            scratch_shapes=[pltpu.SMEM((per,), jnp.int32),
                            pltpu.VMEM((EXPERT_ROWS * ROW_CHUNKS, LANES), jnp.float32),
                            pltpu.SemaphoreType.DMA,
                            pltpu.SemaphoreType.DMA,
                            pltpu.SemaphoreType.DMA]),
        compiler_params=pltpu.CompilerParams(dimension_semantics=("arbitrary",),
                                             vmem_limit_bytes=VMEM_LIMIT),
        name="moe_dispatch",
    )(pad_lo, pad_n, dest_flat, hflat)


def _expert_kernel(be_ref, used_ref, x_ref, wg_ref, bg_ref, wu_ref, bu_ref, wd_ref, bd_ref, y_ref,
                   wg_bf, wu_bf, wd_bf, *, rows):
    b = pl.program_id(0)
    live = b < used_ref[0]

    @pl.when(jnp.logical_not(live))
    def _():
        y_ref[...] = jnp.zeros(y_ref.shape, y_ref.dtype)

    @pl.when(jnp.logical_and(live, jnp.logical_or(b == 0, be_ref[b] != be_ref[jnp.maximum(b - 1, 0)])))
    def _():
        wg_bf[...] = wg_ref[0, 0].astype(jnp.bfloat16)
        wu_bf[...] = wu_ref[0, 0].astype(jnp.bfloat16)
        wd_bf[...] = wd_ref[0, 0].astype(jnp.bfloat16)

    @pl.when(live)
    def _():
        x = jnp.concatenate([x_ref[pl.ds(c, rows, stride=ROW_CHUNKS), :] for c in range(ROW_CHUNKS)],
                            axis=1).astype(jnp.bfloat16)
        half = D_MODEL // 2
        halves = (slice(0, half), slice(half, D_MODEL))
        gu = [(jnp.dot(x, wg_bf[:, cols], preferred_element_type=jnp.float32) + bg_ref[0, 0, :, cols],
               jnp.dot(x, wu_bf[:, cols], preferred_element_type=jnp.float32) + bu_ref[0, 0, :, cols])
              for cols in halves]
        y = bd_ref[0, 0]
        for (g, u), cols in zip(gu, halves):
            g = jnp.minimum(g, SWIGLU_LIMIT)
            u = jnp.clip(u, -SWIGLU_LIMIT, SWIGLU_LIMIT)
            act = (u + 1.0) * (g * (1.0 / (1.0 + jnp.exp(-SWIGLU_ALPHA * g))))
            y = y + jnp.dot(act.astype(jnp.bfloat16), wd_bf[cols, :], preferred_element_type=jnp.float32)
        for c in range(ROW_CHUNKS):
            y_ref[pl.ds(c, rows, stride=ROW_CHUNKS), :] = y[:, c * LANES:(c + 1) * LANES]


def _experts(block_expert, n_used, xs, wg, bg, wu, bu, wd, bd, *, layer):
    nb = block_expert.shape[0]
    rows = EXPERT_ROWS
    depth = wg.shape[0]
    blk = lambda b, be, used: (jnp.minimum(b, used[0] - 1), 0)
    wsel = lambda b, be, used: (layer, be[b], 0, 0)
    wspec = pl.BlockSpec((1, 1, D_MODEL, D_MODEL), wsel)
    bspec = pl.BlockSpec((1, 1, 1, D_MODEL), wsel)
    bias4 = lambda a: a.reshape(depth, N_EXPERTS, 1, D_MODEL)
    return pl.pallas_call(
        functools.partial(_expert_kernel, rows=rows),
        out_shape=jax.ShapeDtypeStruct(xs.shape, jnp.float32),
        grid_spec=pltpu.PrefetchScalarGridSpec(
            num_scalar_prefetch=2,
            grid=(nb,),
            in_specs=[pl.BlockSpec((rows * ROW_CHUNKS, LANES), blk),
                      wspec, bspec, wspec, bspec, wspec, bspec],
            out_specs=pl.BlockSpec((rows * ROW_CHUNKS, LANES), lambda b, be, used: (b, 0)),
            scratch_shapes=[pltpu.VMEM((D_MODEL, D_MODEL), jnp.bfloat16)] * 3),
        compiler_params=pltpu.CompilerParams(dimension_semantics=("arbitrary",),
                                             vmem_limit_bytes=VMEM_LIMIT),
        name="moe_experts",
    )(block_expert, n_used, xs, wg, bias4(bg), wu, bias4(bu), wd, bias4(bd))


def _combine_kernel(cstart_ref, clen_ref, pchunk_ref, gbound_ref, ys_hbm, x_ref, dest_ref, gate_ref, gf_ref,
                    o_ref, zbuf_ref, acc_ref, destb_ref, gateb_ref, sem, *, tile, final):
    i = pl.program_id(0)
    n_tiles = pl.num_programs(0)
    slab_rows = SLAB_CHUNKS * COMBINE_CHUNK
    chunk_flat = COMBINE_CHUNK * ROW_CHUNKS

    def pair_copies(first_chunk, ring):
        return [pltpu.make_async_copy(
            ys_hbm.at[pl.ds(pl.multiple_of(cstart_ref[first_chunk + q] * ROW_CHUNKS, ROW_CHUNKS), chunk_flat), :],
            zbuf_ref.at[ring, q // SLAB_CHUNKS, pl.ds((q % SLAB_CHUNKS) * chunk_flat, chunk_flat), :],
            sem.at[ring]) for q in range(2 * SLAB_CHUNKS)]

    g0 = gbound_ref[i]
    n_pairs = gbound_ref[i + 1] - g0
    g_total = gbound_ref[n_tiles]

    @pl.when(i == 0)
    def _():
        for cp in pair_copies(pchunk_ref[0], 0):
            cp.start()

        @pl.when(g_total > 1)
        def _():
            for cp in pair_copies(pchunk_ref[1], 1):
                cp.start()

    acc_ref[...] = x_ref[...]
    for j in range(TOP_K):
        destb_ref[j] = jnp.broadcast_to(dest_ref[:, j:j + 1], (tile, LANES))
        gateb_ref[j] = jnp.broadcast_to(gate_ref[:, j:j + 1], (tile, LANES))
    col = lax.broadcasted_iota(jnp.int32, (1, slab_rows), 1)
    col_chunk = col // COMBINE_CHUNK
    col_row = col % COMBINE_CHUNK

    def gate_matrix(first):
        rows = jnp.full((1, slab_rows), -1, jnp.int32)
        for q in range(SLAB_CHUNKS):
            ok = jnp.logical_and(col_chunk == q, col_row < clen_ref[first + q])
            rows = jnp.where(ok, cstart_ref[first + q] + col_row, rows)
        g_cols = []
        for c0 in range(0, slab_rows, LANES):
            rows_c = rows[:, c0:c0 + LANES]
            g_c = jnp.zeros((tile, LANES), jnp.float32)
            for j in range(TOP_K):
                g_c = jnp.where(destb_ref[j] == rows_c, gateb_ref[j], g_c)
            g_cols.append(g_c.astype(jnp.bfloat16))
        return jnp.concatenate(g_cols, axis=1)

    def pair(p, carry):
        g = g0 + p
        ring = g % 2
        first = pchunk_ref[g]
        for cp in pair_copies(first, ring):
            cp.wait()
        ys_pair = [jnp.concatenate([zbuf_ref[ring, slot, pl.ds(c, slab_rows, stride=ROW_CHUNKS), :]
                                    for c in range(ROW_CHUNKS)], axis=1).astype(jnp.bfloat16)
                   for slot in range(2)]

        @pl.when(g + 2 < g_total)
        def _():
            for cp in pair_copies(pchunk_ref[g + 2], ring):
                cp.start()

        for slot in range(2):
            acc_ref[...] += jnp.dot(gate_matrix(first + slot * SLAB_CHUNKS), ys_pair[slot],
                                    preferred_element_type=jnp.float32)
        return carry

    lax.fori_loop(0, n_pairs, pair, 0)
    x2 = acc_ref[...]
    o_ref[...] = _rms(x2, gf_ref[...]) if final else x2


def _combine(chunk_start, chunk_len, pair_chunk, pair_bound, ys, x1, dest_tok, gates_tok, g_final, *, final):
    t = TOK_TILE
    n = x1.shape[0]
    tok = lambda width: pl.BlockSpec((t, width), lambda i, *_: (i, 0))
    return pl.pallas_call(
        functools.partial(_combine_kernel, tile=t, final=final),
        out_shape=jax.ShapeDtypeStruct((n, D_MODEL), jnp.float32),
        grid_spec=pltpu.PrefetchScalarGridSpec(
            num_scalar_prefetch=4,
            grid=(n // t,),
            in_specs=[pl.BlockSpec(memory_space=pl.ANY),
                      tok(D_MODEL), tok(TOP_K), tok(TOP_K),
                      pl.BlockSpec((1, D_MODEL), lambda i, *_: (0, 0))],
            out_specs=tok(D_MODEL),
            scratch_shapes=[pltpu.VMEM((2, 2, SLAB_CHUNKS * COMBINE_CHUNK * ROW_CHUNKS, LANES), jnp.float32),
                            pltpu.VMEM((t, D_MODEL), jnp.float32),
                            pltpu.VMEM((TOP_K, t, LANES), jnp.int32),
                            pltpu.VMEM((TOP_K, t, LANES), jnp.float32),
                            pltpu.SemaphoreType.DMA((2,))]),
        compiler_params=pltpu.CompilerParams(dimension_semantics=("arbitrary",),
                                             vmem_limit_bytes=VMEM_LIMIT),
        name="moe_combine",
    )(chunk_start, chunk_len, pair_chunk, pair_bound, ys, x1, dest_tok, gates_tok, g_final.reshape(1, -1))


def _combine_chunks(base, cnt, seg_start):
    n_run = jnp.concatenate([base[1:], cnt[None]], axis=0) - base
    run_start = seg_start[None, :] + base
    n_chunks = (n_run + COMBINE_CHUNK - 1) // COMBINE_CHUNK
    cum = jnp.cumsum(n_chunks, axis=1)
    total = cum[:, -1]
    slots = jnp.arange(MAX_CHUNKS, dtype=jnp.int32)
    expert = jnp.minimum(jnp.sum(cum[:, None, :] <= slots[None, :, None], axis=2), N_EXPERTS - 1)
    hot = expert[..., None] == jnp.arange(N_EXPERTS, dtype=jnp.int32)
    pick = lambda a: jnp.sum(jnp.where(hot, a[:, None, :], 0), axis=2)
    k = slots[None, :] - (pick(cum) - pick(n_chunks))
    valid = slots[None, :] < total[:, None]
    start = jnp.where(valid, pick(run_start) + COMBINE_CHUNK * k, 0)
    length = jnp.where(valid, jnp.clip(pick(n_run) - COMBINE_CHUNK * k, 0, COMBINE_CHUNK), 0)
    pair_size = 2 * SLAB_CHUNKS
    n_tiles = base.shape[0]
    n_pairs = (total + pair_size - 1) // pair_size
    pair_end = jnp.cumsum(n_pairs)
    pair_bound = jnp.concatenate([jnp.zeros((1,), pair_end.dtype), pair_end])
    g = jnp.arange(n_tiles * (MAX_CHUNKS // pair_size), dtype=jnp.int32)
    tile_of = jnp.minimum(jnp.sum(pair_end[None, :] <= g[:, None], axis=1), n_tiles - 1)
    tile_first = jnp.sum(jnp.where(tile_of[:, None] == jnp.arange(n_tiles), (pair_end - n_pairs)[None, :], 0), axis=1)
    pair_chunk = tile_of * MAX_CHUNKS + pair_size * (g - tile_first)
    i32 = lambda a: a.reshape(-1).astype(jnp.int32)
    return i32(start), i32(length), i32(pair_chunk), i32(pair_bound)


def kernel(x, rel_table, norm1, w_in, lambda_q1, lambda_k1, lambda_q2, lambda_k2, subln_g, w_pool, pool_scale,
           w_out, norm2, w_router, b_router, w_gate, b_gate, w_up, b_up, w_down, b_down, final_norm):
    batch, seq, d = x.shape
    depth = norm1.shape[0]
    n = batch * seq
    assert d == D_MODEL and seq % ATT_TILE == 0 and seq % TOK_TILE == 0
    assert n % DISPATCH_TILE == 0
    bf = jnp.bfloat16

    w_q = w_in[:, :, :QK_WIDTH] * (QK_DIM ** -0.5 * LOG2E)
    w_k = w_in[:, :, QK_WIDTH:2 * QK_WIDTH]
    w_v = w_in[:, :, 2 * QK_WIDTH:2 * QK_WIDTH + ATT_WIDTH]
    w_p = w_in[:, :, 2 * QK_WIDTH + ATT_WIDTH:]
    w_kp = jnp.concatenate([w_k, w_p], axis=2).astype(bf)
    w_qvt = jnp.swapaxes(jnp.concatenate([w_q, w_v], axis=2), 1, 2).astype(bf)
    w_out_b = w_out.astype(bf)
    w_pool_b = w_pool.astype(bf)
    wr_t = jnp.swapaxes(w_router, 1, 2).astype(bf)
    tri = (lax.broadcasted_iota(jnp.int32, (TOK_TILE, TOK_TILE), 0)
           < lax.broadcasted_iota(jnp.int32, (TOK_TILE, TOK_TILE), 1)).astype(bf)

    bias = _bias_tiles(rel_table, ATT_TILE)
    far = jnp.stack([rel_table[N_BUCKETS // 2 - 1], rel_table[N_BUCKETS - 1]], axis=1).reshape(-1)

    n_assign = n * TOP_K
    n_blocks = -(-(n_assign + N_EXPERTS * (EXPERT_ROWS - 1)) // EXPERT_ROWS) + 1
    n_rows = n_blocks * EXPERT_ROWS
    block_start = jnp.arange(n_blocks, dtype=jnp.int32) * EXPERT_ROWS

    x2d = x.reshape(n, d)
    for l in range(depth):
        lam_init = 0.8 - 0.6 * math.exp(-0.3 * l)
        kp, qvt = _norm_proj(x2d, norm1[l], w_kp[l], w_qvt[l])
        att = _attention(kp, qvt, bias, far, lambda_q1[l], lambda_k1[l], lambda_q2[l], lambda_k2[l], subln_g[l],
                         batch=batch, seq=seq, lam_init=lam_init)
        x1, hflat, topi, rank, gates, base, counts = _mix_route(
            att, kp, x2d, w_pool_b[l], pool_scale[l], w_out_b[l], norm2[l], wr_t[l], b_router[l], tri, seq=seq)

        cnt = counts[:, 0].astype(jnp.int32)
        padded = ((cnt + EXPERT_ROWS - 1) // EXPERT_ROWS) * EXPERT_ROWS
        pad_end = jnp.cumsum(padded)
        experts = jnp.arange(N_EXPERTS, dtype=jnp.int32)
        seg_start = pad_end - padded
        dest_tok = (jnp.sum(jnp.where(topi[..., None] == experts, seg_start, 0), axis=-1) + rank).T
        dest_flat = dest_tok.reshape(-1)
        block_expert = jnp.minimum(jnp.sum(pad_end[None, :] <= block_start[:, None], axis=1),
                                   N_EXPERTS - 1).astype(jnp.int32)
        n_used = (pad_end[-1:] // EXPERT_ROWS).astype(jnp.int32)

        pad_lo = jnp.concatenate([pad_end - padded + cnt, pad_end[-1:]])
        pad_n = jnp.concatenate([padded - cnt, (n_rows - pad_end[-1:]) // EXPERT_ROWS])
        xs = _dispatch(pad_lo, pad_n, dest_flat, hflat, n_rows)
        ys = _experts(block_expert, n_used, xs, w_gate, b_gate, w_up, b_up, w_down, b_down, layer=l)
        chunk_lists = _combine_chunks(base[:, :, 0].astype(jnp.int32), cnt, seg_start)
        x2d = _combine(*chunk_lists, ys, x1, dest_tok, gates.T, final_norm,
                       final=(l == depth - 1))
    return x2d.reshape(batch, seq, d)
```

```python
import functools
import math

import jax
import jax.numpy as jnp
from jax import lax
from jax.experimental import pallas as pl
from jax.experimental.pallas import tpu as pltpu

D_MODEL = 1024
N_HEADS = 4
QK_DIM = 64
V_DIM = 128
QK_WIDTH = N_HEADS * 2 * QK_DIM
ATT_WIDTH = N_HEADS * V_DIM
POOL_WINDOWS = (2, 4, 8, 16)
POOL_GROUP = 128
POOL_WIDTH = 512
IN_WIDTH = 2 * QK_WIDTH + ATT_WIDTH + POOL_WIDTH
N_BUCKETS = 32
MAX_DISTANCE = 128
N_EXPERTS = 32
TOP_K = 4
SWIGLU_LIMIT = 7.0
SWIGLU_ALPHA = 1.702
EPS = 1e-5

LANES = 128
SUBLANES = 8
ROW_CHUNKS = D_MODEL // LANES
VMEM_LIMIT = 56 * 1024 * 1024

TOK_TILE = 512
ATT_TILE = 512
ATT_CHUNK = 256
BIAS_TILES = 5
NEAR_TILES = 4
V_EXT = V_DIM + 16
HALO = 16
EXPERT_ROWS = 512
DISPATCH_TILE = 1024
COMBINE_CHUNK = 32
SLAB_CHUNKS = 8
MAX_CHUNKS = TOK_TILE * TOP_K // COMBINE_CHUNK + N_EXPERTS

_NT = (((1,), (1,)), ((), ()))
LOG2E = math.log2(math.e)


def _rms(x, g):
    return x * lax.rsqrt(jnp.mean(x * x, axis=-1, keepdims=True) + EPS) * g


def _bias_kernel(tab_ref, out_ref, *, tile):
    h = pl.program_id(0)
    d = pl.program_id(1)
    key = lax.broadcasted_iota(jnp.int32, (tile, tile), 0)
    qry = lax.broadcasted_iota(jnp.int32, (tile, tile), 1)
    rel = (d - BIAS_TILES // 2) * tile + key - qry
    nb = N_BUCKETS // 2
    max_exact = nb // 2
    ret = jnp.where(rel > 0, nb, 0)
    n = jnp.abs(rel)
    n_f = jnp.maximum(n, 1).astype(jnp.float32)
    large = max_exact + (jnp.log(n_f / max_exact) / math.log(MAX_DISTANCE / max_exact)
                         * (nb - max_exact)).astype(jnp.int32)
    large = jnp.minimum(large, nb - 1)
    bucket = ret + jnp.where(n < max_exact, n, large)
    acc = jnp.zeros((tile, tile), jnp.float32)
    for b in range(N_BUCKETS):
        acc = jnp.where(bucket == b, tab_ref[b * N_HEADS + h], acc)
    out_ref[0, 0] = acc * LOG2E


def _bias_tiles(rel_table, tile):
    return pl.pallas_call(
        functools.partial(_bias_kernel, tile=tile),
        out_shape=jax.ShapeDtypeStruct((N_HEADS, BIAS_TILES, tile, tile), jnp.float32),
        grid=(N_HEADS, BIAS_TILES),
        in_specs=[pl.BlockSpec(memory_space=pltpu.SMEM)],
        out_specs=pl.BlockSpec((1, 1, tile, tile), lambda h, d: (h, d, 0, 0)),
        name="rel_bias_tiles",
    )(rel_table.reshape(-1))


def _norm_proj_kernel(x_ref, g_ref, wkp_ref, wqvt_ref, kp_ref, qvt_ref):
    h = _rms(x_ref[...], g_ref[...]).astype(jnp.bfloat16)
    kp_ref[...] = jnp.dot(h, wkp_ref[...], preferred_element_type=jnp.float32).astype(kp_ref.dtype)
    qvt_ref[...] = lax.dot_general(wqvt_ref[...], h, _NT,
                                   preferred_element_type=jnp.float32).astype(qvt_ref.dtype)


def _norm_proj(x2d, g, w_kp, w_qvt):
    n = x2d.shape[0]
    width = QK_WIDTH + POOL_WIDTH
    width_t = QK_WIDTH + ATT_WIDTH
    return pl.pallas_call(
        _norm_proj_kernel,
        out_shape=(jax.ShapeDtypeStruct((n, width), jnp.bfloat16),
                   jax.ShapeDtypeStruct((width_t, n), jnp.bfloat16)),
        grid=(n // TOK_TILE,),
        in_specs=[pl.BlockSpec((TOK_TILE, D_MODEL), lambda i: (i, 0)),
                  pl.BlockSpec((1, D_MODEL), lambda i: (0, 0)),
                  pl.BlockSpec((D_MODEL, width), lambda i: (0, 0)),
                  pl.BlockSpec((width_t, D_MODEL), lambda i: (0, 0))],
        out_specs=(pl.BlockSpec((TOK_TILE, width), lambda i: (i, 0)),
                   pl.BlockSpec((width_t, TOK_TILE), lambda i: (0, i))),
        compiler_params=pltpu.CompilerParams(dimension_semantics=("parallel",),
                                             vmem_limit_bytes=VMEM_LIMIT),
        name="norm_in_proj",
    )(x2d, g.reshape(1, D_MODEL), w_kp, w_qvt)


def _attn_kernel(far_ref, qt_ref, k_ref, vt_ref, bias_ref, lq1_ref, lk1_ref, lq2_ref, lk2_ref, sg_ref,
                 o_ref, qs_ref, vte_ref, s0_ref, s1_ref, mc0_ref, mc1_ref, m_ref, acc_ref,
                 *, tile, n_kv, lam_init):
    h = pl.program_id(1)
    qi = pl.program_id(2)
    n_far = n_kv - NEAR_TILES

    @pl.when(qi == 0)
    def _():
        vte_ref[:V_DIM, :] = vt_ref[...]
        vte_ref[V_DIM:, :] = jnp.ones((V_EXT - V_DIM, vte_ref.shape[1]), vte_ref.dtype)

    qt = qt_ref[...]
    row = lax.broadcasted_iota(jnp.int32, qt.shape, 0)
    zero = jnp.zeros_like(qt)
    qs_ref[:, :tile] = jnp.where(row < QK_DIM, qt, zero)
    qs_ref[:, tile:] = jnp.where(row >= QK_DIM, qt, zero)

    m_ref[...] = jnp.full(m_ref.shape, -jnp.inf, jnp.float32)
    acc_ref[...] = jnp.zeros(acc_ref.shape, jnp.float32)
    chunks = [slice(c * ATT_CHUNK, (c + 1) * ATT_CHUNK) for c in range(2 * tile // ATT_CHUNK)]
    w0 = jnp.clip(qi - 1, 0, n_kv - NEAR_TILES)
    c_below = far_ref[2 * h] * LOG2E
    c_above = far_ref[2 * h + 1] * LOG2E

    def far_tile(f):
        return jnp.where(f < w0, f, f + NEAR_TILES)

    def far_const(ki):
        return jnp.where(ki < qi, c_below, c_above)

    def logits(ki, cols):
        k = k_ref[pl.ds(pl.multiple_of(ki * tile, tile), tile), :]
        return jnp.dot(k, qs_ref[:, cols], preferred_element_type=jnp.float32)

    def scores_near(ki, cols, s_ref, mc_ref):
        d = jnp.clip(ki - qi, -2, 2) + 2
        q0 = cols.start % tile
        s = logits(ki, cols) + bias_ref[0, d, :, q0:q0 + ATT_CHUNK]
        s_ref[:, cols] = s
        mc_ref[:, cols] = jnp.max(s, axis=0, keepdims=True)

    def scores_far(ki, cols, s_ref, mc_ref):
        s = logits(ki, cols)
        s_ref[:, cols] = s
        mc_ref[:, cols] = jnp.max(s, axis=0, keepdims=True) + far_const(ki)

    def update(ki, cols, s_ref, mc_ref, const):
        vt = vte_ref[:, pl.ds(pl.multiple_of(ki * tile, tile), tile)]
        m_prev = m_ref[:, cols]
        m_new = jnp.maximum(m_prev, mc_ref[:, cols])
        alpha = jnp.exp2(m_prev - m_new)
        p = jnp.exp2(s_ref[:, cols] - (m_new - const))
        acc_ref[:, cols] = alpha * acc_ref[:, cols] + jnp.dot(vt, p.astype(jnp.bfloat16),
                                                              preferred_element_type=jnp.float32)
        m_ref[:, cols] = m_new

    bufs = ((s0_ref, mc0_ref), (s1_ref, mc1_ref))
    for cols in chunks:
        scores_near(w0, cols, *bufs[0])
    for i in range(NEAR_TILES):
        for cols in chunks:
            if i + 1 < NEAR_TILES:
                scores_near(w0 + i + 1, cols, *bufs[(i + 1) % 2])
            elif n_far > 0:
                scores_far(far_tile(0), cols, *bufs[(i + 1) % 2])
            update(w0 + i, cols, *bufs[i % 2], 0.0)

    def pair(j, look_ahead=True):
        ka, kb, kc = far_tile(2 * j), far_tile(2 * j + 1), far_tile(2 * j + 2)
        for cols in chunks:
            scores_far(kb, cols, *bufs[(NEAR_TILES + 1) % 2])
            update(ka, cols, *bufs[NEAR_TILES % 2], far_const(ka))
        for cols in chunks:
            if look_ahead:
                scores_far(kc, cols, *bufs[NEAR_TILES % 2])
            update(kb, cols, *bufs[(NEAR_TILES + 1) % 2], far_const(kb))

    if n_far > 0:
        lax.fori_loop(0, n_far // 2 - 1, lambda j, c: (pair(j), c)[1], 0)
        pair(n_far // 2 - 1, look_ahead=False)

    acc = acc_ref[...]
    o = acc[:V_DIM] / acc[V_DIM:V_DIM + 1]
    lam = (jnp.exp(jnp.sum(lq1_ref[...] * lk1_ref[...], axis=1, keepdims=True))
           - jnp.exp(jnp.sum(lq2_ref[...] * lk2_ref[...], axis=1, keepdims=True)) + lam_init)
    o = o[:, :tile] - lam * o[:, tile:]
    o = o * lax.rsqrt(jnp.mean(o * o, axis=0, keepdims=True) + EPS) * (sg_ref[...] * (1.0 - lam_init))
    o_ref[...] = o.T.astype(o_ref.dtype)


def _attention(kp, qvt, bias, far, lq1, lk1, lq2, lk2, sub_g, *, batch, seq, lam_init):
    t = ATT_TILE
    nq = seq // t
    assert nq >= NEAR_TILES and (nq - NEAR_TILES) % 2 == 0
    vrow = QK_WIDTH // LANES
    vec = lambda width: pl.BlockSpec((1, width), lambda b, h, qi: (0, 0))
    return pl.pallas_call(
        functools.partial(_attn_kernel, tile=t, n_kv=nq, lam_init=lam_init),
        out_shape=jax.ShapeDtypeStruct((batch * seq, ATT_WIDTH), jnp.bfloat16),
        grid=(batch, N_HEADS, nq),
        in_specs=[pl.BlockSpec(memory_space=pltpu.SMEM),
                  pl.BlockSpec((LANES, t), lambda b, h, qi: (h, b * nq + qi)),
                  pl.BlockSpec((seq, LANES), lambda b, h, qi: (b, h)),
                  pl.BlockSpec((V_DIM, seq), lambda b, h, qi: (vrow + h, b)),
                  pl.BlockSpec((1, BIAS_TILES, t, t), lambda b, h, qi: (h, 0, 0, 0)),
                  vec(QK_DIM), vec(QK_DIM), vec(QK_DIM), vec(QK_DIM),
                  pl.BlockSpec((V_DIM, 1), lambda b, h, qi: (0, 0))],
        out_specs=pl.BlockSpec((t, V_DIM), lambda b, h, qi: (b * nq + qi, h)),
        scratch_shapes=[pltpu.VMEM((LANES, 2 * t), jnp.bfloat16),
                        pltpu.VMEM((V_EXT, seq), jnp.bfloat16),
                        pltpu.VMEM((t, 2 * t), jnp.float32),
                        pltpu.VMEM((t, 2 * t), jnp.float32),
                        pltpu.VMEM((1, 2 * t), jnp.float32),
                        pltpu.VMEM((1, 2 * t), jnp.float32),
                        pltpu.VMEM((1, 2 * t), jnp.float32),
                        pltpu.VMEM((V_EXT, 2 * t), jnp.float32)],
        compiler_params=pltpu.CompilerParams(dimension_semantics=("arbitrary", "arbitrary", "arbitrary"),
                                             vmem_limit_bytes=VMEM_LIMIT),
        name="diff_attention",
    )(far, qvt, kp, qvt, bias, lq1.reshape(1, -1), lk1.reshape(1, -1), lq2.reshape(1, -1),
      lk2.reshape(1, -1), sub_g.reshape(-1, 1))


def _mix_route_kernel(att_ref, p_ref, prev_ref, next_ref, x_ref, wpool_ref, pscale_ref, wout_ref, g2_ref,
                      wr_ref, br_ref, tri_ref,
                      x1_ref, hflat_ref, topi_ref, rank_ref, gate_ref, base_ref, cnt_out_ref, cnt_ref,
                      *, tile, seq):
    i = pl.program_id(0)
    tiles_per_seq = seq // tile
    si = i % tiles_per_seq

    @pl.when(i == 0)
    def _():
        cnt_ref[...] = jnp.zeros(cnt_ref.shape, jnp.float32)

    pc = p_ref[...].astype(jnp.float32)
    pp = jnp.where(si == 0, 0.0, prev_ref[...].astype(jnp.float32))
    pn = jnp.where(si == tiles_per_seq - 1, 0.0, next_ref[...].astype(jnp.float32))
    pe = jnp.concatenate([pp, pc, pn], axis=0)
    n_ext = tile + 2 * HALO
    pos = si * tile + lax.broadcasted_iota(jnp.int32, (tile, 1), 0)
    pooled = []
    for g, win in enumerate(POOL_WINDOWS):
        xg = pe[:, g * POOL_GROUP:(g + 1) * POOL_GROUP]
        w = xg + pltpu.roll(xg, 1, 0)
        half = 1
        while 2 * half < win:
            w = pltpu.roll(w, half, 0) + pltpu.roll(w, n_ext - half, 0)
            half *= 2
        lo = jnp.maximum(pos - win // 2, 0)
        hi = jnp.minimum(pos - win // 2 + win - 1, seq - 1)
        mean = w[HALO:HALO + tile] / (hi - lo + 1).astype(jnp.float32)
        dg = (mean - pc[:, g * POOL_GROUP:(g + 1) * POOL_GROUP]).astype(jnp.bfloat16)
        pooled.append(jnp.dot(dg, wpool_ref[g], preferred_element_type=jnp.float32))
    pool = jnp.concatenate(pooled, axis=1) * pscale_ref[...]

    mix = jnp.concatenate([att_ref[...], pool.astype(jnp.bfloat16)], axis=1)
    x1 = x_ref[...] + jnp.dot(mix, wout_ref[...], preferred_element_type=jnp.float32)
    x1_ref[...] = x1

    h2 = _rms(x1, g2_ref[...])
    for c in range(ROW_CHUNKS):
        hflat_ref[pl.ds(c, tile, stride=ROW_CHUNKS), :] = h2[:, c * LANES:(c + 1) * LANES]

    logits = lax.dot_general(wr_ref[...], h2.astype(jnp.bfloat16), _NT,
                             preferred_element_type=jnp.float32) + br_ref[...]
    eio = lax.broadcasted_iota(jnp.int32, logits.shape, 0)
    work = logits
    sel = jnp.zeros(logits.shape, jnp.float32)
    top_v, top_i, hot = [], [], []
    for _ in range(TOP_K):
        mx = jnp.max(work, axis=0, keepdims=True)
        idx = jnp.min(jnp.where(work == mx, eio, N_EXPERTS), axis=0, keepdims=True)
        oh = eio == idx
        top_v.append(mx)
        top_i.append(idx)
        hot.append(oh)
        work = jnp.where(oh, -jnp.inf, work)
        sel = sel + oh.astype(jnp.float32)
    ex = [jnp.exp(v - top_v[0]) for v in top_v]
    den = ex[0] + ex[1] + ex[2] + ex[3]
    ahead = jnp.dot(sel.astype(jnp.bfloat16), tri_ref[...], preferred_element_type=jnp.float32) + cnt_ref[...]
    for j in range(TOP_K):
        topi_ref[j:j + 1, :] = top_i[j]
        gate_ref[j:j + 1, :] = ex[j] / den
        rank_ref[j:j + 1, :] = jnp.sum(jnp.where(hot[j], ahead, 0.0), axis=0, keepdims=True).astype(jnp.int32)
    base_ref[0] = jnp.broadcast_to(cnt_ref[...], base_ref.shape[1:])
    cnt_ref[...] = cnt_ref[...] + jnp.sum(sel, axis=1, keepdims=True)
    cnt_out_ref[...] = jnp.broadcast_to(cnt_ref[...], cnt_out_ref.shape)


def _mix_route(att, proj, x2d, wpool, pscale, wout, g2, wr_t, br, tri, *, seq):
    n = x2d.shape[0]
    t = TOK_TILE
    nt = n // t
    pcol = QK_WIDTH // POOL_WIDTH
    hb = t // HALO
    last_halo = n // HALO - 1
    const = lambda shape: pl.BlockSpec(shape, lambda i: (0,) * len(shape))
    out_shape = (jax.ShapeDtypeStruct((n, D_MODEL), jnp.float32),
                 jax.ShapeDtypeStruct((n * ROW_CHUNKS, LANES), jnp.float32),
                 jax.ShapeDtypeStruct((TOP_K, n), jnp.int32),
                 jax.ShapeDtypeStruct((TOP_K, n), jnp.int32),
                 jax.ShapeDtypeStruct((TOP_K, n), jnp.float32),
                 jax.ShapeDtypeStruct((nt, N_EXPERTS, LANES), jnp.float32),
                 jax.ShapeDtypeStruct((N_EXPERTS, LANES), jnp.float32))
    return pl.pallas_call(
        functools.partial(_mix_route_kernel, tile=t, seq=seq),
        out_shape=out_shape,
        grid=(nt,),
        in_specs=[pl.BlockSpec((t, ATT_WIDTH), lambda i: (i, 0)),
                  pl.BlockSpec((t, POOL_WIDTH), lambda i: (i, pcol)),
                  pl.BlockSpec((HALO, POOL_WIDTH), lambda i: (jnp.maximum(i * hb - 1, 0), pcol)),
                  pl.BlockSpec((HALO, POOL_WIDTH), lambda i: (jnp.minimum((i + 1) * hb, last_halo), pcol)),
                  pl.BlockSpec((t, D_MODEL), lambda i: (i, 0)),
                  const((len(POOL_WINDOWS), POOL_GROUP, POOL_GROUP)),
                  const((1, POOL_WIDTH)),
                  const((D_MODEL, D_MODEL)),
                  const((1, D_MODEL)),
                  const((N_EXPERTS, D_MODEL)),
                  const((N_EXPERTS, 1)),
                  const((t, t))],
        out_specs=(pl.BlockSpec((t, D_MODEL), lambda i: (i, 0)),
                   pl.BlockSpec((t * ROW_CHUNKS, LANES), lambda i: (i, 0)),
                   pl.BlockSpec((TOP_K, t), lambda i: (0, i)),
                   pl.BlockSpec((TOP_K, t), lambda i: (0, i)),
                   pl.BlockSpec((TOP_K, t), lambda i: (0, i)),
                   pl.BlockSpec((1, N_EXPERTS, LANES), lambda i: (i, 0, 0)),
                   pl.BlockSpec((N_EXPERTS, LANES), lambda i: (0, 0))),
        scratch_shapes=[pltpu.VMEM((N_EXPERTS, 1), jnp.float32)],
        compiler_params=pltpu.CompilerParams(dimension_semantics=("arbitrary",),
                                             vmem_limit_bytes=VMEM_LIMIT),
        name="mix_out_route",
    )(att, proj, proj, proj, x2d, wpool, pscale.reshape(1, -1), wout, g2.reshape(1, -1), wr_t,
      br.reshape(-1, 1), tri)


def _row(ref, r):
    return ref.at[pl.ds(pl.multiple_of(r * ROW_CHUNKS, ROW_CHUNKS), ROW_CHUNKS), :]


def _dispatch_kernel(pad_lo_ref, pad_n_ref, dest_hbm, h_ref, xs_out, idx_ref, zero_ref, idx_sem, row_sem, pad_sem,
                     *, tile):
    i = pl.program_id(0)
    per = tile * TOP_K

    @pl.when(i == 0)
    def _():
        zero_ref[...] = jnp.zeros(zero_ref.shape, zero_ref.dtype)

        def zero_rows(first_row, n_rows, act):
            dst = xs_out.at[pl.ds(pl.multiple_of(first_row * ROW_CHUNKS, ROW_CHUNKS), n_rows * ROW_CHUNKS), :]
            act(pltpu.make_async_copy(zero_ref.at[pl.ds(0, n_rows * ROW_CHUNKS), :], dst, pad_sem))

        def each_piece(act):
            def per_expert(e, c):
                row, left = pad_lo_ref[e], pad_n_ref[e]
                piece = EXPERT_ROWS // 2
                while piece >= 1:
                    take = left >= piece

                    @pl.when(take)
                    def _(row=row, piece=piece):
                        zero_rows(row, piece, act)

                    row = row + jnp.where(take, piece, 0)
                    left = left - jnp.where(take, piece, 0)
                    piece //= 2
                return c

            lax.fori_loop(0, N_EXPERTS, per_expert, 0)
            lax.fori_loop(0, pad_n_ref[N_EXPERTS],
                          lambda k, c: (zero_rows(pad_lo_ref[N_EXPERTS] + k * EXPERT_ROWS, EXPERT_ROWS, act), c)[1], 0)

        each_piece(lambda cp: cp.start())
        each_piece(lambda cp: cp.wait())

    idx_copy = pltpu.make_async_copy(dest_hbm.at[pl.ds(pl.multiple_of(i * per, per), per)], idx_ref, idx_sem)
    idx_copy.start()
    idx_copy.wait()

    def body(t, c):
        src = _row(h_ref, t)
        for j in range(TOP_K):
            pltpu.make_async_copy(src, _row(xs_out, idx_ref[t * TOP_K + j]), row_sem).start(priority=j % 2)
        return c

    lax.fori_loop(0, tile, body, 0)
    for _ in range(TOP_K):
        pltpu.make_async_copy(h_ref, xs_out.at[pl.ds(0, tile * ROW_CHUNKS), :], row_sem).wait()


def _dispatch(pad_lo, pad_n, dest_flat, hflat, n_rows):
    t = DISPATCH_TILE
    per = t * TOP_K
    nt = dest_flat.shape[0] // per
    return pl.pallas_call(
        functools.partial(_dispatch_kernel, tile=t),
        out_shape=jax.ShapeDtypeStruct((n_rows * ROW_CHUNKS, LANES), jnp.float32),
        grid_spec=pltpu.PrefetchScalarGridSpec(
            num_scalar_prefetch=2,
            grid=(nt,),
            in_specs=[pl.BlockSpec(memory_space=pl.ANY),
                      pl.BlockSpec((t * ROW_CHUNKS, LANES), lambda i, lo, cnt: (i, 0))],
            out_specs=pl.BlockSpec(memory_space=pl.ANY),
            scratch_shapes=[pltpu.SMEM((per,), jnp.int32),
                            pltpu.VMEM((EXPERT_ROWS * ROW_CHUNKS, LANES), jnp.float32),
                            pltpu.SemaphoreType.DMA,
                            pltpu.SemaphoreType.DMA,
                            pltpu.SemaphoreType.DMA]),
        compiler_params=pltpu.CompilerParams(dimension_semantics=("arbitrary",),
                                             vmem_limit_bytes=VMEM_LIMIT),
        name="moe_dispatch",
    )(pad_lo, pad_n, dest_flat, hflat)


def _expert_kernel(be_ref, used_ref, x_ref, wg_ref, bg_ref, wu_ref, bu_ref, wd_ref, bd_ref, y_ref,
                   wg_bf, wu_bf, wd_bf, *, rows):
    b = pl.program_id(0)
    live = b < used_ref[0]

    @pl.when(jnp.logical_not(live))
    def _():
        y_ref[...] = jnp.zeros(y_ref.shape, y_ref.dtype)

    @pl.when(jnp.logical_and(live, jnp.logical_or(b == 0, be_ref[b] != be_ref[jnp.maximum(b - 1, 0)])))
    def _():
        wg_bf[...] = wg_ref[0, 0].astype(jnp.bfloat16)
        wu_bf[...] = wu_ref[0, 0].astype(jnp.bfloat16)
        wd_bf[...] = wd_ref[0, 0].astype(jnp.bfloat16)

    @pl.when(live)
    def _():
        x = jnp.concatenate([x_ref[pl.ds(c, rows, stride=ROW_CHUNKS), :] for c in range(ROW_CHUNKS)],
                            axis=1).astype(jnp.bfloat16)
        half = D_MODEL // 2
        halves = (slice(0, half), slice(half, D_MODEL))
        gu = [(jnp.dot(x, wg_bf[:, cols], preferred_element_type=jnp.float32) + bg_ref[0, 0, :, cols],
               jnp.dot(x, wu_bf[:, cols], preferred_element_type=jnp.float32) + bu_ref[0, 0, :, cols])
              for cols in halves]
        y = bd_ref[0, 0]
        for (g, u), cols in zip(gu, halves):
            g = jnp.minimum(g, SWIGLU_LIMIT)
            u = jnp.clip(u, -SWIGLU_LIMIT, SWIGLU_LIMIT)
            act = (u + 1.0) * (g * (1.0 / (1.0 + jnp.exp(-SWIGLU_ALPHA * g))))
            y = y + jnp.dot(act.astype(jnp.bfloat16), wd_bf[cols, :], preferred_element_type=jnp.float32)
        for c in range(ROW_CHUNKS):
            y_ref[pl.ds(c, rows, stride=ROW_CHUNKS), :] = y[:, c * LANES:(c + 1) * LANES]


def _experts(block_expert, n_used, xs, wg, bg, wu, bu, wd, bd, *, layer):
    nb = block_expert.shape[0]
    rows = EXPERT_ROWS
    depth = wg.shape[0]
    blk = lambda b, be, used: (jnp.minimum(b, used[0] - 1), 0)
    wsel = lambda b, be, used: (layer, be[b], 0, 0)
    wspec = pl.BlockSpec((1, 1, D_MODEL, D_MODEL), wsel)
    bspec = pl.BlockSpec((1, 1, 1, D_MODEL), wsel)
    bias4 = lambda a: a.reshape(depth, N_EXPERTS, 1, D_MODEL)
    return pl.pallas_call(
        functools.partial(_expert_kernel, rows=rows),
        out_shape=jax.ShapeDtypeStruct(xs.shape, jnp.float32),
        grid_spec=pltpu.PrefetchScalarGridSpec(
            num_scalar_prefetch=2,
            grid=(nb,),
            in_specs=[pl.BlockSpec((rows * ROW_CHUNKS, LANES), blk),
                      wspec, bspec, wspec, bspec, wspec, bspec],
            out_specs=pl.BlockSpec((rows * ROW_CHUNKS, LANES), lambda b, be, used: (b, 0)),
            scratch_shapes=[pltpu.VMEM((D_MODEL, D_MODEL), jnp.bfloat16)] * 3),
        compiler_params=pltpu.CompilerParams(dimension_semantics=("arbitrary",),
                                             vmem_limit_bytes=VMEM_LIMIT),
        name="moe_experts",
    )(block_expert, n_used, xs, wg, bias4(bg), wu, bias4(bu), wd, bias4(bd))


def _combine_kernel(cstart_ref, clen_ref, pchunk_ref, gbound_ref, ys_hbm, x_ref, dest_ref, gate_ref, gf_ref,
                    o_ref, zbuf_ref, acc_ref, destb_ref, gateb_ref, sem, *, tile, final):
    i = pl.program_id(0)
    n_tiles = pl.num_programs(0)
    slab_rows = SLAB_CHUNKS * COMBINE_CHUNK
    chunk_flat = COMBINE_CHUNK * ROW_CHUNKS

    def pair_copies(first_chunk, ring):
        return [pltpu.make_async_copy(
            ys_hbm.at[pl.ds(pl.multiple_of(cstart_ref[first_chunk + q] * ROW_CHUNKS, ROW_CHUNKS), chunk_flat), :],
            zbuf_ref.at[ring, q // SLAB_CHUNKS, pl.ds((q % SLAB_CHUNKS) * chunk_flat, chunk_flat), :],
            sem.at[ring]) for q in range(2 * SLAB_CHUNKS)]

    g0 = gbound_ref[i]
    n_pairs = gbound_ref[i + 1] - g0
    g_total = gbound_ref[n_tiles]

    @pl.when(i == 0)
    def _():
        for cp in pair_copies(pchunk_ref[0], 0):
            cp.start()

        @pl.when(g_total > 1)
        def _():
            for cp in pair_copies(pchunk_ref[1], 1):
                cp.start()

    acc_ref[...] = x_ref[...]
    for j in range(TOP_K):
        destb_ref[j] = jnp.broadcast_to(dest_ref[:, j:j + 1], (tile, LANES))
        gateb_ref[j] = jnp.broadcast_to(gate_ref[:, j:j + 1], (tile, LANES))
    col = lax.broadcasted_iota(jnp.int32, (1, slab_rows), 1)
    col_chunk = col // COMBINE_CHUNK
    col_row = col % COMBINE_CHUNK

    def gate_matrix(first):
        rows = jnp.full((1, slab_rows), -1, jnp.int32)
        for q in range(SLAB_CHUNKS):
            ok = jnp.logical_and(col_chunk == q, col_row < clen_ref[first + q])
            rows = jnp.where(ok, cstart_ref[first + q] + col_row, rows)
        g_cols = []
        for c0 in range(0, slab_rows, LANES):
            rows_c = rows[:, c0:c0 + LANES]
            g_c = jnp.zeros((tile, LANES), jnp.float32)
            for j in range(TOP_K):
                g_c = jnp.where(destb_ref[j] == rows_c, gateb_ref[j], g_c)
            g_cols.append(g_c.astype(jnp.bfloat16))
        return jnp.concatenate(g_cols, axis=1)

    def pair(p, carry):
        g = g0 + p
        ring = g % 2
        first = pchunk_ref[g]
        for cp in pair_copies(first, ring):
            cp.wait()
        ys_pair = [jnp.concatenate([zbuf_ref[ring, slot, pl.ds(c, slab_rows, stride=ROW_CHUNKS), :]
                                    for c in range(ROW_CHUNKS)], axis=1).astype(jnp.bfloat16)
                   for slot in range(2)]

        @pl.when(g + 2 < g_total)
        def _():
            for cp in pair_copies(pchunk_ref[g + 2], ring):
                cp.start()

        for slot in range(2):
            acc_ref[...] += jnp.dot(gate_matrix(first + slot * SLAB_CHUNKS), ys_pair[slot],
                                    preferred_element_type=jnp.float32)
        return carry

    lax.fori_loop(0, n_pairs, pair, 0)
    x2 = acc_ref[...]
    o_ref[...] = _rms(x2, gf_ref[...]) if final else x2


def _combine(chunk_start, chunk_len, pair_chunk, pair_bound, ys, x1, dest_tok, gates_tok, g_final, *, final):
    t = TOK_TILE
    n = x1.shape[0]
    tok = lambda width: pl.BlockSpec((t, width), lambda i, *_: (i, 0))
    return pl.pallas_call(
        functools.partial(_combine_kernel, tile=t, final=final),
        out_shape=jax.ShapeDtypeStruct((n, D_MODEL), jnp.float32),
        grid_spec=pltpu.PrefetchScalarGridSpec(
            num_scalar_prefetch=4,
            grid=(n // t,),
            in_specs=[pl.BlockSpec(memory_space=pl.ANY),
                      tok(D_MODEL), tok(TOP_K), tok(TOP_K),
                      pl.BlockSpec((1, D_MODEL), lambda i, *_: (0, 0))],
            out_specs=tok(D_MODEL),
            scratch_shapes=[pltpu.VMEM((2, 2, SLAB_CHUNKS * COMBINE_CHUNK * ROW_CHUNKS, LANES), jnp.float32),
                            pltpu.VMEM((t, D_MODEL), jnp.float32),
                            pltpu.VMEM((TOP_K, t, LANES), jnp.int32),
                            pltpu.VMEM((TOP_K, t, LANES), jnp.float32),
                            pltpu.SemaphoreType.DMA((2,))]),
        compiler_params=pltpu.CompilerParams(dimension_semantics=("arbitrary",),
                                             vmem_limit_bytes=VMEM_LIMIT),
        name="moe_combine",
    )(chunk_start, chunk_len, pair_chunk, pair_bound, ys, x1, dest_tok, gates_tok, g_final.reshape(1, -1))


def _combine_chunks(base, cnt, seg_start):
    n_run = jnp.concatenate([base[1:], cnt[None]], axis=0) - base
    run_start = seg_start[None, :] + base
    n_chunks = (n_run + COMBINE_CHUNK - 1) // COMBINE_CHUNK
    cum = jnp.cumsum(n_chunks, axis=1)
    total = cum[:, -1]
    slots = jnp.arange(MAX_CHUNKS, dtype=jnp.int32)
    expert = jnp.minimum(jnp.sum(cum[:, None, :] <= slots[None, :, None], axis=2), N_EXPERTS - 1)
    hot = expert[..., None] == jnp.arange(N_EXPERTS, dtype=jnp.int32)
    pick = lambda a: jnp.sum(jnp.where(hot, a[:, None, :], 0), axis=2)
    k = slots[None, :] - (pick(cum) - pick(n_chunks))
    valid = slots[None, :] < total[:, None]
    start = jnp.where(valid, pick(run_start) + COMBINE_CHUNK * k, 0)
    length = jnp.where(valid, jnp.clip(pick(n_run) - COMBINE_CHUNK * k, 0, COMBINE_CHUNK), 0)
    pair_size = 2 * SLAB_CHUNKS
    n_tiles = base.shape[0]
    n_pairs = (total + pair_size - 1) // pair_size
    pair_end = jnp.cumsum(n_pairs)
    pair_bound = jnp.concatenate([jnp.zeros((1,), pair_end.dtype), pair_end])
    g = jnp.arange(n_tiles * (MAX_CHUNKS // pair_size), dtype=jnp.int32)
    tile_of = jnp.minimum(jnp.sum(pair_end[None, :] <= g[:, None], axis=1), n_tiles - 1)
    tile_first = jnp.sum(jnp.where(tile_of[:, None] == jnp.arange(n_tiles), (pair_end - n_pairs)[None, :], 0), axis=1)
    pair_chunk = tile_of * MAX_CHUNKS + pair_size * (g - tile_first)
    i32 = lambda a: a.reshape(-1).astype(jnp.int32)
    return i32(start), i32(length), i32(pair_chunk), i32(pair_bound)


def kernel(x, rel_table, norm1, w_in, lambda_q1, lambda_k1, lambda_q2, lambda_k2, subln_g, w_pool, pool_scale,
           w_out, norm2, w_router, b_router, w_gate, b_gate, w_up, b_up, w_down, b_down, final_norm):
    batch, seq, d = x.shape
    depth = norm1.shape[0]
    n = batch * seq
    assert d == D_MODEL and seq % ATT_TILE == 0 and seq % TOK_TILE == 0
    assert n % DISPATCH_TILE == 0
    bf = jnp.bfloat16

    w_q = w_in[:, :, :QK_WIDTH] * (QK_DIM ** -0.5 * LOG2E)
    w_k = w_in[:, :, QK_WIDTH:2 * QK_WIDTH]
    w_v = w_in[:, :, 2 * QK_WIDTH:2 * QK_WIDTH + ATT_WIDTH]
    w_p = w_in[:, :, 2 * QK_WIDTH + ATT_WIDTH:]
    w_kp = jnp.concatenate([w_k, w_p], axis=2).astype(bf)
    w_qvt = jnp.swapaxes(jnp.concatenate([w_q, w_v], axis=2), 1, 2).astype(bf)
    w_out_b = w_out.astype(bf)
    w_pool_b = w_pool.astype(bf)
    wr_t = jnp.swapaxes(w_router, 1, 2).astype(bf)
    tri = (lax.broadcasted_iota(jnp.int32, (TOK_TILE, TOK_TILE), 0)
           < lax.broadcasted_iota(jnp.int32, (TOK_TILE, TOK_TILE), 1)).astype(bf)

    bias = _bias_tiles(rel_table, ATT_TILE)
    far = jnp.stack([rel_table[N_BUCKETS // 2 - 1], rel_table[N_BUCKETS - 1]], axis=1).reshape(-1)

    n_assign = n * TOP_K
    n_blocks = -(-(n_assign + N_EXPERTS * (EXPERT_ROWS - 1)) // EXPERT_ROWS) + 1
    n_rows = n_blocks * EXPERT_ROWS
    block_start = jnp.arange(n_blocks, dtype=jnp.int32) * EXPERT_ROWS

    x2d = x.reshape(n, d)
    for l in range(depth):
        lam_init = 0.8 - 0.6 * math.exp(-0.3 * l)
        kp, qvt = _norm_proj(x2d, norm1[l], w_kp[l], w_qvt[l])
        att = _attention(kp, qvt, bias, far, lambda_q1[l], lambda_k1[l], lambda_q2[l], lambda_k2[l], subln_g[l],
                         batch=batch, seq=seq, lam_init=lam_init)
        x1, hflat, topi, rank, gates, base, counts = _mix_route(
            att, kp, x2d, w_pool_b[l], pool_scale[l], w_out_b[l], norm2[l], wr_t[l], b_router[l], tri, seq=seq)

        cnt = counts[:, 0].astype(jnp.int32)
        padded = ((cnt + EXPERT_ROWS - 1) // EXPERT_ROWS) * EXPERT_ROWS
        pad_end = jnp.cumsum(padded)
        experts = jnp.arange(N_EXPERTS, dtype=jnp.int32)
        seg_start = pad_end - padded
        dest_tok = (jnp.sum(jnp.where(topi[..., None] == experts, seg_start, 0), axis=-1) + rank).T
        dest_flat = dest_tok.reshape(-1)
        block_expert = jnp.minimum(jnp.sum(pad_end[None, :] <= block_start[:, None], axis=1),
                                   N_EXPERTS - 1).astype(jnp.int32)
        n_used = (pad_end[-1:] // EXPERT_ROWS).astype(jnp.int32)

        pad_lo = jnp.concatenate([pad_end - padded + cnt, pad_end[-1:]])
        pad_n = jnp.concatenate([padded - cnt, (n_rows - pad_end[-1:]) // EXPERT_ROWS])
        xs = _dispatch(pad_lo, pad_n, dest_flat, hflat, n_rows)
        ys = _experts(block_expert, n_used, xs, w_gate, b_gate, w_up, b_up, w_down, b_down, layer=l)
        chunk_lists = _combine_chunks(base[:, :, 0].astype(jnp.int32), cnt, seg_start)
        x2d = _combine(*chunk_lists, ys, x1, dest_tok, gates.T, final_norm,
                       final=(l == depth - 1))
    return x2d.reshape(batch, seq, d)
```

```python
import functools
import math

import jax
import jax.numpy as jnp
from jax import lax
from jax.experimental import pallas as pl
from jax.experimental.pallas import tpu as pltpu

D_MODEL = 1024
N_HEADS = 4
QK_DIM = 64
V_DIM = 128
QK_WIDTH = N_HEADS * 2 * QK_DIM
ATT_WIDTH = N_HEADS * V_DIM
POOL_WINDOWS = (2, 4, 8, 16)
POOL_GROUP = 128
POOL_WIDTH = 512
IN_WIDTH = 2 * QK_WIDTH + ATT_WIDTH + POOL_WIDTH
N_BUCKETS = 32
MAX_DISTANCE = 128
N_EXPERTS = 32
TOP_K = 4
SWIGLU_LIMIT = 7.0
SWIGLU_ALPHA = 1.702
EPS = 1e-5

LANES = 128
SUBLANES = 8
ROW_CHUNKS = D_MODEL // LANES
VMEM_LIMIT = 56 * 1024 * 1024

TOK_TILE = 512
ATT_TILE = 512
ATT_CHUNK = 256
BIAS_TILES = 5
NEAR_TILES = 4
V_EXT = V_DIM + 16
HALO = 16
EXPERT_ROWS = 512
DISPATCH_TILE = 2048
COMBINE_CHUNK = 32
SLAB_CHUNKS = 8
MAX_CHUNKS = TOK_TILE * TOP_K // COMBINE_CHUNK + N_EXPERTS

_NT = (((1,), (1,)), ((), ()))
LOG2E = math.log2(math.e)


def _rms(x, g):
    return x * lax.rsqrt(jnp.mean(x * x, axis=-1, keepdims=True) + EPS) * g


def _bias_kernel(tab_ref, out_ref, *, tile):
    h = pl.program_id(0)
    d = pl.program_id(1)
    key = lax.broadcasted_iota(jnp.int32, (tile, tile), 0)
    qry = lax.broadcasted_iota(jnp.int32, (tile, tile), 1)
    rel = (d - BIAS_TILES // 2) * tile + key - qry
    nb = N_BUCKETS // 2
    max_exact = nb // 2
    ret = jnp.where(rel > 0, nb, 0)
    n = jnp.abs(rel)
    n_f = jnp.maximum(n, 1).astype(jnp.float32)
    large = max_exact + (jnp.log(n_f / max_exact) / math.log(MAX_DISTANCE / max_exact)
                         * (nb - max_exact)).astype(jnp.int32)
    large = jnp.minimum(large, nb - 1)
    bucket = ret + jnp.where(n < max_exact, n, large)
    acc = jnp.zeros((tile, tile), jnp.float32)
    for b in range(N_BUCKETS):
        acc = jnp.where(bucket == b, tab_ref[b * N_HEADS + h], acc)
    out_ref[0, 0] = acc * LOG2E


def _bias_tiles(rel_table, tile):
    return pl.pallas_call(
        functools.partial(_bias_kernel, tile=tile),
        out_shape=jax.ShapeDtypeStruct((N_HEADS, BIAS_TILES, tile, tile), jnp.float32),
        grid=(N_HEADS, BIAS_TILES),
        in_specs=[pl.BlockSpec(memory_space=pltpu.SMEM)],
        out_specs=pl.BlockSpec((1, 1, tile, tile), lambda h, d: (h, d, 0, 0)),
        name="rel_bias_tiles",
    )(rel_table.reshape(-1))


def _norm_proj_kernel(x_ref, g_ref, wkp_ref, wqvt_ref, kp_ref, qvt_ref):
    h = _rms(x_ref[...], g_ref[...]).astype(jnp.bfloat16)
    kp_ref[...] = jnp.dot(h, wkp_ref[...], preferred_element_type=jnp.float32).astype(kp_ref.dtype)
    qvt_ref[...] = lax.dot_general(wqvt_ref[...], h, _NT,
                                   preferred_element_type=jnp.float32).astype(qvt_ref.dtype)


def _norm_proj(x2d, g, w_kp, w_qvt):
    n = x2d.shape[0]
    width = QK_WIDTH + POOL_WIDTH
    width_t = QK_WIDTH + ATT_WIDTH
    return pl.pallas_call(
        _norm_proj_kernel,
        out_shape=(jax.ShapeDtypeStruct((n, width), jnp.bfloat16),
                   jax.ShapeDtypeStruct((width_t, n), jnp.bfloat16)),
        grid=(n // TOK_TILE,),
        in_specs=[pl.BlockSpec((TOK_TILE, D_MODEL), lambda i: (i, 0)),
                  pl.BlockSpec((1, D_MODEL), lambda i: (0, 0)),
                  pl.BlockSpec((D_MODEL, width), lambda i: (0, 0)),
                  pl.BlockSpec((width_t, D_MODEL), lambda i: (0, 0))],
        out_specs=(pl.BlockSpec((TOK_TILE, width), lambda i: (i, 0)),
                   pl.BlockSpec((width_t, TOK_TILE), lambda i: (0, i))),
        compiler_params=pltpu.CompilerParams(dimension_semantics=("parallel",),
                                             vmem_limit_bytes=VMEM_LIMIT),
        name="norm_in_proj",
    )(x2d, g.reshape(1, D_MODEL), w_kp, w_qvt)


def _attn_kernel(far_ref, qt_ref, k_ref, vt_ref, bias_ref, lq1_ref, lk1_ref, lq2_ref, lk2_ref, sg_ref,
                 o_ref, qs_ref, vte_ref, s0_ref, s1_ref, mc0_ref, mc1_ref, m_ref, acc_ref,
                 *, tile, n_kv, lam_init):
    h = pl.program_id(1)
    qi = pl.program_id(2)
    n_far = n_kv - NEAR_TILES

    @pl.when(qi == 0)
    def _():
        vte_ref[:V_DIM, :] = vt_ref[...]
        vte_ref[V_DIM:, :] = jnp.ones((V_EXT - V_DIM, vte_ref.shape[1]), vte_ref.dtype)

    qt = qt_ref[...]
    row = lax.broadcasted_iota(jnp.int32, qt.shape, 0)
    zero = jnp.zeros_like(qt)
    qs_ref[:, :tile] = jnp.where(row < QK_DIM, qt, zero)
    qs_ref[:, tile:] = jnp.where(row >= QK_DIM, qt, zero)

    m_ref[...] = jnp.full(m_ref.shape, -jnp.inf, jnp.float32)
    acc_ref[...] = jnp.zeros(acc_ref.shape, jnp.float32)
    chunks = [slice(c * ATT_CHUNK, (c + 1) * ATT_CHUNK) for c in range(2 * tile // ATT_CHUNK)]
    w0 = jnp.clip(qi - 1, 0, n_kv - NEAR_TILES)
    c_below = far_ref[2 * h] * LOG2E
    c_above = far_ref[2 * h + 1] * LOG2E

    def far_tile(f):
        return jnp.where(f < w0, f, f + NEAR_TILES)

    def far_const(ki):
        return jnp.where(ki < qi, c_below, c_above)

    def logits(ki, cols):
        k = k_ref[pl.ds(pl.multiple_of(ki * tile, tile), tile), :]
        return jnp.dot(k, qs_ref[:, cols], preferred_element_type=jnp.float32)

    def scores_near(ki, cols, s_ref, mc_ref):
        d = jnp.clip(ki - qi, -2, 2) + 2
        q0 = cols.start % tile
        s = logits(ki, cols) + bias_ref[0, d, :, q0:q0 + ATT_CHUNK]
        s_ref[:, cols] = s
        mc_ref[:, cols] = jnp.max(s, axis=0, keepdims=True)

    def scores_far(ki, cols, s_ref, mc_ref):
        s = logits(ki, cols)
        s_ref[:, cols] = s
        mc_ref[:, cols] = jnp.max(s, axis=0, keepdims=True) + far_const(ki)

    def update(ki, cols, s_ref, mc_ref, const):
        vt = vte_ref[:, pl.ds(pl.multiple_of(ki * tile, tile), tile)]
        m_prev = m_ref[:, cols]
        m_new = jnp.maximum(m_prev, mc_ref[:, cols])
        alpha = jnp.exp2(m_prev - m_new)
        p = jnp.exp2(s_ref[:, cols] - (m_new - const))
        acc_ref[:, cols] = alpha * acc_ref[:, cols] + jnp.dot(vt, p.astype(jnp.bfloat16),
                                                              preferred_element_type=jnp.float32)
        m_ref[:, cols] = m_new

    bufs = ((s0_ref, mc0_ref), (s1_ref, mc1_ref))
    for cols in chunks:
        scores_near(w0, cols, *bufs[0])
    for i in range(NEAR_TILES):
        for cols in chunks:
            if i + 1 < NEAR_TILES:
                scores_near(w0 + i + 1, cols, *bufs[(i + 1) % 2])
            elif n_far > 0:
                scores_far(far_tile(0), cols, *bufs[(i + 1) % 2])
            update(w0 + i, cols, *bufs[i % 2], 0.0)

    def pair(j, look_ahead=True):
        ka, kb, kc = far_tile(2 * j), far_tile(2 * j + 1), far_tile(2 * j + 2)
        for cols in chunks:
            scores_far(kb, cols, *bufs[(NEAR_TILES + 1) % 2])
            update(ka, cols, *bufs[NEAR_TILES % 2], far_const(ka))
        for cols in chunks:
            if look_ahead:
                scores_far(kc, cols, *bufs[NEAR_TILES % 2])
            update(kb, cols, *bufs[(NEAR_TILES + 1) % 2], far_const(kb))

    if n_far > 0:
        lax.fori_loop(0, n_far // 2 - 1, lambda j, c: (pair(j), c)[1], 0)
        pair(n_far // 2 - 1, look_ahead=False)

    acc = acc_ref[...]
    o = acc[:V_DIM] / acc[V_DIM:V_DIM + 1]
    lam = (jnp.exp(jnp.sum(lq1_ref[...] * lk1_ref[...], axis=1, keepdims=True))
           - jnp.exp(jnp.sum(lq2_ref[...] * lk2_ref[...], axis=1, keepdims=True)) + lam_init)
    o = o[:, :tile] - lam * o[:, tile:]
    o = o * lax.rsqrt(jnp.mean(o * o, axis=0, keepdims=True) + EPS) * (sg_ref[...] * (1.0 - lam_init))
    o_ref[...] = o.T.astype(o_ref.dtype)


def _attention(kp, qvt, bias, far, lq1, lk1, lq2, lk2, sub_g, *, batch, seq, lam_init):
    t = ATT_TILE
    nq = seq // t
    assert nq >= NEAR_TILES and (nq - NEAR_TILES) % 2 == 0
    vrow = QK_WIDTH // LANES
    vec = lambda width: pl.BlockSpec((1, width), lambda b, h, qi: (0, 0))
    return pl.pallas_call(
        functools.partial(_attn_kernel, tile=t, n_kv=nq, lam_init=lam_init),
        out_shape=jax.ShapeDtypeStruct((batch * seq, ATT_WIDTH), jnp.bfloat16),
        grid=(batch, N_HEADS, nq),
        in_specs=[pl.BlockSpec(memory_space=pltpu.SMEM),
                  pl.BlockSpec((LANES, t), lambda b, h, qi: (h, b * nq + qi)),
                  pl.BlockSpec((seq, LANES), lambda b, h, qi: (b, h)),
                  pl.BlockSpec((V_DIM, seq), lambda b, h, qi: (vrow + h, b)),
                  pl.BlockSpec((1, BIAS_TILES, t, t), lambda b, h, qi: (h, 0, 0, 0)),
                  vec(QK_DIM), vec(QK_DIM), vec(QK_DIM), vec(QK_DIM),
                  pl.BlockSpec((V_DIM, 1), lambda b, h, qi: (0, 0))],
        out_specs=pl.BlockSpec((t, V_DIM), lambda b, h, qi: (b * nq + qi, h)),
        scratch_shapes=[pltpu.VMEM((LANES, 2 * t), jnp.bfloat16),
                        pltpu.VMEM((V_EXT, seq), jnp.bfloat16),
                        pltpu.VMEM((t, 2 * t), jnp.float32),
                        pltpu.VMEM((t, 2 * t), jnp.float32),
                        pltpu.VMEM((1, 2 * t), jnp.float32),
                        pltpu.VMEM((1, 2 * t), jnp.float32),
                        pltpu.VMEM((1, 2 * t), jnp.float32),
                        pltpu.VMEM((V_EXT, 2 * t), jnp.float32)],
        compiler_params=pltpu.CompilerParams(dimension_semantics=("arbitrary", "arbitrary", "arbitrary"),
                                             vmem_limit_bytes=VMEM_LIMIT),
        name="diff_attention",
    )(far, qvt, kp, qvt, bias, lq1.reshape(1, -1), lk1.reshape(1, -1), lq2.reshape(1, -1),
      lk2.reshape(1, -1), sub_g.reshape(-1, 1))


def _mix_route_kernel(att_ref, p_ref, prev_ref, next_ref, x_ref, wpool_ref, pscale_ref, wout_ref, g2_ref,
                      wr_ref, br_ref, tri_ref,
                      x1_ref, hflat_ref, topi_ref, rank_ref, gate_ref, base_ref, cnt_out_ref, cnt_ref,
                      *, tile, seq):
    i = pl.program_id(0)
    tiles_per_seq = seq // tile
    si = i % tiles_per_seq

    @pl.when(i == 0)
    def _():
        cnt_ref[...] = jnp.zeros(cnt_ref.shape, jnp.float32)

    pc = p_ref[...].astype(jnp.float32)
    pp = jnp.where(si == 0, 0.0, prev_ref[...].astype(jnp.float32))
    pn = jnp.where(si == tiles_per_seq - 1, 0.0, next_ref[...].astype(jnp.float32))
    pe = jnp.concatenate([pp, pc, pn], axis=0)
    n_ext = tile + 2 * HALO
    pos = si * tile + lax.broadcasted_iota(jnp.int32, (tile, 1), 0)
    pooled = []
    for g, win in enumerate(POOL_WINDOWS):
        xg = pe[:, g * POOL_GROUP:(g + 1) * POOL_GROUP]
        w = xg + pltpu.roll(xg, 1, 0)
        half = 1
        while 2 * half < win:
            w = pltpu.roll(w, half, 0) + pltpu.roll(w, n_ext - half, 0)
            half *= 2
        lo = jnp.maximum(pos - win // 2, 0)
        hi = jnp.minimum(pos - win // 2 + win - 1, seq - 1)
        mean = w[HALO:HALO + tile] / (hi - lo + 1).astype(jnp.float32)
        dg = (mean - pc[:, g * POOL_GROUP:(g + 1) * POOL_GROUP]).astype(jnp.bfloat16)
        pooled.append(jnp.dot(dg, wpool_ref[g], preferred_element_type=jnp.float32))
    pool = jnp.concatenate(pooled, axis=1) * pscale_ref[...]

    mix = jnp.concatenate([att_ref[...], pool.astype(jnp.bfloat16)], axis=1)
    x1 = x_ref[...] + jnp.dot(mix, wout_ref[...], preferred_element_type=jnp.float32)
    x1_ref[...] = x1

    h2 = _rms(x1, g2_ref[...])
    for c in range(ROW_CHUNKS):
        hflat_ref[pl.ds(c, tile, stride=ROW_CHUNKS), :] = h2[:, c * LANES:(c + 1) * LANES]

    logits = lax.dot_general(wr_ref[...], h2.astype(jnp.bfloat16), _NT,
                             preferred_element_type=jnp.float32) + br_ref[...]
    eio = lax.broadcasted_iota(jnp.int32, logits.shape, 0)
    work = logits
    sel = jnp.zeros(logits.shape, jnp.float32)
    top_v, top_i, hot = [], [], []
    for _ in range(TOP_K):
        mx = jnp.max(work, axis=0, keepdims=True)
        idx = jnp.min(jnp.where(work == mx, eio, N_EXPERTS), axis=0, keepdims=True)
        oh = eio == idx
        top_v.append(mx)
        top_i.append(idx)
        hot.append(oh)
        work = jnp.where(oh, -jnp.inf, work)
        sel = sel + oh.astype(jnp.float32)
    ex = [jnp.exp(v - top_v[0]) for v in top_v]
    den = ex[0] + ex[1] + ex[2] + ex[3]
    ahead = jnp.dot(sel.astype(jnp.bfloat16), tri_ref[...], preferred_element_type=jnp.float32) + cnt_ref[...]
    for j in range(TOP_K):
        topi_ref[j:j + 1, :] = top_i[j]
        gate_ref[j:j + 1, :] = ex[j] / den
        rank_ref[j:j + 1, :] = jnp.sum(jnp.where(hot[j], ahead, 0.0), axis=0, keepdims=True).astype(jnp.int32)
    base_ref[0] = jnp.broadcast_to(cnt_ref[...], base_ref.shape[1:])
    cnt_ref[...] = cnt_ref[...] + jnp.sum(sel, axis=1, keepdims=True)
    cnt_out_ref[...] = jnp.broadcast_to(cnt_ref[...], cnt_out_ref.shape)


def _mix_route(att, proj, x2d, wpool, pscale, wout, g2, wr_t, br, tri, *, seq):
    n = x2d.shape[0]
    t = TOK_TILE
    nt = n // t
    pcol = QK_WIDTH // POOL_WIDTH
    hb = t // HALO
    last_halo = n // HALO - 1
    const = lambda shape: pl.BlockSpec(shape, lambda i: (0,) * len(shape))
    out_shape = (jax.ShapeDtypeStruct((n, D_MODEL), jnp.float32),
                 jax.ShapeDtypeStruct((n * ROW_CHUNKS, LANES), jnp.float32),
                 jax.ShapeDtypeStruct((TOP_K, n), jnp.int32),
                 jax.ShapeDtypeStruct((TOP_K, n), jnp.int32),
                 jax.ShapeDtypeStruct((TOP_K, n), jnp.float32),
                 jax.ShapeDtypeStruct((nt, N_EXPERTS, LANES), jnp.float32),
                 jax.ShapeDtypeStruct((N_EXPERTS, LANES), jnp.float32))
    return pl.pallas_call(
        functools.partial(_mix_route_kernel, tile=t, seq=seq),
        out_shape=out_shape,
        grid=(nt,),
        in_specs=[pl.BlockSpec((t, ATT_WIDTH), lambda i: (i, 0)),
                  pl.BlockSpec((t, POOL_WIDTH), lambda i: (i, pcol)),
                  pl.BlockSpec((HALO, POOL_WIDTH), lambda i: (jnp.maximum(i * hb - 1, 0), pcol)),
                  pl.BlockSpec((HALO, POOL_WIDTH), lambda i: (jnp.minimum((i + 1) * hb, last_halo), pcol)),
                  pl.BlockSpec((t, D_MODEL), lambda i: (i, 0)),
                  const((len(POOL_WINDOWS), POOL_GROUP, POOL_GROUP)),
                  const((1, POOL_WIDTH)),
                  const((D_MODEL, D_MODEL)),
                  const((1, D_MODEL)),
                  const((N_EXPERTS, D_MODEL)),
                  const((N_EXPERTS, 1)),
                  const((t, t))],
        out_specs=(pl.BlockSpec((t, D_MODEL), lambda i: (i, 0)),
                   pl.BlockSpec((t * ROW_CHUNKS, LANES), lambda i: (i, 0)),
                   pl.BlockSpec((TOP_K, t), lambda i: (0, i)),
                   pl.BlockSpec((TOP_K, t), lambda i: (0, i)),
                   pl.BlockSpec((TOP_K, t), lambda i: (0, i)),
                   pl.BlockSpec((1, N_EXPERTS, LANES), lambda i: (i, 0, 0)),
                   pl.BlockSpec((N_EXPERTS, LANES), lambda i: (0, 0))),
        scratch_shapes=[pltpu.VMEM((N_EXPERTS, 1), jnp.float32)],
        compiler_params=pltpu.CompilerParams(dimension_semantics=("arbitrary",),
                                             vmem_limit_bytes=VMEM_LIMIT),
        name="mix_out_route",
    )(att, proj, proj, proj, x2d, wpool, pscale.reshape(1, -1), wout, g2.reshape(1, -1), wr_t,
      br.reshape(-1, 1), tri)


def _row(ref, r):
    return ref.at[pl.ds(pl.multiple_of(r * ROW_CHUNKS, ROW_CHUNKS), ROW_CHUNKS), :]


def _dispatch_kernel(pad_lo_ref, pad_n_ref, dest_hbm, h_ref, xs_out, idx_ref, zero_ref, idx_sem, row_sem, pad_sem,
                     *, tile):
    i = pl.program_id(0)
    per = tile * TOP_K

    @pl.when(i == 0)
    def _():
        zero_ref[...] = jnp.zeros(zero_ref.shape, zero_ref.dtype)

        def zero_rows(first_row, n_rows, act):
            dst = xs_out.at[pl.ds(pl.multiple_of(first_row * ROW_CHUNKS, ROW_CHUNKS), n_rows * ROW_CHUNKS), :]
            act(pltpu.make_async_copy(zero_ref.at[pl.ds(0, n_rows * ROW_CHUNKS), :], dst, pad_sem))

        def each_piece(act):
            def per_expert(e, c):
                row, left = pad_lo_ref[e], pad_n_ref[e]
                piece = EXPERT_ROWS // 2
                while piece >= 1:
                    take = left >= piece

                    @pl.when(take)
                    def _(row=row, piece=piece):
                        zero_rows(row, piece, act)

                    row = row + jnp.where(take, piece, 0)
                    left = left - jnp.where(take, piece, 0)
                    piece //= 2
                return c

            lax.fori_loop(0, N_EXPERTS, per_expert, 0)
            lax.fori_loop(0, pad_n_ref[N_EXPERTS],
                          lambda k, c: (zero_rows(pad_lo_ref[N_EXPERTS] + k * EXPERT_ROWS, EXPERT_ROWS, act), c)[1], 0)

        each_piece(lambda cp: cp.start())
        each_piece(lambda cp: cp.wait())

    idx_copy = pltpu.make_async_copy(dest_hbm.at[pl.ds(pl.multiple_of(i * per, per), per)], idx_ref, idx_sem)
    idx_copy.start()
    idx_copy.wait()

    def body(t, c):
        src = _row(h_ref, t)
        for j in range(TOP_K):
            pltpu.make_async_copy(src, _row(xs_out, idx_ref[t * TOP_K + j]), row_sem).start(priority=j % 2)
        return c

    lax.fori_loop(0, tile, body, 0)
    for _ in range(TOP_K):
        pltpu.make_async_copy(h_ref, xs_out.at[pl.ds(0, tile * ROW_CHUNKS), :], row_sem).wait()


def _dispatch(pad_lo, pad_n, dest_flat, hflat, n_rows):
    t = DISPATCH_TILE
    per = t * TOP_K
    nt = dest_flat.shape[0] // per
    return pl.pallas_call(
        functools.partial(_dispatch_kernel, tile=t),
        out_shape=jax.ShapeDtypeStruct((n_rows * ROW_CHUNKS, LANES), jnp.float32),
        grid_spec=pltpu.PrefetchScalarGridSpec(
            num_scalar_prefetch=2,
            grid=(nt,),
            in_specs=[pl.BlockSpec(memory_space=pl.ANY),
                      pl.BlockSpec((t * ROW_CHUNKS, LANES), lambda i, lo, cnt: (i, 0))],
            out_specs=pl.BlockSpec(memory_space=pl.ANY),
            scratch_shapes=[pltpu.SMEM((per,), jnp.int32),
                            pltpu.VMEM((EXPERT_ROWS * ROW_CHUNKS, LANES), jnp.float32),
                            pltpu.SemaphoreType.DMA,
                            pltpu.SemaphoreType.DMA,
                            pltpu.SemaphoreType.DMA]),
        compiler_params=pltpu.CompilerParams(dimension_semantics=("arbitrary",),
                                             vmem_limit_bytes=VMEM_LIMIT),
        name="moe_dispatch",
    )(pad_lo, pad_n, dest_flat, hflat)


def _expert_kernel(be_ref, used_ref, x_ref, wg_ref, bg_ref, wu_ref, bu_ref, wd_ref, bd_ref, y_ref,
                   wg_bf, wu_bf, wd_bf, *, rows):
    b = pl.program_id(0)
    live = b < used_ref[0]

    @pl.when(jnp.logical_not(live))
    def _():
        y_ref[...] = jnp.zeros(y_ref.shape, y_ref.dtype)

    @pl.when(jnp.logical_and(live, jnp.logical_or(b == 0, be_ref[b] != be_ref[jnp.maximum(b - 1, 0)])))
    def _():
        wg_bf[...] = wg_ref[0, 0].astype(jnp.bfloat16)
        wu_bf[...] = wu_ref[0, 0].astype(jnp.bfloat16)
        wd_bf[...] = wd_ref[0, 0].astype(jnp.bfloat16)

    @pl.when(live)
    def _():
        half_rows = rows // 2
        half = D_MODEL // 2
        halves = (slice(0, half), slice(half, D_MODEL))
        for r0 in (0, half_rows):
            first = r0 * ROW_CHUNKS
            x = jnp.concatenate([x_ref[pl.ds(first + c, half_rows, stride=ROW_CHUNKS), :]
                                 for c in range(ROW_CHUNKS)], axis=1).astype(jnp.bfloat16)
            gu = [(jnp.dot(x, wg_bf[:, cols], preferred_element_type=jnp.float32) + bg_ref[0, 0, :, cols],
                   jnp.dot(x, wu_bf[:, cols], preferred_element_type=jnp.float32) + bu_ref[0, 0, :, cols])
                  for cols in halves]
            y = bd_ref[0, 0]
            for (g, u), cols in zip(gu, halves):
                g = jnp.minimum(g, SWIGLU_LIMIT)
                u = jnp.clip(u, -SWIGLU_LIMIT, SWIGLU_LIMIT)
                act = (u + 1.0) * (g * (1.0 / (1.0 + jnp.exp(-SWIGLU_ALPHA * g))))
                y = y + jnp.dot(act.astype(jnp.bfloat16), wd_bf[cols, :], preferred_element_type=jnp.float32)
            for c in range(ROW_CHUNKS):
                y_ref[pl.ds(first + c, half_rows, stride=ROW_CHUNKS), :] = y[:, c * LANES:(c + 1) * LANES]


def _experts(block_expert, n_used, xs, wg, bg, wu, bu, wd, bd, *, layer):
    nb = block_expert.shape[0]
    rows = EXPERT_ROWS
    depth = wg.shape[0]
    blk = lambda b, be, used: (jnp.minimum(b, used[0] - 1), 0)
    wsel = lambda b, be, used: (layer, be[b], 0, 0)
    wspec = pl.BlockSpec((1, 1, D_MODEL, D_MODEL), wsel)
    bspec = pl.BlockSpec((1, 1, 1, D_MODEL), wsel)
    bias4 = lambda a: a.reshape(depth, N_EXPERTS, 1, D_MODEL)
    return pl.pallas_call(
        functools.partial(_expert_kernel, rows=rows),
        out_shape=jax.ShapeDtypeStruct(xs.shape, jnp.float32),
        grid_spec=pltpu.PrefetchScalarGridSpec(
            num_scalar_prefetch=2,
            grid=(nb,),
            in_specs=[pl.BlockSpec((rows * ROW_CHUNKS, LANES), blk),
                      wspec, bspec, wspec, bspec, wspec, bspec],
            out_specs=pl.BlockSpec((rows * ROW_CHUNKS, LANES), lambda b, be, used: (b, 0)),
            scratch_shapes=[pltpu.VMEM((D_MODEL, D_MODEL), jnp.bfloat16)] * 3),
        compiler_params=pltpu.CompilerParams(dimension_semantics=("arbitrary",),
                                             vmem_limit_bytes=VMEM_LIMIT),
        name="moe_experts",
    )(block_expert, n_used, xs, wg, bias4(bg), wu, bias4(bu), wd, bias4(bd))


def _combine_kernel(cstart_ref, clen_ref, pchunk_ref, gbound_ref, ys_hbm, x_ref, dest_ref, gate_ref, gf_ref,
                    o_ref, zbuf_ref, acc_ref, destb_ref, gateb_ref, sem, *, tile, final):
    i = pl.program_id(0)
    n_tiles = pl.num_programs(0)
    slab_rows = SLAB_CHUNKS * COMBINE_CHUNK
    chunk_flat = COMBINE_CHUNK * ROW_CHUNKS

    def pair_copies(first_chunk, ring):
        return [pltpu.make_async_copy(
            ys_hbm.at[pl.ds(pl.multiple_of(cstart_ref[first_chunk + q] * ROW_CHUNKS, ROW_CHUNKS), chunk_flat), :],
            zbuf_ref.at[ring, q // SLAB_CHUNKS, pl.ds((q % SLAB_CHUNKS) * chunk_flat, chunk_flat), :],
            sem.at[ring]) for q in range(2 * SLAB_CHUNKS)]

    g0 = gbound_ref[i]
    n_pairs = gbound_ref[i + 1] - g0
    g_total = gbound_ref[n_tiles]

    @pl.when(i == 0)
    def _():
        for cp in pair_copies(pchunk_ref[0], 0):
            cp.start()

        @pl.when(g_total > 1)
        def _():
            for cp in pair_copies(pchunk_ref[1], 1):
                cp.start()

    acc_ref[...] = x_ref[...]
    for j in range(TOP_K):
        destb_ref[j] = jnp.broadcast_to(dest_ref[:, j:j + 1], (tile, LANES))
        gateb_ref[j] = jnp.broadcast_to(gate_ref[:, j:j + 1], (tile, LANES))
    col = lax.broadcasted_iota(jnp.int32, (1, slab_rows), 1)
    col_chunk = col // COMBINE_CHUNK
    col_row = col % COMBINE_CHUNK

    def gate_matrix(first):
        rows = jnp.full((1, slab_rows), -1, jnp.int32)
        for q in range(SLAB_CHUNKS):
            ok = jnp.logical_and(col_chunk == q, col_row < clen_ref[first + q])
            rows = jnp.where(ok, cstart_ref[first + q] + col_row, rows)
        g_cols = []
        for c0 in range(0, slab_rows, LANES):
            rows_c = rows[:, c0:c0 + LANES]
            g_c = jnp.zeros((tile, LANES), jnp.float32)
            for j in range(TOP_K):
                g_c = jnp.where(destb_ref[j] == rows_c, gateb_ref[j], g_c)
            g_cols.append(g_c.astype(jnp.bfloat16))
        return jnp.concatenate(g_cols, axis=1)

    def pair(p, carry):
        g = g0 + p
        ring = g % 2
        first = pchunk_ref[g]
        for cp in pair_copies(first, ring):
            cp.wait()
        ys_pair = [jnp.concatenate([zbuf_ref[ring, slot, pl.ds(c, slab_rows, stride=ROW_CHUNKS), :]
                                    for c in range(ROW_CHUNKS)], axis=1).astype(jnp.bfloat16)
                   for slot in range(2)]

        @pl.when(g + 2 < g_total)
        def _():
            for cp in pair_copies(pchunk_ref[g + 2], ring):
                cp.start()

        for slot in range(2):
            acc_ref[...] += jnp.dot(gate_matrix(first + slot * SLAB_CHUNKS), ys_pair[slot],
                                    preferred_element_type=jnp.float32)
        return carry

    lax.fori_loop(0, n_pairs, pair, 0)
    x2 = acc_ref[...]
    o_ref[...] = _rms(x2, gf_ref[...]) if final else x2


def _combine(chunk_start, chunk_len, pair_chunk, pair_bound, ys, x1, dest_tok, gates_tok, g_final, *, final):
    t = TOK_TILE
    n = x1.shape[0]
    tok = lambda width: pl.BlockSpec((t, width), lambda i, *_: (i, 0))
    return pl.pallas_call(
        functools.partial(_combine_kernel, tile=t, final=final),
        out_shape=jax.ShapeDtypeStruct((n, D_MODEL), jnp.float32),
        grid_spec=pltpu.PrefetchScalarGridSpec(
            num_scalar_prefetch=4,
            grid=(n // t,),
            in_specs=[pl.BlockSpec(memory_space=pl.ANY),
                      tok(D_MODEL), tok(TOP_K), tok(TOP_K),
                      pl.BlockSpec((1, D_MODEL), lambda i, *_: (0, 0))],
            out_specs=tok(D_MODEL),
            scratch_shapes=[pltpu.VMEM((2, 2, SLAB_CHUNKS * COMBINE_CHUNK * ROW_CHUNKS, LANES), jnp.float32),
                            pltpu.VMEM((t, D_MODEL), jnp.float32),
                            pltpu.VMEM((TOP_K, t, LANES), jnp.int32),
                            pltpu.VMEM((TOP_K, t, LANES), jnp.float32),
                            pltpu.SemaphoreType.DMA((2,))]),
        compiler_params=pltpu.CompilerParams(dimension_semantics=("arbitrary",),
                                             vmem_limit_bytes=VMEM_LIMIT),
        name="moe_combine",
    )(chunk_start, chunk_len, pair_chunk, pair_bound, ys, x1, dest_tok, gates_tok, g_final.reshape(1, -1))


def _combine_chunks(base, cnt, seg_start):
    n_run = jnp.concatenate([base[1:], cnt[None]], axis=0) - base
    run_start = seg_start[None, :] + base
    n_chunks = (n_run + COMBINE_CHUNK - 1) // COMBINE_CHUNK
    cum = jnp.cumsum(n_chunks, axis=1)
    total = cum[:, -1]
    slots = jnp.arange(MAX_CHUNKS, dtype=jnp.int32)
    expert = jnp.minimum(jnp.sum(cum[:, None, :] <= slots[None, :, None], axis=2), N_EXPERTS - 1)
    hot = expert[..., None] == jnp.arange(N_EXPERTS, dtype=jnp.int32)
    pick = lambda a: jnp.sum(jnp.where(hot, a[:, None, :], 0), axis=2)
    k = slots[None, :] - (pick(cum) - pick(n_chunks))
    valid = slots[None, :] < total[:, None]
    start = jnp.where(valid, pick(run_start) + COMBINE_CHUNK * k, 0)
    length = jnp.where(valid, jnp.clip(pick(n_run) - COMBINE_CHUNK * k, 0, COMBINE_CHUNK), 0)
    pair_size = 2 * SLAB_CHUNKS
    n_tiles = base.shape[0]
    n_pairs = (total + pair_size - 1) // pair_size
    pair_end = jnp.cumsum(n_pairs)
    pair_bound = jnp.concatenate([jnp.zeros((1,), pair_end.dtype), pair_end])
    g = jnp.arange(n_tiles * (MAX_CHUNKS // pair_size), dtype=jnp.int32)
    tile_of = jnp.minimum(jnp.sum(pair_end[None, :] <= g[:, None], axis=1), n_tiles - 1)
    tile_first = jnp.sum(jnp.where(tile_of[:, None] == jnp.arange(n_tiles), (pair_end - n_pairs)[None, :], 0), axis=1)
    pair_chunk = tile_of * MAX_CHUNKS + pair_size * (g - tile_first)
    i32 = lambda a: a.reshape(-1).astype(jnp.int32)
    return i32(start), i32(length), i32(pair_chunk), i32(pair_bound)


def kernel(x, rel_table, norm1, w_in, lambda_q1, lambda_k1, lambda_q2, lambda_k2, subln_g, w_pool, pool_scale,
           w_out, norm2, w_router, b_router, w_gate, b_gate, w_up, b_up, w_down, b_down, final_norm):
    batch, seq, d = x.shape
    depth = norm1.shape[0]
    n = batch * seq
    assert d == D_MODEL and seq % ATT_TILE == 0 and seq % TOK_TILE == 0
    assert n % DISPATCH_TILE == 0
    bf = jnp.bfloat16

    w_q = w_in[:, :, :QK_WIDTH] * (QK_DIM ** -0.5 * LOG2E)
    w_k = w_in[:, :, QK_WIDTH:2 * QK_WIDTH]
    w_v = w_in[:, :, 2 * QK_WIDTH:2 * QK_WIDTH + ATT_WIDTH]
    w_p = w_in[:, :, 2 * QK_WIDTH + ATT_WIDTH:]
    w_kp = jnp.concatenate([w_k, w_p], axis=2).astype(bf)
    w_qvt = jnp.swapaxes(jnp.concatenate([w_q, w_v], axis=2), 1, 2).astype(bf)
    w_out_b = w_out.astype(bf)
    w_pool_b = w_pool.astype(bf)
    wr_t = jnp.swapaxes(w_router, 1, 2).astype(bf)
    tri = (lax.broadcasted_iota(jnp.int32, (TOK_TILE, TOK_TILE), 0)
           < lax.broadcasted_iota(jnp.int32, (TOK_TILE, TOK_TILE), 1)).astype(bf)

    bias = _bias_tiles(rel_table, ATT_TILE)
    far = jnp.stack([rel_table[N_BUCKETS // 2 - 1], rel_table[N_BUCKETS - 1]], axis=1).reshape(-1)

    n_assign = n * TOP_K
    n_blocks = -(-(n_assign + N_EXPERTS * (EXPERT_ROWS - 1)) // EXPERT_ROWS) + 1
    n_rows = n_blocks * EXPERT_ROWS
    block_start = jnp.arange(n_blocks, dtype=jnp.int32) * EXPERT_ROWS

    x2d = x.reshape(n, d)
    for l in range(depth):
        lam_init = 0.8 - 0.6 * math.exp(-0.3 * l)
        kp, qvt = _norm_proj(x2d, norm1[l], w_kp[l], w_qvt[l])
        att = _attention(kp, qvt, bias, far, lambda_q1[l], lambda_k1[l], lambda_q2[l], lambda_k2[l], subln_g[l],
                         batch=batch, seq=seq, lam_init=lam_init)
        x1, hflat, topi, rank, gates, base, counts = _mix_route(
            att, kp, x2d, w_pool_b[l], pool_scale[l], w_out_b[l], norm2[l], wr_t[l], b_router[l], tri, seq=seq)

        cnt = counts[:, 0].astype(jnp.int32)
        padded = ((cnt + EXPERT_ROWS - 1) // EXPERT_ROWS) * EXPERT_ROWS
        pad_end = jnp.cumsum(padded)
        experts = jnp.arange(N_EXPERTS, dtype=jnp.int32)
        seg_start = pad_end - padded
        dest_tok = (jnp.sum(jnp.where(topi[..., None] == experts, seg_start, 0), axis=-1) + rank).T
        dest_flat = dest_tok.reshape(-1)
        block_expert = jnp.minimum(jnp.sum(pad_end[None, :] <= block_start[:, None], axis=1),
                                   N_EXPERTS - 1).astype(jnp.int32)
        n_used = (pad_end[-1:] // EXPERT_ROWS).astype(jnp.int32)

        pad_lo = jnp.concatenate([pad_end - padded + cnt, pad_end[-1:]])
        pad_n = jnp.concatenate([padded - cnt, (n_rows - pad_end[-1:]) // EXPERT_ROWS])
        xs = _dispatch(pad_lo, pad_n, dest_flat, hflat, n_rows)
        ys = _experts(block_expert, n_used, xs, w_gate, b_gate, w_up, b_up, w_down, b_down, layer=l)
        chunk_lists = _combine_chunks(base[:, :, 0].astype(jnp.int32), cnt, seg_start)
        x2d = _combine(*chunk_lists, ys, x1, dest_tok, gates.T, final_norm,
                       final=(l == depth - 1))
    return x2d.reshape(batch, seq, d)
```

```python
import functools
import math

import jax
import jax.numpy as jnp
from jax import lax
from jax.experimental import pallas as pl
from jax.experimental.pallas import tpu as pltpu

D_MODEL = 1024
N_HEADS = 4
QK_DIM = 64
V_DIM = 128
QK_WIDTH = N_HEADS * 2 * QK_DIM
ATT_WIDTH = N_HEADS * V_DIM
POOL_WINDOWS = (2, 4, 8, 16)
POOL_GROUP = 128
POOL_WIDTH = 512
N_BUCKETS = 32
MAX_DISTANCE = 128
N_EXPERTS = 32
TOP_K = 4
SWIGLU_LIMIT = 7.0
SWIGLU_ALPHA = 1.702
EPS = 1e-5

LANES = 128
BF16_SUBLANES = 16
MXU_DEPTH = 256
ROW_CHUNKS = D_MODEL // LANES
VMEM_LIMIT = 56 * 1024 * 1024

TOK_TILE = 512
ATT_TILE = 512
ATT_CHUNK = 256
BIAS_TILES = 5
NEAR_TILES = 4
V_EXT = V_DIM + BF16_SUBLANES
HALO = BF16_SUBLANES
EXPERT_ROWS = 512
DISPATCH_TILE = 2048
COMBINE_CHUNK = 32
SLAB_CHUNKS = MXU_DEPTH // COMBINE_CHUNK
MAX_CHUNKS = TOK_TILE * TOP_K // COMBINE_CHUNK + N_EXPERTS

_NT = (((1,), (1,)), ((), ()))
LOG2E = math.log2(math.e)


def _rms(x, g):
    return x * lax.rsqrt(jnp.mean(x * x, axis=-1, keepdims=True) + EPS) * g


def _bias_kernel(tab_ref, out_ref, *, tile):
    h = pl.program_id(0)
    d = pl.program_id(1)
    key = lax.broadcasted_iota(jnp.int32, (tile, tile), 0)
    qry = lax.broadcasted_iota(jnp.int32, (tile, tile), 1)
    rel = (d - BIAS_TILES // 2) * tile + key - qry
    nb = N_BUCKETS // 2
    max_exact = nb // 2
    ret = jnp.where(rel > 0, nb, 0)
    n = jnp.abs(rel)
    n_f = jnp.maximum(n, 1).astype(jnp.float32)
    large = max_exact + (jnp.log(n_f / max_exact) / math.log(MAX_DISTANCE / max_exact)
                         * (nb - max_exact)).astype(jnp.int32)
    large = jnp.minimum(large, nb - 1)
    bucket = ret + jnp.where(n < max_exact, n, large)
    acc = jnp.zeros((tile, tile), jnp.float32)
    for b in range(N_BUCKETS):
        acc = jnp.where(bucket == b, tab_ref[b * N_HEADS + h], acc)
    out_ref[0, 0] = acc * LOG2E


def _bias_tiles(rel_table, tile):
    nb, max_exact = N_BUCKETS // 2, N_BUCKETS // 4
    last_bucket_from = max_exact * (MAX_DISTANCE / max_exact) ** ((nb - 1 - max_exact) / (nb - max_exact))
    assert tile + 1 >= last_bucket_from
    return pl.pallas_call(
        functools.partial(_bias_kernel, tile=tile),
        out_shape=jax.ShapeDtypeStruct((N_HEADS, BIAS_TILES, tile, tile), jnp.float32),
        grid=(N_HEADS, BIAS_TILES),
        in_specs=[pl.BlockSpec(memory_space=pltpu.SMEM)],
        out_specs=pl.BlockSpec((1, 1, tile, tile), lambda h, d: (h, d, 0, 0)),
        name="rel_bias_tiles",
    )(rel_table.reshape(-1))


def _norm_proj_kernel(x_ref, g_ref, wkp_ref, wqvt_ref, kp_ref, qvt_ref):
    h = _rms(x_ref[...], g_ref[...]).astype(jnp.bfloat16)
    kp_ref[...] = jnp.dot(h, wkp_ref[...], preferred_element_type=jnp.float32).astype(kp_ref.dtype)
    qvt_ref[...] = lax.dot_general(wqvt_ref[...], h, _NT,
                                   preferred_element_type=jnp.float32).astype(qvt_ref.dtype)


def _norm_proj(x2d, g, w_kp, w_qvt):
    n = x2d.shape[0]
    width = QK_WIDTH + POOL_WIDTH
    width_t = QK_WIDTH + ATT_WIDTH
    return pl.pallas_call(
        _norm_proj_kernel,
        out_shape=(jax.ShapeDtypeStruct((n, width), jnp.bfloat16),
                   jax.ShapeDtypeStruct((width_t, n), jnp.bfloat16)),
        grid=(n // TOK_TILE,),
        in_specs=[pl.BlockSpec((TOK_TILE, D_MODEL), lambda i: (i, 0)),
                  pl.BlockSpec((1, D_MODEL), lambda i: (0, 0)),
                  pl.BlockSpec((D_MODEL, width), lambda i: (0, 0)),
                  pl.BlockSpec((width_t, D_MODEL), lambda i: (0, 0))],
        out_specs=(pl.BlockSpec((TOK_TILE, width), lambda i: (i, 0)),
                   pl.BlockSpec((width_t, TOK_TILE), lambda i: (0, i))),
        compiler_params=pltpu.CompilerParams(dimension_semantics=("parallel",),
                                             vmem_limit_bytes=VMEM_LIMIT),
        name="norm_in_proj",
    )(x2d, g.reshape(1, D_MODEL), w_kp, w_qvt)


def _attn_kernel(far_ref, qt_ref, k_ref, vt_ref, bias_ref, lq1_ref, lk1_ref, lq2_ref, lk2_ref, sg_ref,
                 o_ref, qs_ref, vte_ref, s0_ref, s1_ref, mc0_ref, mc1_ref, m_ref, acc_ref,
                 *, tile, n_kv, lam_init):
    h = pl.program_id(1)
    qi = pl.program_id(2)
    n_far = n_kv - NEAR_TILES

    @pl.when(qi == 0)
    def _():
        vte_ref[:V_DIM, :] = vt_ref[...]
        vte_ref[V_DIM:, :] = jnp.ones((V_EXT - V_DIM, vte_ref.shape[1]), vte_ref.dtype)

    qt = qt_ref[...]
    row = lax.broadcasted_iota(jnp.int32, qt.shape, 0)
    zero = jnp.zeros_like(qt)
    qs_ref[:, :tile] = jnp.where(row < QK_DIM, qt, zero)
    qs_ref[:, tile:] = jnp.where(row >= QK_DIM, qt, zero)

    m_ref[...] = jnp.full(m_ref.shape, -jnp.inf, jnp.float32)
    acc_ref[...] = jnp.zeros(acc_ref.shape, jnp.float32)
    chunks = [slice(c * ATT_CHUNK, (c + 1) * ATT_CHUNK) for c in range(2 * tile // ATT_CHUNK)]
    w0 = jnp.clip(qi - 1, 0, n_kv - NEAR_TILES)
    c_below = far_ref[2 * h] * LOG2E
    c_above = far_ref[2 * h + 1] * LOG2E

    def far_tile(f):
        return jnp.where(f < w0, f, f + NEAR_TILES)

    def far_const(ki):
        return jnp.where(ki < qi, c_below, c_above)

    def logits(ki, cols):
        k = k_ref[pl.ds(pl.multiple_of(ki * tile, tile), tile), :]
        return jnp.dot(k, qs_ref[:, cols], preferred_element_type=jnp.float32)

    def scores_near(ki, cols, s_ref, mc_ref):
        d = jnp.clip(ki - qi, -(BIAS_TILES // 2), BIAS_TILES // 2) + BIAS_TILES // 2
        q0 = cols.start % tile
        s = logits(ki, cols) + bias_ref[0, d, :, q0:q0 + ATT_CHUNK]
        s_ref[:, cols] = s
        mc_ref[:, cols] = jnp.max(s, axis=0, keepdims=True)

    def scores_far(ki, cols, s_ref, mc_ref):
        s = logits(ki, cols)
        s_ref[:, cols] = s
        mc_ref[:, cols] = jnp.max(s, axis=0, keepdims=True) + far_const(ki)

    def update(ki, cols, s_ref, mc_ref, const):
        vt = vte_ref[:, pl.ds(pl.multiple_of(ki * tile, tile), tile)]
        m_prev = m_ref[:, cols]
        m_new = jnp.maximum(m_prev, mc_ref[:, cols])
        alpha = jnp.exp2(m_prev - m_new)
        p = jnp.exp2(s_ref[:, cols] - (m_new - const))
        acc_ref[:, cols] = alpha * acc_ref[:, cols] + jnp.dot(vt, p.astype(jnp.bfloat16),
                                                              preferred_element_type=jnp.float32)
        m_ref[:, cols] = m_new

    bufs = ((s0_ref, mc0_ref), (s1_ref, mc1_ref))
    for cols in chunks:
        scores_near(w0, cols, *bufs[0])
    for i in range(NEAR_TILES):
        for cols in chunks:
            if i + 1 < NEAR_TILES:
                scores_near(w0 + i + 1, cols, *bufs[(i + 1) % 2])
            elif n_far > 0:
                scores_far(far_tile(0), cols, *bufs[(i + 1) % 2])
            update(w0 + i, cols, *bufs[i % 2], 0.0)

    def pair(j, look_ahead=True):
        ka, kb, kc = far_tile(2 * j), far_tile(2 * j + 1), far_tile(2 * j + 2)
        for cols in chunks:
            scores_far(kb, cols, *bufs[(NEAR_TILES + 1) % 2])
            update(ka, cols, *bufs[NEAR_TILES % 2], far_const(ka))
        for cols in chunks:
            if look_ahead:
                scores_far(kc, cols, *bufs[NEAR_TILES % 2])
            update(kb, cols, *bufs[(NEAR_TILES + 1) % 2], far_const(kb))

    if n_far > 0:
        lax.fori_loop(0, n_far // 2 - 1, lambda j, c: (pair(j), c)[1], 0)
        pair(n_far // 2 - 1, look_ahead=False)

    acc = acc_ref[...]
    o = acc[:V_DIM] / acc[V_DIM:V_DIM + 1]
    lam = (jnp.exp(jnp.sum(lq1_ref[...] * lk1_ref[...], axis=1, keepdims=True))
           - jnp.exp(jnp.sum(lq2_ref[...] * lk2_ref[...], axis=1, keepdims=True)) + lam_init)
    o = o[:, :tile] - lam * o[:, tile:]
    o = o * lax.rsqrt(jnp.mean(o * o, axis=0, keepdims=True) + EPS) * (sg_ref[...] * (1.0 - lam_init))
    o_ref[...] = o.T.astype(o_ref.dtype)


def _attention(kp, qvt, bias, far, lq1, lk1, lq2, lk2, sub_g, *, batch, seq, lam_init):
    t = ATT_TILE
    nq = seq // t
    assert nq >= NEAR_TILES and (nq - NEAR_TILES) % 2 == 0
    vrow = QK_WIDTH // LANES
    vec = lambda width: pl.BlockSpec((1, width), lambda b, h, qi: (0, 0))
    return pl.pallas_call(
        functools.partial(_attn_kernel, tile=t, n_kv=nq, lam_init=lam_init),
        out_shape=jax.ShapeDtypeStruct((batch * seq, ATT_WIDTH), jnp.bfloat16),
        grid=(batch, N_HEADS, nq),
        in_specs=[pl.BlockSpec(memory_space=pltpu.SMEM),
                  pl.BlockSpec((LANES, t), lambda b, h, qi: (h, b * nq + qi)),
                  pl.BlockSpec((seq, LANES), lambda b, h, qi: (b, h)),
                  pl.BlockSpec((V_DIM, seq), lambda b, h, qi: (vrow + h, b)),
                  pl.BlockSpec((1, BIAS_TILES, t, t), lambda b, h, qi: (h, 0, 0, 0)),
                  vec(QK_DIM), vec(QK_DIM), vec(QK_DIM), vec(QK_DIM),
                  pl.BlockSpec((V_DIM, 1), lambda b, h, qi: (0, 0))],
        out_specs=pl.BlockSpec((t, V_DIM), lambda b, h, qi: (b * nq + qi, h)),
        scratch_shapes=[pltpu.VMEM((LANES, 2 * t), jnp.bfloat16),
                        pltpu.VMEM((V_EXT, seq), jnp.bfloat16),
                        pltpu.VMEM((t, 2 * t), jnp.float32),
                        pltpu.VMEM((t, 2 * t), jnp.float32),
                        pltpu.VMEM((1, 2 * t), jnp.float32),
                        pltpu.VMEM((1, 2 * t), jnp.float32),
                        pltpu.VMEM((1, 2 * t), jnp.float32),
                        pltpu.VMEM((V_EXT, 2 * t), jnp.float32)],
        compiler_params=pltpu.CompilerParams(dimension_semantics=("arbitrary", "arbitrary", "arbitrary"),
                                             vmem_limit_bytes=VMEM_LIMIT),
        name="diff_attention",
    )(far, qvt, kp, qvt, bias, lq1.reshape(1, -1), lk1.reshape(1, -1), lq2.reshape(1, -1),
      lk2.reshape(1, -1), sub_g.reshape(-1, 1))


def _mix_route_kernel(att_ref, p_ref, prev_ref, next_ref, x_ref, wpool_ref, pscale_ref, wout_ref, g2_ref,
                      wr_ref, br_ref, tri_ref,
                      x1_ref, hflat_ref, topi_ref, rank_ref, gate_ref, base_ref, cnt_out_ref, cnt_ref,
                      *, tile, seq):
    i = pl.program_id(0)
    tiles_per_seq = seq // tile
    si = i % tiles_per_seq

    @pl.when(i == 0)
    def _():
        cnt_ref[...] = jnp.zeros(cnt_ref.shape, jnp.float32)

    pc = p_ref[...].astype(jnp.float32)
    pp = jnp.where(si == 0, 0.0, prev_ref[...].astype(jnp.float32))
    pn = jnp.where(si == tiles_per_seq - 1, 0.0, next_ref[...].astype(jnp.float32))
    pe = jnp.concatenate([pp, pc, pn], axis=0)
    n_ext = tile + 2 * HALO
    pos = si * tile + lax.broadcasted_iota(jnp.int32, (tile, 1), 0)
    pooled = []
    for g, win in enumerate(POOL_WINDOWS):
        xg = pe[:, g * POOL_GROUP:(g + 1) * POOL_GROUP]
        w = xg + pltpu.roll(xg, 1, 0)
        half = 1
        while 2 * half < win:
            w = pltpu.roll(w, half, 0) + pltpu.roll(w, n_ext - half, 0)
            half *= 2
        lo = jnp.maximum(pos - win // 2, 0)
        hi = jnp.minimum(pos - win // 2 + win - 1, seq - 1)
        mean = w[HALO:HALO + tile] / (hi - lo + 1).astype(jnp.float32)
        dg = (mean - pc[:, g * POOL_GROUP:(g + 1) * POOL_GROUP]).astype(jnp.bfloat16)
        pooled.append(jnp.dot(dg, wpool_ref[g], preferred_element_type=jnp.float32))
    pool = jnp.concatenate(pooled, axis=1) * pscale_ref[...]

    mix = jnp.concatenate([att_ref[...], pool.astype(jnp.bfloat16)], axis=1)
    x1 = x_ref[...] + jnp.dot(mix, wout_ref[...], preferred_element_type=jnp.float32)
    x1_ref[...] = x1

    h2 = _rms(x1, g2_ref[...])
    for c in range(ROW_CHUNKS):
        hflat_ref[pl.ds(c, tile, stride=ROW_CHUNKS), :] = h2[:, c * LANES:(c + 1) * LANES]

    logits = lax.dot_general(wr_ref[...], h2.astype(jnp.bfloat16), _NT,
                             preferred_element_type=jnp.float32) + br_ref[...]
    eio = lax.broadcasted_iota(jnp.int32, logits.shape, 0)
    work = logits
    sel = jnp.zeros(logits.shape, jnp.float32)
    top_v, top_i, hot = [], [], []
    for _ in range(TOP_K):
        mx = jnp.max(work, axis=0, keepdims=True)
        idx = jnp.min(jnp.where(work == mx, eio, N_EXPERTS), axis=0, keepdims=True)
        oh = eio == idx
        top_v.append(mx)
        top_i.append(idx)
        hot.append(oh)
        work = jnp.where(oh, -jnp.inf, work)
        sel = sel + oh.astype(jnp.float32)
    ex = [jnp.exp(v - top_v[0]) for v in top_v]
    den = ex[0] + ex[1] + ex[2] + ex[3]
    ahead = jnp.dot(sel.astype(jnp.bfloat16), tri_ref[...], preferred_element_type=jnp.float32) + cnt_ref[...]
    for j in range(TOP_K):
        topi_ref[j:j + 1, :] = top_i[j]
        gate_ref[j:j + 1, :] = ex[j] / den
        rank_ref[j:j + 1, :] = jnp.sum(jnp.where(hot[j], ahead, 0.0), axis=0, keepdims=True).astype(jnp.int32)
    base_ref[0] = jnp.broadcast_to(cnt_ref[...], base_ref.shape[1:])
    cnt_ref[...] = cnt_ref[...] + jnp.sum(sel, axis=1, keepdims=True)
    cnt_out_ref[...] = jnp.broadcast_to(cnt_ref[...], cnt_out_ref.shape)


def _mix_route(att, proj, x2d, wpool, pscale, wout, g2, wr_t, br, tri, *, seq):
    n = x2d.shape[0]
    t = TOK_TILE
    nt = n // t
    pcol = QK_WIDTH // POOL_WIDTH
    hb = t // HALO
    last_halo = n // HALO - 1
    const = lambda shape: pl.BlockSpec(shape, lambda i: (0,) * len(shape))
    out_shape = (jax.ShapeDtypeStruct((n, D_MODEL), jnp.float32),
                 jax.ShapeDtypeStruct((n * ROW_CHUNKS, LANES), jnp.float32),
                 jax.ShapeDtypeStruct((TOP_K, n), jnp.int32),
                 jax.ShapeDtypeStruct((TOP_K, n), jnp.int32),
                 jax.ShapeDtypeStruct((TOP_K, n), jnp.float32),
                 jax.ShapeDtypeStruct((nt, N_EXPERTS, LANES), jnp.float32),
                 jax.ShapeDtypeStruct((N_EXPERTS, LANES), jnp.float32))
    return pl.pallas_call(
        functools.partial(_mix_route_kernel, tile=t, seq=seq),
        out_shape=out_shape,
        grid=(nt,),
        in_specs=[pl.BlockSpec((t, ATT_WIDTH), lambda i: (i, 0)),
                  pl.BlockSpec((t, POOL_WIDTH), lambda i: (i, pcol)),
                  pl.BlockSpec((HALO, POOL_WIDTH), lambda i: (jnp.maximum(i * hb - 1, 0), pcol)),
                  pl.BlockSpec((HALO, POOL_WIDTH), lambda i: (jnp.minimum((i + 1) * hb, last_halo), pcol)),
                  pl.BlockSpec((t, D_MODEL), lambda i: (i, 0)),
                  const((len(POOL_WINDOWS), POOL_GROUP, POOL_GROUP)),
                  const((1, POOL_WIDTH)),
                  const((D_MODEL, D_MODEL)),
                  const((1, D_MODEL)),
                  const((N_EXPERTS, D_MODEL)),
                  const((N_EXPERTS, 1)),
                  const((t, t))],
        out_specs=(pl.BlockSpec((t, D_MODEL), lambda i: (i, 0)),
                   pl.BlockSpec((t * ROW_CHUNKS, LANES), lambda i: (i, 0)),
                   pl.BlockSpec((TOP_K, t), lambda i: (0, i)),
                   pl.BlockSpec((TOP_K, t), lambda i: (0, i)),
                   pl.BlockSpec((TOP_K, t), lambda i: (0, i)),
                   pl.BlockSpec((1, N_EXPERTS, LANES), lambda i: (i, 0, 0)),
                   pl.BlockSpec((N_EXPERTS, LANES), lambda i: (0, 0))),
        scratch_shapes=[pltpu.VMEM((N_EXPERTS, 1), jnp.float32)],
        compiler_params=pltpu.CompilerParams(dimension_semantics=("arbitrary",),
                                             vmem_limit_bytes=VMEM_LIMIT),
        name="mix_out_route",
    )(att, proj, proj, proj, x2d, wpool, pscale.reshape(1, -1), wout, g2.reshape(1, -1), wr_t,
      br.reshape(-1, 1), tri)


def _row(ref, r):
    return ref.at[pl.ds(pl.multiple_of(r * ROW_CHUNKS, ROW_CHUNKS), ROW_CHUNKS), :]


def _dispatch_kernel(pad_lo_ref, pad_n_ref, dest_hbm, h_ref, xs_out, idx_ref, zero_ref, idx_sem, row_sem, pad_sem,
                     *, tile):
    i = pl.program_id(0)
    per = tile * TOP_K

    @pl.when(i == 0)
    def _():
        zero_ref[...] = jnp.zeros(zero_ref.shape, zero_ref.dtype)

        def zero_rows(first_row, n_rows, act):
            dst = xs_out.at[pl.ds(pl.multiple_of(first_row * ROW_CHUNKS, ROW_CHUNKS), n_rows * ROW_CHUNKS), :]
            act(pltpu.make_async_copy(zero_ref.at[pl.ds(0, n_rows * ROW_CHUNKS), :], dst, pad_sem))

        def each_piece(act):
            def per_expert(e, c):
                row, left = pad_lo_ref[e], pad_n_ref[e]
                piece = EXPERT_ROWS // 2
                while piece >= 1:
                    take = left >= piece

                    @pl.when(take)
                    def _(row=row, piece=piece):
                        zero_rows(row, piece, act)

                    row = row + jnp.where(take, piece, 0)
                    left = left - jnp.where(take, piece, 0)
                    piece //= 2
                return c

            lax.fori_loop(0, N_EXPERTS, per_expert, 0)
            lax.fori_loop(0, pad_n_ref[N_EXPERTS],
                          lambda k, c: (zero_rows(pad_lo_ref[N_EXPERTS] + k * EXPERT_ROWS, EXPERT_ROWS, act), c)[1], 0)

        each_piece(lambda cp: cp.start())
        each_piece(lambda cp: cp.wait())

    idx_copy = pltpu.make_async_copy(dest_hbm.at[pl.ds(pl.multiple_of(i * per, per), per)], idx_ref, idx_sem)
    idx_copy.start()
    idx_copy.wait()

    def body(t, c):
        src = _row(h_ref, t)
        for j in range(TOP_K):
            pltpu.make_async_copy(src, _row(xs_out, idx_ref[t * TOP_K + j]), row_sem).start(priority=j % 2)
        return c

    lax.fori_loop(0, tile, body, 0)
    for _ in range(TOP_K):
        pltpu.make_async_copy(h_ref, xs_out.at[pl.ds(0, tile * ROW_CHUNKS), :], row_sem).wait()


def _dispatch(pad_lo, pad_n, dest_flat, hflat, n_rows):
    t = DISPATCH_TILE
    per = t * TOP_K
    nt = dest_flat.shape[0] // per
    return pl.pallas_call(
        functools.partial(_dispatch_kernel, tile=t),
        out_shape=jax.ShapeDtypeStruct((n_rows * ROW_CHUNKS, LANES), jnp.float32),
        grid_spec=pltpu.PrefetchScalarGridSpec(
            num_scalar_prefetch=2,
            grid=(nt,),
            in_specs=[pl.BlockSpec(memory_space=pl.ANY),
                      pl.BlockSpec((t * ROW_CHUNKS, LANES), lambda i, lo, cnt: (i, 0))],
            out_specs=pl.BlockSpec(memory_space=pl.ANY),
            scratch_shapes=[pltpu.SMEM((per,), jnp.int32),
                            pltpu.VMEM((EXPERT_ROWS * ROW_CHUNKS, LANES), jnp.float32),
                            pltpu.SemaphoreType.DMA,
                            pltpu.SemaphoreType.DMA,
                            pltpu.SemaphoreType.DMA]),
        compiler_params=pltpu.CompilerParams(dimension_semantics=("arbitrary",),
                                             vmem_limit_bytes=VMEM_LIMIT),
        name="moe_dispatch",
    )(pad_lo, pad_n, dest_flat, hflat)


def _expert_kernel(be_ref, used_ref, x_ref, wg_ref, bg_ref, wu_ref, bu_ref, wd_ref, bd_ref, y_ref,
                   wg_bf, wu_bf, wd_bf, *, rows):
    b = pl.program_id(0)
    live = b < used_ref[0]

    @pl.when(jnp.logical_not(live))
    def _():
        y_ref[...] = jnp.zeros(y_ref.shape, y_ref.dtype)

    @pl.when(jnp.logical_and(live, jnp.logical_or(b == 0, be_ref[b] != be_ref[jnp.maximum(b - 1, 0)])))
    def _():
        wg_bf[...] = wg_ref[0, 0].astype(jnp.bfloat16)
        wu_bf[...] = wu_ref[0, 0].astype(jnp.bfloat16)
        wd_bf[...] = wd_ref[0, 0].astype(jnp.bfloat16)

    @pl.when(live)
    def _():
        x = jnp.concatenate([x_ref[pl.ds(c, rows, stride=ROW_CHUNKS), :] for c in range(ROW_CHUNKS)],
                            axis=1).astype(jnp.bfloat16)
        half = D_MODEL // 2
        halves = (slice(0, half), slice(half, D_MODEL))
        gu = [(jnp.dot(x, wg_bf[:, cols], preferred_element_type=jnp.float32) + bg_ref[0, 0, :, cols],
               jnp.dot(x, wu_bf[:, cols], preferred_element_type=jnp.float32) + bu_ref[0, 0, :, cols])
              for cols in halves]
        y = bd_ref[0, 0]
        for (g, u), cols in zip(gu, halves):
            g = jnp.minimum(g, SWIGLU_LIMIT)
            u = jnp.clip(u, -SWIGLU_LIMIT, SWIGLU_LIMIT)
            act = (u + 1.0) * (g * (1.0 / (1.0 + jnp.exp(-SWIGLU_ALPHA * g))))
            y = y + jnp.dot(act.astype(jnp.bfloat16), wd_bf[cols, :], preferred_element_type=jnp.float32)
        for c in range(ROW_CHUNKS):
            y_ref[pl.ds(c, rows, stride=ROW_CHUNKS), :] = y[:, c * LANES:(c + 1) * LANES]


def _experts(block_expert, n_used, xs, wg, bg, wu, bu, wd, bd, *, layer):
    nb = block_expert.shape[0]
    rows = EXPERT_ROWS
    depth = wg.shape[0]
    blk = lambda b, be, used: (jnp.minimum(b, used[0] - 1), 0)
    wsel = lambda b, be, used: (layer, be[b], 0, 0)
    wspec = pl.BlockSpec((1, 1, D_MODEL, D_MODEL), wsel)
    bspec = pl.BlockSpec((1, 1, 1, D_MODEL), wsel)
    bias4 = lambda a: a.reshape(depth, N_EXPERTS, 1, D_MODEL)
    return pl.pallas_call(
        functools.partial(_expert_kernel, rows=rows),
        out_shape=jax.ShapeDtypeStruct(xs.shape, jnp.float32),
        grid_spec=pltpu.PrefetchScalarGridSpec(
            num_scalar_prefetch=2,
            grid=(nb,),
            in_specs=[pl.BlockSpec((rows * ROW_CHUNKS, LANES), blk),
                      wspec, bspec, wspec, bspec, wspec, bspec],
            out_specs=pl.BlockSpec((rows * ROW_CHUNKS, LANES), lambda b, be, used: (b, 0)),
            scratch_shapes=[pltpu.VMEM((D_MODEL, D_MODEL), jnp.bfloat16)] * 3),
        compiler_params=pltpu.CompilerParams(dimension_semantics=("arbitrary",),
                                             vmem_limit_bytes=VMEM_LIMIT),
        name="moe_experts",
    )(block_expert, n_used, xs, wg, bias4(bg), wu, bias4(bu), wd, bias4(bd))


def _combine_kernel(cstart_ref, clen_ref, pchunk_ref, gbound_ref, ys_hbm, x_ref, dest_ref, gate_ref, gf_ref,
                    o_ref, zbuf_ref, acc_ref, destb_ref, gateb_ref, sem, *, tile, final):
    i = pl.program_id(0)
    n_tiles = pl.num_programs(0)
    slab_rows = SLAB_CHUNKS * COMBINE_CHUNK
    chunk_flat = COMBINE_CHUNK * ROW_CHUNKS

    def pair_copies(first_chunk, ring):
        return [pltpu.make_async_copy(
            ys_hbm.at[pl.ds(pl.multiple_of(cstart_ref[first_chunk + q] * ROW_CHUNKS, ROW_CHUNKS), chunk_flat), :],
            zbuf_ref.at[ring, q // SLAB_CHUNKS, pl.ds((q % SLAB_CHUNKS) * chunk_flat, chunk_flat), :],
            sem.at[ring]) for q in range(2 * SLAB_CHUNKS)]

    g0 = gbound_ref[i]
    n_pairs = gbound_ref[i + 1] - g0
    g_total = gbound_ref[n_tiles]

    @pl.when(i == 0)
    def _():
        for cp in pair_copies(pchunk_ref[0], 0):
            cp.start()

        @pl.when(g_total > 1)
        def _():
            for cp in pair_copies(pchunk_ref[1], 1):
                cp.start()

    acc_ref[...] = x_ref[...]
    for j in range(TOP_K):
        destb_ref[j] = jnp.broadcast_to(dest_ref[:, j:j + 1], (tile, LANES))
        gateb_ref[j] = jnp.broadcast_to(gate_ref[:, j:j + 1], (tile, LANES))
    col = lax.broadcasted_iota(jnp.int32, (1, slab_rows), 1)
    col_chunk = col // COMBINE_CHUNK
    col_row = col % COMBINE_CHUNK

    def gate_matrix(first):
        rows = jnp.full((1, slab_rows), -1, jnp.int32)
        for q in range(SLAB_CHUNKS):
            ok = jnp.logical_and(col_chunk == q, col_row < clen_ref[first + q])
            rows = jnp.where(ok, cstart_ref[first + q] + col_row, rows)
        g_cols = []
        for c0 in range(0, slab_rows, LANES):
            rows_c = rows[:, c0:c0 + LANES]
            g_c = jnp.zeros((tile, LANES), jnp.float32)
            for j in range(TOP_K):
                g_c = jnp.where(destb_ref[j] == rows_c, gateb_ref[j], g_c)
            g_cols.append(g_c.astype(jnp.bfloat16))
        return jnp.concatenate(g_cols, axis=1)

    def pair(p, carry):
        g = g0 + p
        ring = g % 2
        first = pchunk_ref[g]
        for cp in pair_copies(first, ring):
            cp.wait()
        ys_pair = [jnp.concatenate([zbuf_ref[ring, slot, pl.ds(c, slab_rows, stride=ROW_CHUNKS), :]
                                    for c in range(ROW_CHUNKS)], axis=1).astype(jnp.bfloat16)
                   for slot in range(2)]

        @pl.when(g + 2 < g_total)
        def _():
            for cp in pair_copies(pchunk_ref[g + 2], ring):
                cp.start()

        for slot in range(2):
            acc_ref[...] += jnp.dot(gate_matrix(first + slot * SLAB_CHUNKS), ys_pair[slot],
                                    preferred_element_type=jnp.float32)
        return carry

    lax.fori_loop(0, n_pairs, pair, 0)
    x2 = acc_ref[...]
    o_ref[...] = _rms(x2, gf_ref[...]) if final else x2


def _combine(chunk_start, chunk_len, pair_chunk, pair_bound, ys, x1, dest_tok, gates_tok, g_final, *, final):
    t = TOK_TILE
    n = x1.shape[0]
    tok = lambda width: pl.BlockSpec((t, width), lambda i, *_: (i, 0))
    return pl.pallas_call(
        functools.partial(_combine_kernel, tile=t, final=final),
        out_shape=jax.ShapeDtypeStruct((n, D_MODEL), jnp.float32),
        grid_spec=pltpu.PrefetchScalarGridSpec(
            num_scalar_prefetch=4,
            grid=(n // t,),
            in_specs=[pl.BlockSpec(memory_space=pl.ANY),
                      tok(D_MODEL), tok(TOP_K), tok(TOP_K),
                      pl.BlockSpec((1, D_MODEL), lambda i, *_: (0, 0))],
            out_specs=tok(D_MODEL),
            scratch_shapes=[pltpu.VMEM((2, 2, SLAB_CHUNKS * COMBINE_CHUNK * ROW_CHUNKS, LANES), jnp.float32),
                            pltpu.VMEM((t, D_MODEL), jnp.float32),
                            pltpu.VMEM((TOP_K, t, LANES), jnp.int32),
                            pltpu.VMEM((TOP_K, t, LANES), jnp.float32),
                            pltpu.SemaphoreType.DMA((2,))]),
        compiler_params=pltpu.CompilerParams(dimension_semantics=("arbitrary",),
                                             vmem_limit_bytes=VMEM_LIMIT),
        name="moe_combine",
    )(chunk_start, chunk_len, pair_chunk, pair_bound, ys, x1, dest_tok, gates_tok, g_final.reshape(1, -1))


def _combine_chunks(base, cnt, seg_start):
    n_run = jnp.concatenate([base[1:], cnt[None]], axis=0) - base
    run_start = seg_start[None, :] + base
    n_chunks = (n_run + COMBINE_CHUNK - 1) // COMBINE_CHUNK
    cum = jnp.cumsum(n_chunks, axis=1)
    total = cum[:, -1]
    slots = jnp.arange(MAX_CHUNKS, dtype=jnp.int32)
    expert = jnp.minimum(jnp.sum(cum[:, None, :] <= slots[None, :, None], axis=2), N_EXPERTS - 1)
    hot = expert[..., None] == jnp.arange(N_EXPERTS, dtype=jnp.int32)
    pick = lambda a: jnp.sum(jnp.where(hot, a[:, None, :], 0), axis=2)
    k = slots[None, :] - (pick(cum) - pick(n_chunks))
    valid = slots[None, :] < total[:, None]
    start = jnp.where(valid, pick(run_start) + COMBINE_CHUNK * k, 0)
    length = jnp.where(valid, jnp.clip(pick(n_run) - COMBINE_CHUNK * k, 0, COMBINE_CHUNK), 0)
    pair_size = 2 * SLAB_CHUNKS
    n_tiles = base.shape[0]
    n_pairs = (total + pair_size - 1) // pair_size
    pair_end = jnp.cumsum(n_pairs)
    pair_bound = jnp.concatenate([jnp.zeros((1,), pair_end.dtype), pair_end])
    g = jnp.arange(n_tiles * (MAX_CHUNKS // pair_size), dtype=jnp.int32)
    tile_of = jnp.minimum(jnp.sum(pair_end[None, :] <= g[:, None], axis=1), n_tiles - 1)
    tile_first = jnp.sum(jnp.where(tile_of[:, None] == jnp.arange(n_tiles), (pair_end - n_pairs)[None, :], 0), axis=1)
    pair_chunk = tile_of * MAX_CHUNKS + pair_size * (g - tile_first)
    i32 = lambda a: a.reshape(-1).astype(jnp.int32)
    return i32(start), i32(length), i32(pair_chunk), i32(pair_bound)


def kernel(x, rel_table, norm1, w_in, lambda_q1, lambda_k1, lambda_q2, lambda_k2, subln_g, w_pool, pool_scale,
           w_out, norm2, w_router, b_router, w_gate, b_gate, w_up, b_up, w_down, b_down, final_norm):
    batch, seq, d = x.shape
    depth = norm1.shape[0]
    n = batch * seq
    assert d == D_MODEL and seq % ATT_TILE == 0 and seq % TOK_TILE == 0
    assert n % DISPATCH_TILE == 0
    bf = jnp.bfloat16

    w_q = w_in[:, :, :QK_WIDTH] * (QK_DIM ** -0.5 * LOG2E)
    w_k = w_in[:, :, QK_WIDTH:2 * QK_WIDTH]
    w_v = w_in[:, :, 2 * QK_WIDTH:2 * QK_WIDTH + ATT_WIDTH]
    w_p = w_in[:, :, 2 * QK_WIDTH + ATT_WIDTH:]
    w_kp = jnp.concatenate([w_k, w_p], axis=2).astype(bf)
    w_qvt = jnp.swapaxes(jnp.concatenate([w_q, w_v], axis=2), 1, 2).astype(bf)
    w_out_b = w_out.astype(bf)
    w_pool_b = w_pool.astype(bf)
    wr_t = jnp.swapaxes(w_router, 1, 2).astype(bf)
    tri = (lax.broadcasted_iota(jnp.int32, (TOK_TILE, TOK_TILE), 0)
           < lax.broadcasted_iota(jnp.int32, (TOK_TILE, TOK_TILE), 1)).astype(bf)

    bias = _bias_tiles(rel_table, ATT_TILE)
    far = jnp.stack([rel_table[N_BUCKETS // 2 - 1], rel_table[N_BUCKETS - 1]], axis=1).reshape(-1)

    n_assign = n * TOP_K
    n_blocks = -(-(n_assign + N_EXPERTS * (EXPERT_ROWS - 1)) // EXPERT_ROWS) + 1
    n_rows = n_blocks * EXPERT_ROWS
    block_start = jnp.arange(n_blocks, dtype=jnp.int32) * EXPERT_ROWS

    x2d = x.reshape(n, d)
    for l in range(depth):
        lam_init = 0.8 - 0.6 * math.exp(-0.3 * l)
        kp, qvt = _norm_proj(x2d, norm1[l], w_kp[l], w_qvt[l])
        att = _attention(kp, qvt, bias, far, lambda_q1[l], lambda_k1[l], lambda_q2[l], lambda_k2[l], subln_g[l],
                         batch=batch, seq=seq, lam_init=lam_init)
        x1, hflat, topi, rank, gates, base, counts = _mix_route(
            att, kp, x2d, w_pool_b[l], pool_scale[l], w_out_b[l], norm2[l], wr_t[l], b_router[l], tri, seq=seq)

        cnt = counts[:, 0].astype(jnp.int32)
        padded = ((cnt + EXPERT_ROWS - 1) // EXPERT_ROWS) * EXPERT_ROWS
        pad_end = jnp.cumsum(padded)
        experts = jnp.arange(N_EXPERTS, dtype=jnp.int32)
        seg_start = pad_end - padded
        dest_tok = (jnp.sum(jnp.where(topi[..., None] == experts, seg_start, 0), axis=-1) + rank).T
        dest_flat = dest_tok.reshape(-1)
        block_expert = jnp.minimum(jnp.sum(pad_end[None, :] <= block_start[:, None], axis=1),
                                   N_EXPERTS - 1).astype(jnp.int32)
        n_used = (pad_end[-1:] // EXPERT_ROWS).astype(jnp.int32)

        pad_lo = jnp.concatenate([pad_end - padded + cnt, pad_end[-1:]])
        pad_n = jnp.concatenate([padded - cnt, (n_rows - pad_end[-1:]) // EXPERT_ROWS])
        xs = _dispatch(pad_lo, pad_n, dest_flat, hflat, n_rows)
        ys = _experts(block_expert, n_used, xs, w_gate, b_gate, w_up, b_up, w_down, b_down, layer=l)
        chunk_lists = _combine_chunks(base[:, :, 0].astype(jnp.int32), cnt, seg_start)
        x2d = _combine(*chunk_lists, ys, x1, dest_tok, gates.T, final_norm,
                       final=(l == depth - 1))
    return x2d.reshape(batch, seq, d)
```

```python
import functools
import math

import jax
import jax.numpy as jnp
from jax import lax
from jax.experimental import pallas as pl
from jax.experimental.pallas import tpu as pltpu

D_MODEL = 1024
N_HEADS = 4
QK_DIM = 64
V_DIM = 128
QK_WIDTH = N_HEADS * 2 * QK_DIM
ATT_WIDTH = N_HEADS * V_DIM
POOL_WINDOWS = (2, 4, 8, 16)
POOL_GROUP = 128
POOL_WIDTH = 512
N_BUCKETS = 32
MAX_DISTANCE = 128
N_EXPERTS = 32
TOP_K = 4
SWIGLU_LIMIT = 7.0
SWIGLU_ALPHA = 1.702
EPS = 1e-5

LANES = 128
BF16_SUBLANES = 16
MXU_DEPTH = 256
ROW_CHUNKS = D_MODEL // LANES
VMEM_LIMIT = 56 * 1024 * 1024

TOK_TILE = 512
ATT_TILE = 512
ATT_CHUNK = 512
BIAS_TILES = 5
NEAR_TILES = 4
V_EXT = V_DIM + BF16_SUBLANES
HALO = BF16_SUBLANES
EXPERT_ROWS = 512
DISPATCH_TILE = 2048
COMBINE_CHUNK = 32
SLAB_CHUNKS = MXU_DEPTH // COMBINE_CHUNK
MAX_CHUNKS = TOK_TILE * TOP_K // COMBINE_CHUNK + N_EXPERTS

_NT = (((1,), (1,)), ((), ()))
LOG2E = math.log2(math.e)


def _rms(x, g):
    return x * lax.rsqrt(jnp.mean(x * x, axis=-1, keepdims=True) + EPS) * g


def _bias_kernel(tab_ref, out_ref, *, tile):
    h = pl.program_id(0)
    d = pl.program_id(1)
    key = lax.broadcasted_iota(jnp.int32, (tile, tile), 0)
    qry = lax.broadcasted_iota(jnp.int32, (tile, tile), 1)
    rel = (d - BIAS_TILES // 2) * tile + key - qry
    nb = N_BUCKETS // 2
    max_exact = nb // 2
    ret = jnp.where(rel > 0, nb, 0)
    n = jnp.abs(rel)
    n_f = jnp.maximum(n, 1).astype(jnp.float32)
    large = max_exact + (jnp.log(n_f / max_exact) / math.log(MAX_DISTANCE / max_exact)
                         * (nb - max_exact)).astype(jnp.int32)
    large = jnp.minimum(large, nb - 1)
    bucket = ret + jnp.where(n < max_exact, n, large)
    acc = jnp.zeros((tile, tile), jnp.float32)
    for b in range(N_BUCKETS):
        acc = jnp.where(bucket == b, tab_ref[b * N_HEADS + h], acc)
    out_ref[0, 0] = acc * LOG2E


def _bias_tiles(rel_table, tile):
    nb, max_exact = N_BUCKETS // 2, N_BUCKETS // 4
    last_bucket_from = max_exact * (MAX_DISTANCE / max_exact) ** ((nb - 1 - max_exact) / (nb - max_exact))
    assert tile + 1 >= last_bucket_from
    return pl.pallas_call(
        functools.partial(_bias_kernel, tile=tile),
        out_shape=jax.ShapeDtypeStruct((N_HEADS, BIAS_TILES, tile, tile), jnp.float32),
        grid=(N_HEADS, BIAS_TILES),
        in_specs=[pl.BlockSpec(memory_space=pltpu.SMEM)],
        out_specs=pl.BlockSpec((1, 1, tile, tile), lambda h, d: (h, d, 0, 0)),
        name="rel_bias_tiles",
    )(rel_table.reshape(-1))


def _norm_proj_kernel(x_ref, g_ref, wkp_ref, wqvt_ref, kp_ref, qvt_ref):
    h = _rms(x_ref[...], g_ref[...]).astype(jnp.bfloat16)
    kp_ref[...] = jnp.dot(h, wkp_ref[...], preferred_element_type=jnp.float32).astype(kp_ref.dtype)
    qvt_ref[...] = lax.dot_general(wqvt_ref[...], h, _NT,
                                   preferred_element_type=jnp.float32).astype(qvt_ref.dtype)


def _norm_proj(x2d, g, w_kp, w_qvt):
    n = x2d.shape[0]
    width = QK_WIDTH + POOL_WIDTH
    width_t = QK_WIDTH + ATT_WIDTH
    return pl.pallas_call(
        _norm_proj_kernel,
        out_shape=(jax.ShapeDtypeStruct((n, width), jnp.bfloat16),
                   jax.ShapeDtypeStruct((width_t, n), jnp.bfloat16)),
        grid=(n // TOK_TILE,),
        in_specs=[pl.BlockSpec((TOK_TILE, D_MODEL), lambda i: (i, 0)),
                  pl.BlockSpec((1, D_MODEL), lambda i: (0, 0)),
                  pl.BlockSpec((D_MODEL, width), lambda i: (0, 0)),
                  pl.BlockSpec((width_t, D_MODEL), lambda i: (0, 0))],
        out_specs=(pl.BlockSpec((TOK_TILE, width), lambda i: (i, 0)),
                   pl.BlockSpec((width_t, TOK_TILE), lambda i: (0, i))),
        compiler_params=pltpu.CompilerParams(dimension_semantics=("parallel",),
                                             vmem_limit_bytes=VMEM_LIMIT),
        name="norm_in_proj",
    )(x2d, g.reshape(1, D_MODEL), w_kp, w_qvt)


def _attn_kernel(far_ref, qt_ref, k_ref, vt_ref, bias_ref, lq1_ref, lk1_ref, lq2_ref, lk2_ref, sg_ref,
                 o_ref, qs_ref, vte_ref, s0_ref, s1_ref, mc0_ref, mc1_ref, m_ref, acc_ref,
                 *, tile, n_kv, lam_init):
    h = pl.program_id(1)
    qi = pl.program_id(2)
    n_far = n_kv - NEAR_TILES

    @pl.when(qi == 0)
    def _():
        vte_ref[:V_DIM, :] = vt_ref[...]
        vte_ref[V_DIM:, :] = jnp.ones((V_EXT - V_DIM, vte_ref.shape[1]), vte_ref.dtype)

    qt = qt_ref[...]
    row = lax.broadcasted_iota(jnp.int32, qt.shape, 0)
    zero = jnp.zeros_like(qt)
    qs_ref[:, :tile] = jnp.where(row < QK_DIM, qt, zero)
    qs_ref[:, tile:] = jnp.where(row >= QK_DIM, qt, zero)

    m_ref[...] = jnp.full(m_ref.shape, -jnp.inf, jnp.float32)
    acc_ref[...] = jnp.zeros(acc_ref.shape, jnp.float32)
    chunks = [slice(c * ATT_CHUNK, (c + 1) * ATT_CHUNK) for c in range(2 * tile // ATT_CHUNK)]
    w0 = jnp.clip(qi - 1, 0, n_kv - NEAR_TILES)
    c_below = far_ref[2 * h] * LOG2E
    c_above = far_ref[2 * h + 1] * LOG2E

    def far_tile(f):
        return jnp.where(f < w0, f, f + NEAR_TILES)

    def far_const(ki):
        return jnp.where(ki < qi, c_below, c_above)

    def logits(ki, cols):
        k = k_ref[pl.ds(pl.multiple_of(ki * tile, tile), tile), :]
        return jnp.dot(k, qs_ref[:, cols], preferred_element_type=jnp.float32)

    def scores_near(ki, cols, s_ref, mc_ref):
        d = jnp.clip(ki - qi, -(BIAS_TILES // 2), BIAS_TILES // 2) + BIAS_TILES // 2
        q0 = cols.start % tile
        s = logits(ki, cols) + bias_ref[0, d, :, q0:q0 + ATT_CHUNK]
        s_ref[:, cols] = s
        mc_ref[:, cols] = jnp.max(s, axis=0, keepdims=True)

    def scores_far(ki, cols, s_ref, mc_ref):
        s = logits(ki, cols)
        s_ref[:, cols] = s
        mc_ref[:, cols] = jnp.max(s, axis=0, keepdims=True) + far_const(ki)

    def update(ki, cols, s_ref, mc_ref, const):
        vt = vte_ref[:, pl.ds(pl.multiple_of(ki * tile, tile), tile)]
        m_prev = m_ref[:, cols]
        m_new = jnp.maximum(m_prev, mc_ref[:, cols])
        alpha = jnp.exp2(m_prev - m_new)
        p = jnp.exp2(s_ref[:, cols] - (m_new - const))
        acc_ref[:, cols] = alpha * acc_ref[:, cols] + jnp.dot(vt, p.astype(jnp.bfloat16),
                                                              preferred_element_type=jnp.float32)
        m_ref[:, cols] = m_new

    bufs = ((s0_ref, mc0_ref), (s1_ref, mc1_ref))
    for cols in chunks:
        scores_near(w0, cols, *bufs[0])
    for i in range(NEAR_TILES):
        for cols in chunks:
            if i + 1 < NEAR_TILES:
                scores_near(w0 + i + 1, cols, *bufs[(i + 1) % 2])
            elif n_far > 0:
                scores_far(far_tile(0), cols, *bufs[(i + 1) % 2])
            update(w0 + i, cols, *bufs[i % 2], 0.0)

    def pair(j, look_ahead=True):
        ka, kb, kc = far_tile(2 * j), far_tile(2 * j + 1), far_tile(2 * j + 2)
        for cols in chunks:
            scores_far(kb, cols, *bufs[(NEAR_TILES + 1) % 2])
            update(ka, cols, *bufs[NEAR_TILES % 2], far_const(ka))
        for cols in chunks:
            if look_ahead:
                scores_far(kc, cols, *bufs[NEAR_TILES % 2])
            update(kb, cols, *bufs[(NEAR_TILES + 1) % 2], far_const(kb))

    if n_far > 0:
        lax.fori_loop(0, n_far // 2 - 1, lambda j, c: (pair(j), c)[1], 0)
        pair(n_far // 2 - 1, look_ahead=False)

    acc = acc_ref[...]
    o = acc[:V_DIM] / acc[V_DIM:V_DIM + 1]
    lam = (jnp.exp(jnp.sum(lq1_ref[...] * lk1_ref[...], axis=1, keepdims=True))
           - jnp.exp(jnp.sum(lq2_ref[...] * lk2_ref[...], axis=1, keepdims=True)) + lam_init)
    o = o[:, :tile] - lam * o[:, tile:]
    o = o * lax.rsqrt(jnp.mean(o * o, axis=0, keepdims=True) + EPS) * (sg_ref[...] * (1.0 - lam_init))
    o_ref[...] = o.T.astype(o_ref.dtype)


def _attention(kp, qvt, bias, far, lq1, lk1, lq2, lk2, sub_g, *, batch, seq, lam_init):
    t = ATT_TILE
    nq = seq // t
    assert nq >= NEAR_TILES and (nq - NEAR_TILES) % 2 == 0
    vrow = QK_WIDTH // LANES
    vec = lambda width: pl.BlockSpec((1, width), lambda b, h, qi: (0, 0))
    return pl.pallas_call(
        functools.partial(_attn_kernel, tile=t, n_kv=nq, lam_init=lam_init),
        out_shape=jax.ShapeDtypeStruct((batch * seq, ATT_WIDTH), jnp.bfloat16),
        grid=(batch, N_HEADS, nq),
        in_specs=[pl.BlockSpec(memory_space=pltpu.SMEM),
                  pl.BlockSpec((LANES, t), lambda b, h, qi: (h, b * nq + qi)),
                  pl.BlockSpec((seq, LANES), lambda b, h, qi: (b, h)),
                  pl.BlockSpec((V_DIM, seq), lambda b, h, qi: (vrow + h, b)),
                  pl.BlockSpec((1, BIAS_TILES, t, t), lambda b, h, qi: (h, 0, 0, 0)),
                  vec(QK_DIM), vec(QK_DIM), vec(QK_DIM), vec(QK_DIM),
                  pl.BlockSpec((V_DIM, 1), lambda b, h, qi: (0, 0))],
        out_specs=pl.BlockSpec((t, V_DIM), lambda b, h, qi: (b * nq + qi, h)),
        scratch_shapes=[pltpu.VMEM((LANES, 2 * t), jnp.bfloat16),
                        pltpu.VMEM((V_EXT, seq), jnp.bfloat16),
                        pltpu.VMEM((t, 2 * t), jnp.float32),
                        pltpu.VMEM((t, 2 * t), jnp.float32),
                        pltpu.VMEM((1, 2 * t), jnp.float32),
                        pltpu.VMEM((1, 2 * t), jnp.float32),
                        pltpu.VMEM((1, 2 * t), jnp.float32),
                        pltpu.VMEM((V_EXT, 2 * t), jnp.float32)],
        compiler_params=pltpu.CompilerParams(dimension_semantics=("arbitrary", "arbitrary", "arbitrary"),
                                             vmem_limit_bytes=VMEM_LIMIT),
        name="diff_attention",
    )(far, qvt, kp, qvt, bias, lq1.reshape(1, -1), lk1.reshape(1, -1), lq2.reshape(1, -1),
      lk2.reshape(1, -1), sub_g.reshape(-1, 1))


def _mix_route_kernel(att_ref, p_ref, prev_ref, next_ref, x_ref, wpool_ref, pscale_ref, wout_ref, g2_ref,
                      wr_ref, br_ref, tri_ref,
                      x1_ref, hflat_ref, topi_ref, rank_ref, gate_ref, base_ref, cnt_out_ref, cnt_ref,
                      *, tile, seq):
    i = pl.program_id(0)
    tiles_per_seq = seq // tile
    si = i % tiles_per_seq

    @pl.when(i == 0)
    def _():
        cnt_ref[...] = jnp.zeros(cnt_ref.shape, jnp.float32)

    pc = p_ref[...].astype(jnp.float32)
    pp = jnp.where(si == 0, 0.0, prev_ref[...].astype(jnp.float32))
    pn = jnp.where(si == tiles_per_seq - 1, 0.0, next_ref[...].astype(jnp.float32))
    pe = jnp.concatenate([pp, pc, pn], axis=0)
    n_ext = tile + 2 * HALO
    pos = si * tile + lax.broadcasted_iota(jnp.int32, (tile, 1), 0)
    pooled = []
    for g, win in enumerate(POOL_WINDOWS):
        xg = pe[:, g * POOL_GROUP:(g + 1) * POOL_GROUP]
        w = xg + pltpu.roll(xg, 1, 0)
        half = 1
        while 2 * half < win:
            w = pltpu.roll(w, half, 0) + pltpu.roll(w, n_ext - half, 0)
            half *= 2
        lo = jnp.maximum(pos - win // 2, 0)
        hi = jnp.minimum(pos - win // 2 + win - 1, seq - 1)
        mean = w[HALO:HALO + tile] / (hi - lo + 1).astype(jnp.float32)
        dg = (mean - pc[:, g * POOL_GROUP:(g + 1) * POOL_GROUP]).astype(jnp.bfloat16)
        pooled.append(jnp.dot(dg, wpool_ref[g], preferred_element_type=jnp.float32))
    pool = jnp.concatenate(pooled, axis=1) * pscale_ref[...]

    mix = jnp.concatenate([att_ref[...], pool.astype(jnp.bfloat16)], axis=1)
    x1 = x_ref[...] + jnp.dot(mix, wout_ref[...], preferred_element_type=jnp.float32)
    x1_ref[...] = x1

    h2 = _rms(x1, g2_ref[...])
    for c in range(ROW_CHUNKS):
        hflat_ref[pl.ds(c, tile, stride=ROW_CHUNKS), :] = h2[:, c * LANES:(c + 1) * LANES]

    logits = lax.dot_general(wr_ref[...], h2.astype(jnp.bfloat16), _NT,
                             preferred_element_type=jnp.float32) + br_ref[...]
    eio = lax.broadcasted_iota(jnp.int32, logits.shape, 0)
    work = logits
    sel = jnp.zeros(logits.shape, jnp.float32)
    top_v, top_i, hot = [], [], []
    for _ in range(TOP_K):
        mx = jnp.max(work, axis=0, keepdims=True)
        idx = jnp.min(jnp.where(work == mx, eio, N_EXPERTS), axis=0, keepdims=True)
        oh = eio == idx
        top_v.append(mx)
        top_i.append(idx)
        hot.append(oh)
        work = jnp.where(oh, -jnp.inf, work)
        sel = sel + oh.astype(jnp.float32)
    ex = [jnp.exp(v - top_v[0]) for v in top_v]
    den = ex[0] + ex[1] + ex[2] + ex[3]
    ahead = jnp.dot(sel.astype(jnp.bfloat16), tri_ref[...], preferred_element_type=jnp.float32) + cnt_ref[...]
    for j in range(TOP_K):
        topi_ref[j:j + 1, :] = top_i[j]
        gate_ref[j:j + 1, :] = ex[j] / den
        rank_ref[j:j + 1, :] = jnp.sum(jnp.where(hot[j], ahead, 0.0), axis=0, keepdims=True).astype(jnp.int32)
    base_ref[0] = jnp.broadcast_to(cnt_ref[...], base_ref.shape[1:])
    cnt_ref[...] = cnt_ref[...] + jnp.sum(sel, axis=1, keepdims=True)
    cnt_out_ref[...] = jnp.broadcast_to(cnt_ref[...], cnt_out_ref.shape)


def _mix_route(att, proj, x2d, wpool, pscale, wout, g2, wr_t, br, tri, *, seq):
    n = x2d.shape[0]
    t = TOK_TILE
    nt = n // t
    pcol = QK_WIDTH // POOL_WIDTH
    hb = t // HALO
    last_halo = n // HALO - 1
    const = lambda shape: pl.BlockSpec(shape, lambda i: (0,) * len(shape))
    out_shape = (jax.ShapeDtypeStruct((n, D_MODEL), jnp.float32),
                 jax.ShapeDtypeStruct((n * ROW_CHUNKS, LANES), jnp.float32),
                 jax.ShapeDtypeStruct((TOP_K, n), jnp.int32),
                 jax.ShapeDtypeStruct((TOP_K, n), jnp.int32),
                 jax.ShapeDtypeStruct((TOP_K, n), jnp.float32),
                 jax.ShapeDtypeStruct((nt, N_EXPERTS, LANES), jnp.float32),
                 jax.ShapeDtypeStruct((N_EXPERTS, LANES), jnp.float32))
    return pl.pallas_call(
        functools.partial(_mix_route_kernel, tile=t, seq=seq),
        out_shape=out_shape,
        grid=(nt,),
        in_specs=[pl.BlockSpec((t, ATT_WIDTH), lambda i: (i, 0)),
                  pl.BlockSpec((t, POOL_WIDTH), lambda i: (i, pcol)),
                  pl.BlockSpec((HALO, POOL_WIDTH), lambda i: (jnp.maximum(i * hb - 1, 0), pcol)),
                  pl.BlockSpec((HALO, POOL_WIDTH), lambda i: (jnp.minimum((i + 1) * hb, last_halo), pcol)),
                  pl.BlockSpec((t, D_MODEL), lambda i: (i, 0)),
                  const((len(POOL_WINDOWS), POOL_GROUP, POOL_GROUP)),
                  const((1, POOL_WIDTH)),
                  const((D_MODEL, D_MODEL)),
                  const((1, D_MODEL)),
                  const((N_EXPERTS, D_MODEL)),
                  const((N_EXPERTS, 1)),
                  const((t, t))],
        out_specs=(pl.BlockSpec((t, D_MODEL), lambda i: (i, 0)),
                   pl.BlockSpec((t * ROW_CHUNKS, LANES), lambda i: (i, 0)),
                   pl.BlockSpec((TOP_K, t), lambda i: (0, i)),
                   pl.BlockSpec((TOP_K, t), lambda i: (0, i)),
                   pl.BlockSpec((TOP_K, t), lambda i: (0, i)),
                   pl.BlockSpec((1, N_EXPERTS, LANES), lambda i: (i, 0, 0)),
                   pl.BlockSpec((N_EXPERTS, LANES), lambda i: (0, 0))),
        scratch_shapes=[pltpu.VMEM((N_EXPERTS, 1), jnp.float32)],
        compiler_params=pltpu.CompilerParams(dimension_semantics=("arbitrary",),
                                             vmem_limit_bytes=VMEM_LIMIT),
        name="mix_out_route",
    )(att, proj, proj, proj, x2d, wpool, pscale.reshape(1, -1), wout, g2.reshape(1, -1), wr_t,
      br.reshape(-1, 1), tri)


def _row(ref, r):
    return ref.at[pl.ds(pl.multiple_of(r * ROW_CHUNKS, ROW_CHUNKS), ROW_CHUNKS), :]


def _dispatch_kernel(pad_lo_ref, pad_n_ref, dest_hbm, h_ref, xs_out, idx_ref, zero_ref, idx_sem, row_sem, pad_sem,
                     *, tile):
    i = pl.program_id(0)
    per = tile * TOP_K

    @pl.when(i == 0)
    def _():
        zero_ref[...] = jnp.zeros(zero_ref.shape, zero_ref.dtype)

        def zero_rows(first_row, n_rows, act):
            dst = xs_out.at[pl.ds(pl.multiple_of(first_row * ROW_CHUNKS, ROW_CHUNKS), n_rows * ROW_CHUNKS), :]
            act(pltpu.make_async_copy(zero_ref.at[pl.ds(0, n_rows * ROW_CHUNKS), :], dst, pad_sem))

        def each_piece(act):
            def per_expert(e, c):
                row, left = pad_lo_ref[e], pad_n_ref[e]
                piece = EXPERT_ROWS // 2
                while piece >= 1:
                    take = left >= piece

                    @pl.when(take)
                    def _(row=row, piece=piece):
                        zero_rows(row, piece, act)

                    row = row + jnp.where(take, piece, 0)
                    left = left - jnp.where(take, piece, 0)
                    piece //= 2
                return c

            lax.fori_loop(0, N_EXPERTS, per_expert, 0)
            lax.fori_loop(0, pad_n_ref[N_EXPERTS],
                          lambda k, c: (zero_rows(pad_lo_ref[N_EXPERTS] + k * EXPERT_ROWS, EXPERT_ROWS, act), c)[1], 0)

        each_piece(lambda cp: cp.start())
        each_piece(lambda cp: cp.wait())

    idx_copy = pltpu.make_async_copy(dest_hbm.at[pl.ds(pl.multiple_of(i * per, per), per)], idx_ref, idx_sem)
    idx_copy.start()
    idx_copy.wait()

    def body(t, c):
        src = _row(h_ref, t)
        for j in range(TOP_K):
            pltpu.make_async_copy(src, _row(xs_out, idx_ref[t * TOP_K + j]), row_sem).start(priority=j % 2)
        return c

    lax.fori_loop(0, tile, body, 0)
    for _ in range(TOP_K):
        pltpu.make_async_copy(h_ref, xs_out.at[pl.ds(0, tile * ROW_CHUNKS), :], row_sem).wait()


def _dispatch(pad_lo, pad_n, dest_flat, hflat, n_rows):
    t = DISPATCH_TILE
    per = t * TOP_K
    nt = dest_flat.shape[0] // per
    return pl.pallas_call(
        functools.partial(_dispatch_kernel, tile=t),
        out_shape=jax.ShapeDtypeStruct((n_rows * ROW_CHUNKS, LANES), jnp.float32),
        grid_spec=pltpu.PrefetchScalarGridSpec(
            num_scalar_prefetch=2,
            grid=(nt,),
            in_specs=[pl.BlockSpec(memory_space=pl.ANY),
                      pl.BlockSpec((t * ROW_CHUNKS, LANES), lambda i, lo, cnt: (i, 0))],
            out_specs=pl.BlockSpec(memory_space=pl.ANY),
            scratch_shapes=[pltpu.SMEM((per,), jnp.int32),
                            pltpu.VMEM((EXPERT_ROWS * ROW_CHUNKS, LANES), jnp.float32),
                            pltpu.SemaphoreType.DMA,
                            pltpu.SemaphoreType.DMA,
                            pltpu.SemaphoreType.DMA]),
        compiler_params=pltpu.CompilerParams(dimension_semantics=("arbitrary",),
                                             vmem_limit_bytes=VMEM_LIMIT),
        name="moe_dispatch",
    )(pad_lo, pad_n, dest_flat, hflat)


def _expert_kernel(be_ref, used_ref, x_ref, wg_ref, bg_ref, wu_ref, bu_ref, wd_ref, bd_ref, y_ref,
                   wg_bf, wu_bf, wd_bf, *, rows):
    b = pl.program_id(0)
    live = b < used_ref[0]

    @pl.when(jnp.logical_not(live))
    def _():
        y_ref[...] = jnp.zeros(y_ref.shape, y_ref.dtype)

    @pl.when(jnp.logical_and(live, jnp.logical_or(b == 0, be_ref[b] != be_ref[jnp.maximum(b - 1, 0)])))
    def _():
        wg_bf[...] = wg_ref[0, 0].astype(jnp.bfloat16)
        wu_bf[...] = wu_ref[0, 0].astype(jnp.bfloat16)
        wd_bf[...] = wd_ref[0, 0].astype(jnp.bfloat16)

    @pl.when(live)
    def _():
        x = jnp.concatenate([x_ref[pl.ds(c, rows, stride=ROW_CHUNKS), :] for c in range(ROW_CHUNKS)],
                            axis=1).astype(jnp.bfloat16)
        half = D_MODEL // 2
        halves = (slice(0, half), slice(half, D_MODEL))
        gu = [(jnp.dot(x, wg_bf[:, cols], preferred_element_type=jnp.float32) + bg_ref[0, 0, :, cols],
               jnp.dot(x, wu_bf[:, cols], preferred_element_type=jnp.float32) + bu_ref[0, 0, :, cols])
              for cols in halves]
        y = bd_ref[0, 0]
        for (g, u), cols in zip(gu, halves):
            g = jnp.minimum(g, SWIGLU_LIMIT)
            u = jnp.clip(u, -SWIGLU_LIMIT, SWIGLU_LIMIT)
            act = (u + 1.0) * (g * (1.0 / (1.0 + jnp.exp(-SWIGLU_ALPHA * g))))
            y = y + jnp.dot(act.astype(jnp.bfloat16), wd_bf[cols, :], preferred_element_type=jnp.float32)
        for c in range(ROW_CHUNKS):
            y_ref[pl.ds(c, rows, stride=ROW_CHUNKS), :] = y[:, c * LANES:(c + 1) * LANES]


def _experts(block_expert, n_used, xs, wg, bg, wu, bu, wd, bd, *, layer):
    nb = block_expert.shape[0]
    rows = EXPERT_ROWS
    depth = wg.shape[0]
    blk = lambda b, be, used: (jnp.minimum(b, used[0] - 1), 0)
    wsel = lambda b, be, used: (layer, be[b], 0, 0)
    wspec = pl.BlockSpec((1, 1, D_MODEL, D_MODEL), wsel)
    bspec = pl.BlockSpec((1, 1, 1, D_MODEL), wsel)
    bias4 = lambda a: a.reshape(depth, N_EXPERTS, 1, D_MODEL)
    return pl.pallas_call(
        functools.partial(_expert_kernel, rows=rows),
        out_shape=jax.ShapeDtypeStruct(xs.shape, jnp.float32),
        grid_spec=pltpu.PrefetchScalarGridSpec(
            num_scalar_prefetch=2,
            grid=(nb,),
            in_specs=[pl.BlockSpec((rows * ROW_CHUNKS, LANES), blk),
                      wspec, bspec, wspec, bspec, wspec, bspec],
            out_specs=pl.BlockSpec((rows * ROW_CHUNKS, LANES), lambda b, be, used: (b, 0)),
            scratch_shapes=[pltpu.VMEM((D_MODEL, D_MODEL), jnp.bfloat16)] * 3),
        compiler_params=pltpu.CompilerParams(dimension_semantics=("arbitrary",),
                                             vmem_limit_bytes=VMEM_LIMIT),
        name="moe_experts",
    )(block_expert, n_used, xs, wg, bias4(bg), wu, bias4(bu), wd, bias4(bd))


def _combine_kernel(cstart_ref, clen_ref, pchunk_ref, gbound_ref, ys_hbm, x_ref, dest_ref, gate_ref, gf_ref,
                    o_ref, zbuf_ref, acc_ref, destb_ref, gateb_ref, sem, *, tile, final):
    i = pl.program_id(0)
    n_tiles = pl.num_programs(0)
    slab_rows = SLAB_CHUNKS * COMBINE_CHUNK
    chunk_flat = COMBINE_CHUNK * ROW_CHUNKS

    def pair_copies(first_chunk, ring):
        return [pltpu.make_async_copy(
            ys_hbm.at[pl.ds(pl.multiple_of(cstart_ref[first_chunk + q] * ROW_CHUNKS, ROW_CHUNKS), chunk_flat), :],
            zbuf_ref.at[ring, q // SLAB_CHUNKS, pl.ds((q % SLAB_CHUNKS) * chunk_flat, chunk_flat), :],
            sem.at[ring]) for q in range(2 * SLAB_CHUNKS)]

    g0 = gbound_ref[i]
    n_pairs = gbound_ref[i + 1] - g0
    g_total = gbound_ref[n_tiles]

    @pl.when(i == 0)
    def _():
        for cp in pair_copies(pchunk_ref[0], 0):
            cp.start()

        @pl.when(g_total > 1)
        def _():
            for cp in pair_copies(pchunk_ref[1], 1):
                cp.start()

    acc_ref[...] = x_ref[...]
    for j in range(TOP_K):
        destb_ref[j] = jnp.broadcast_to(dest_ref[:, j:j + 1], (tile, LANES))
        gateb_ref[j] = jnp.broadcast_to(gate_ref[:, j:j + 1], (tile, LANES))
    col = lax.broadcasted_iota(jnp.int32, (1, slab_rows), 1)
    col_chunk = col // COMBINE_CHUNK
    col_row = col % COMBINE_CHUNK

    def gate_matrix(first):
        rows = jnp.full((1, slab_rows), -1, jnp.int32)
        for q in range(SLAB_CHUNKS):
            ok = jnp.logical_and(col_chunk == q, col_row < clen_ref[first + q])
            rows = jnp.where(ok, cstart_ref[first + q] + col_row, rows)
        g_cols = []
        for c0 in range(0, slab_rows, LANES):
            rows_c = rows[:, c0:c0 + LANES]
            g_c = jnp.zeros((tile, LANES), jnp.float32)
            for j in range(TOP_K):
                g_c = jnp.where(destb_ref[j] == rows_c, gateb_ref[j], g_c)
            g_cols.append(g_c.astype(jnp.bfloat16))
        return jnp.concatenate(g_cols, axis=1)

    def pair(p, carry):
        g = g0 + p
        ring = g % 2
        first = pchunk_ref[g]
        for cp in pair_copies(first, ring):
            cp.wait()
        ys_pair = [jnp.concatenate([zbuf_ref[ring, slot, pl.ds(c, slab_rows, stride=ROW_CHUNKS), :]
                                    for c in range(ROW_CHUNKS)], axis=1).astype(jnp.bfloat16)
                   for slot in range(2)]

        @pl.when(g + 2 < g_total)
        def _():
            for cp in pair_copies(pchunk_ref[g + 2], ring):
                cp.start()

        for slot in range(2):
            acc_ref[...] += jnp.dot(gate_matrix(first + slot * SLAB_CHUNKS), ys_pair[slot],
                                    preferred_element_type=jnp.float32)
        return carry

    lax.fori_loop(0, n_pairs, pair, 0)
    x2 = acc_ref[...]
    o_ref[...] = _rms(x2, gf_ref[...]) if final else x2


def _combine(chunk_start, chunk_len, pair_chunk, pair_bound, ys, x1, dest_tok, gates_tok, g_final, *, final):
    t = TOK_TILE
    n = x1.shape[0]
    tok = lambda width: pl.BlockSpec((t, width), lambda i, *_: (i, 0))
    return pl.pallas_call(
        functools.partial(_combine_kernel, tile=t, final=final),
        out_shape=jax.ShapeDtypeStruct((n, D_MODEL), jnp.float32),
        grid_spec=pltpu.PrefetchScalarGridSpec(
            num_scalar_prefetch=4,
            grid=(n // t,),
            in_specs=[pl.BlockSpec(memory_space=pl.ANY),
                      tok(D_MODEL), tok(TOP_K), tok(TOP_K),
                      pl.BlockSpec((1, D_MODEL), lambda i, *_: (0, 0))],
            out_specs=tok(D_MODEL),
            scratch_shapes=[pltpu.VMEM((2, 2, SLAB_CHUNKS * COMBINE_CHUNK * ROW_CHUNKS, LANES), jnp.float32),
                            pltpu.VMEM((t, D_MODEL), jnp.float32),
                            pltpu.VMEM((TOP_K, t, LANES), jnp.int32),
                            pltpu.VMEM((TOP_K, t, LANES), jnp.float32),
                            pltpu.SemaphoreType.DMA((2,))]),
        compiler_params=pltpu.CompilerParams(dimension_semantics=("arbitrary",),
                                             vmem_limit_bytes=VMEM_LIMIT),
        name="moe_combine",
    )(chunk_start, chunk_len, pair_chunk, pair_bound, ys, x1, dest_tok, gates_tok, g_final.reshape(1, -1))


def _combine_chunks(base, cnt, seg_start):
    n_run = jnp.concatenate([base[1:], cnt[None]], axis=0) - base
    run_start = seg_start[None, :] + base
    n_chunks = (n_run + COMBINE_CHUNK - 1) // COMBINE_CHUNK
    cum = jnp.cumsum(n_chunks, axis=1)
    total = cum[:, -1]
    slots = jnp.arange(MAX_CHUNKS, dtype=jnp.int32)
    expert = jnp.minimum(jnp.sum(cum[:, None, :] <= slots[None, :, None], axis=2), N_EXPERTS - 1)
    hot = expert[..., None] == jnp.arange(N_EXPERTS, dtype=jnp.int32)
    pick = lambda a: jnp.sum(jnp.where(hot, a[:, None, :], 0), axis=2)
    k = slots[None, :] - (pick(cum) - pick(n_chunks))
    valid = slots[None, :] < total[:, None]
    start = jnp.where(valid, pick(run_start) + COMBINE_CHUNK * k, 0)
    length = jnp.where(valid, jnp.clip(pick(n_run) - COMBINE_CHUNK * k, 0, COMBINE_CHUNK), 0)
    pair_size = 2 * SLAB_CHUNKS
    n_tiles = base.shape[0]
    n_pairs = (total + pair_size - 1) // pair_size
    pair_end = jnp.cumsum(n_pairs)
    pair_bound = jnp.concatenate([jnp.zeros((1,), pair_end.dtype), pair_end])
    g = jnp.arange(n_tiles * (MAX_CHUNKS // pair_size), dtype=jnp.int32)
    tile_of = jnp.minimum(jnp.sum(pair_end[None, :] <= g[:, None], axis=1), n_tiles - 1)
    tile_first = jnp.sum(jnp.where(tile_of[:, None] == jnp.arange(n_tiles), (pair_end - n_pairs)[None, :], 0), axis=1)
    pair_chunk = tile_of * MAX_CHUNKS + pair_size * (g - tile_first)
    i32 = lambda a: a.reshape(-1).astype(jnp.int32)
    return i32(start), i32(length), i32(pair_chunk), i32(pair_bound)


def kernel(x, rel_table, norm1, w_in, lambda_q1, lambda_k1, lambda_q2, lambda_k2, subln_g, w_pool, pool_scale,
           w_out, norm2, w_router, b_router, w_gate, b_gate, w_up, b_up, w_down, b_down, final_norm):
    batch, seq, d = x.shape
    depth = norm1.shape[0]
    n = batch * seq
    assert d == D_MODEL and seq % ATT_TILE == 0 and seq % TOK_TILE == 0
    assert n % DISPATCH_TILE == 0
    bf = jnp.bfloat16

    w_q = w_in[:, :, :QK_WIDTH] * (QK_DIM ** -0.5 * LOG2E)
    w_k = w_in[:, :, QK_WIDTH:2 * QK_WIDTH]
    w_v = w_in[:, :, 2 * QK_WIDTH:2 * QK_WIDTH + ATT_WIDTH]
    w_p = w_in[:, :, 2 * QK_WIDTH + ATT_WIDTH:]
    w_kp = jnp.concatenate([w_k, w_p], axis=2).astype(bf)
    w_qvt = jnp.swapaxes(jnp.concatenate([w_q, w_v], axis=2), 1, 2).astype(bf)
    w_out_b = w_out.astype(bf)
    w_pool_b = w_pool.astype(bf)
    wr_t = jnp.swapaxes(w_router, 1, 2).astype(bf)
    tri = (lax.broadcasted_iota(jnp.int32, (TOK_TILE, TOK_TILE), 0)
           < lax.broadcasted_iota(jnp.int32, (TOK_TILE, TOK_TILE), 1)).astype(bf)

    bias = _bias_tiles(rel_table, ATT_TILE)
    far = jnp.stack([rel_table[N_BUCKETS // 2 - 1], rel_table[N_BUCKETS - 1]], axis=1).reshape(-1)

    n_assign = n * TOP_K
    n_blocks = -(-(n_assign + N_EXPERTS * (EXPERT_ROWS - 1)) // EXPERT_ROWS) + 1
    n_rows = n_blocks * EXPERT_ROWS
    block_start = jnp.arange(n_blocks, dtype=jnp.int32) * EXPERT_ROWS

    x2d = x.reshape(n, d)
    for l in range(depth):
        lam_init = 0.8 - 0.6 * math.exp(-0.3 * l)
        kp, qvt = _norm_proj(x2d, norm1[l], w_kp[l], w_qvt[l])
        att = _attention(kp, qvt, bias, far, lambda_q1[l], lambda_k1[l], lambda_q2[l], lambda_k2[l], subln_g[l],
                         batch=batch, seq=seq, lam_init=lam_init)
        x1, hflat, topi, rank, gates, base, counts = _mix_route(
            att, kp, x2d, w_pool_b[l], pool_scale[l], w_out_b[l], norm2[l], wr_t[l], b_router[l], tri, seq=seq)

        cnt = counts[:, 0].astype(jnp.int32)
        padded = ((cnt + EXPERT_ROWS - 1) // EXPERT_ROWS) * EXPERT_ROWS
        pad_end = jnp.cumsum(padded)
        experts = jnp.arange(N_EXPERTS, dtype=jnp.int32)
        seg_start = pad_end - padded
        dest_tok = (jnp.sum(jnp.where(topi[..., None] == experts, seg_start, 0), axis=-1) + rank).T
        dest_flat = dest_tok.reshape(-1)
        block_expert = jnp.minimum(jnp.sum(pad_end[None, :] <= block_start[:, None], axis=1),
                                   N_EXPERTS - 1).astype(jnp.int32)
        n_used = (pad_end[-1:] // EXPERT_ROWS).astype(jnp.int32)

        pad_lo = jnp.concatenate([pad_end - padded + cnt, pad_end[-1:]])
        pad_n = jnp.concatenate([padded - cnt, (n_rows - pad_end[-1:]) // EXPERT_ROWS])
        xs = _dispatch(pad_lo, pad_n, dest_flat, hflat, n_rows)
        ys = _experts(block_expert, n_used, xs, w_gate, b_gate, w_up, b_up, w_down, b_down, layer=l)
        chunk_lists = _combine_chunks(base[:, :, 0].astype(jnp.int32), cnt, seg_start)
        x2d = _combine(*chunk_lists, ys, x1, dest_tok, gates.T, final_norm,
                       final=(l == depth - 1))
    return x2d.reshape(batch, seq, d)
```

```python
import functools
import math

import jax
import jax.numpy as jnp
from jax import lax
from jax.experimental import pallas as pl
from jax.experimental.pallas import tpu as pltpu

D_MODEL = 1024
N_HEADS = 4
QK_DIM = 64
V_DIM = 128
QK_WIDTH = N_HEADS * 2 * QK_DIM
ATT_WIDTH = N_HEADS * V_DIM
POOL_WINDOWS = (2, 4, 8, 16)
POOL_GROUP = 128
POOL_WIDTH = 512
N_BUCKETS = 32
MAX_DISTANCE = 128
N_EXPERTS = 32
TOP_K = 4
SWIGLU_LIMIT = 7.0
SWIGLU_ALPHA = 1.702
EPS = 1e-5

LANES = 128
BF16_SUBLANES = 16
MXU_DEPTH = 256
ROW_CHUNKS = D_MODEL // LANES
VMEM_LIMIT = 56 * 1024 * 1024

TOK_TILE = 512
ATT_TILE = 512
ATT_CHUNK = 512
BIAS_TILES = 5
NEAR_TILES = 4
V_EXT = V_DIM + BF16_SUBLANES
HALO = BF16_SUBLANES
EXPERT_ROWS = 512
DISPATCH_TILE = 2048
COMBINE_CHUNK = 32
SLAB_CHUNKS = MXU_DEPTH // COMBINE_CHUNK
MAX_CHUNKS = TOK_TILE * TOP_K // COMBINE_CHUNK + N_EXPERTS

_NT = (((1,), (1,)), ((), ()))
LOG2E = math.log2(math.e)


def _rms(x, g):
    return x * lax.rsqrt(jnp.mean(x * x, axis=-1, keepdims=True) + EPS) * g


def _bias_kernel(tab_ref, out_ref, *, tile):
    h = pl.program_id(0)
    d = pl.program_id(1)
    key = lax.broadcasted_iota(jnp.int32, (tile, tile), 0)
    qry = lax.broadcasted_iota(jnp.int32, (tile, tile), 1)
    rel = (d - BIAS_TILES // 2) * tile + key - qry
    nb = N_BUCKETS // 2
    max_exact = nb // 2
    ret = jnp.where(rel > 0, nb, 0)
    n = jnp.abs(rel)
    n_f = jnp.maximum(n, 1).astype(jnp.float32)
    large = max_exact + (jnp.log(n_f / max_exact) / math.log(MAX_DISTANCE / max_exact)
                         * (nb - max_exact)).astype(jnp.int32)
    large = jnp.minimum(large, nb - 1)
    bucket = ret + jnp.where(n < max_exact, n, large)
    acc = jnp.zeros((tile, tile), jnp.float32)
    for b in range(N_BUCKETS):
        acc = jnp.where(bucket == b, tab_ref[b * N_HEADS + h], acc)
    out_ref[0, 0] = acc * LOG2E


def _bias_tiles(rel_table, tile):
    nb, max_exact = N_BUCKETS // 2, N_BUCKETS // 4
    last_bucket_from = max_exact * (MAX_DISTANCE / max_exact) ** ((nb - 1 - max_exact) / (nb - max_exact))
    assert tile + 1 >= last_bucket_from
    return pl.pallas_call(
        functools.partial(_bias_kernel, tile=tile),
        out_shape=jax.ShapeDtypeStruct((N_HEADS, BIAS_TILES, tile, tile), jnp.float32),
        grid=(N_HEADS, BIAS_TILES),
        in_specs=[pl.BlockSpec(memory_space=pltpu.SMEM)],
        out_specs=pl.BlockSpec((1, 1, tile, tile), lambda h, d: (h, d, 0, 0)),
        name="rel_bias_tiles",
    )(rel_table.reshape(-1))


def _norm_proj_kernel(x_ref, g_ref, wkp_ref, wqvt_ref, kp_ref, qvt_ref):
    h = _rms(x_ref[...], g_ref[...]).astype(jnp.bfloat16)
    kp_ref[...] = jnp.dot(h, wkp_ref[...], preferred_element_type=jnp.float32).astype(kp_ref.dtype)
    qvt_ref[...] = lax.dot_general(wqvt_ref[...], h, _NT,
                                   preferred_element_type=jnp.float32).astype(qvt_ref.dtype)


def _norm_proj(x2d, g, w_kp, w_qvt):
    n = x2d.shape[0]
    width = QK_WIDTH + POOL_WIDTH
    width_t = QK_WIDTH + ATT_WIDTH
    return pl.pallas_call(
        _norm_proj_kernel,
        out_shape=(jax.ShapeDtypeStruct((n, width), jnp.bfloat16),
                   jax.ShapeDtypeStruct((width_t, n), jnp.bfloat16)),
        grid=(n // TOK_TILE,),
        in_specs=[pl.BlockSpec((TOK_TILE, D_MODEL), lambda i: (i, 0)),
                  pl.BlockSpec((1, D_MODEL), lambda i: (0, 0)),
                  pl.BlockSpec((D_MODEL, width), lambda i: (0, 0)),
                  pl.BlockSpec((width_t, D_MODEL), lambda i: (0, 0))],
        out_specs=(pl.BlockSpec((TOK_TILE, width), lambda i: (i, 0)),
                   pl.BlockSpec((width_t, TOK_TILE), lambda i: (0, i))),
        compiler_params=pltpu.CompilerParams(dimension_semantics=("parallel",),
                                             vmem_limit_bytes=VMEM_LIMIT),
        name="norm_in_proj",
    )(x2d, g.reshape(1, D_MODEL), w_kp, w_qvt)


def _attn_kernel(far_ref, qt_ref, k_ref, vt_ref, bias_ref, lq1_ref, lk1_ref, lq2_ref, lk2_ref, sg_ref,
                 o_ref, qs_ref, vte_ref, s0_ref, s1_ref, mc0_ref, mc1_ref, m_ref, acc_ref,
                 *, tile, n_kv, lam_init):
    h = pl.program_id(1)
    qi = pl.program_id(2)
    n_far = n_kv - NEAR_TILES

    @pl.when(qi == 0)
    def _():
        vte_ref[:V_DIM, :] = vt_ref[...]
        vte_ref[V_DIM:, :] = jnp.ones((V_EXT - V_DIM, vte_ref.shape[1]), vte_ref.dtype)

    qt = qt_ref[...]
    row = lax.broadcasted_iota(jnp.int32, qt.shape, 0)
    zero = jnp.zeros_like(qt)
    qs_ref[:, :tile] = jnp.where(row < QK_DIM, qt, zero)
    qs_ref[:, tile:] = jnp.where(row >= QK_DIM, qt, zero)

    m_ref[...] = jnp.full(m_ref.shape, -jnp.inf, jnp.float32)
    acc_ref[...] = jnp.zeros(acc_ref.shape, jnp.float32)
    chunks = [slice(c * ATT_CHUNK, (c + 1) * ATT_CHUNK) for c in range(2 * tile // ATT_CHUNK)]
    w0 = jnp.clip(qi - 1, 0, n_kv - NEAR_TILES)
    c_below = far_ref[2 * h] * LOG2E
    c_above = far_ref[2 * h + 1] * LOG2E

    def far_tile(f):
        return jnp.where(f < w0, f, f + NEAR_TILES)

    def far_const(ki):
        return jnp.where(ki < qi, c_below, c_above)

    def logits(ki, cols):
        k = k_ref[pl.ds(pl.multiple_of(ki * tile, tile), tile), :]
        return jnp.dot(k, qs_ref[:, cols], preferred_element_type=jnp.float32)

    def scores_near(ki, cols, s_ref, mc_ref):
        d = jnp.clip(ki - qi, -(BIAS_TILES // 2), BIAS_TILES // 2) + BIAS_TILES // 2
        q0 = cols.start % tile
        s = logits(ki, cols) + bias_ref[0, d, :, q0:q0 + ATT_CHUNK]
        s_ref[:, cols] = s
        mc_ref[:, cols] = jnp.max(s, axis=0, keepdims=True)

    def scores_far(ki, cols, s_ref, mc_ref):
        s = logits(ki, cols)
        s_ref[:, cols] = s
        mc_ref[:, cols] = jnp.max(s, axis=0, keepdims=True) + far_const(ki)

    def update(ki, cols, s_ref, mc_ref, const):
        vt = vte_ref[:, pl.ds(pl.multiple_of(ki * tile, tile), tile)]
        m_prev = m_ref[:, cols]
        m_new = jnp.maximum(m_prev, mc_ref[:, cols])
        alpha = jnp.exp2(m_prev - m_new)
        p = jnp.exp2(s_ref[:, cols] - (m_new - const))
        acc_ref[:, cols] = alpha * acc_ref[:, cols] + jnp.dot(vt, p.astype(jnp.bfloat16),
                                                              preferred_element_type=jnp.float32)
        m_ref[:, cols] = m_new

    bufs = ((s0_ref, mc0_ref), (s1_ref, mc1_ref))
    for cols in chunks:
        scores_near(w0, cols, *bufs[0])
    for i in range(NEAR_TILES):
        for cols in chunks:
            if i + 1 < NEAR_TILES:
                scores_near(w0 + i + 1, cols, *bufs[(i + 1) % 2])
            elif n_far > 0:
                scores_far(far_tile(0), cols, *bufs[(i + 1) % 2])
            update(w0 + i, cols, *bufs[i % 2], 0.0)

    def pair(j, look_ahead=True):
        ka, kb, kc = far_tile(2 * j), far_tile(2 * j + 1), far_tile(2 * j + 2)
        for cols in chunks:
            scores_far(kb, cols, *bufs[(NEAR_TILES + 1) % 2])
            update(ka, cols, *bufs[NEAR_TILES % 2], far_const(ka))
        for cols in chunks:
            if look_ahead:
                scores_far(kc, cols, *bufs[NEAR_TILES % 2])
            update(kb, cols, *bufs[(NEAR_TILES + 1) % 2], far_const(kb))

    if n_far > 0:
        lax.fori_loop(0, n_far // 2 - 1, lambda j, c: (pair(j), c)[1], 0)
        pair(n_far // 2 - 1, look_ahead=False)

    acc = acc_ref[...]
    o = acc[:V_DIM] / acc[V_DIM:V_DIM + 1]
    lam = (jnp.exp(jnp.sum(lq1_ref[...] * lk1_ref[...], axis=1, keepdims=True))
           - jnp.exp(jnp.sum(lq2_ref[...] * lk2_ref[...], axis=1, keepdims=True)) + lam_init)
    o = o[:, :tile] - lam * o[:, tile:]
    o = o * lax.rsqrt(jnp.mean(o * o, axis=0, keepdims=True) + EPS) * (sg_ref[...] * (1.0 - lam_init))
    o_ref[...] = o.T.astype(o_ref.dtype)


def _attention(kp, qvt, bias, far, lq1, lk1, lq2, lk2, sub_g, *, batch, seq, lam_init):
    t = ATT_TILE
    nq = seq // t
    assert nq >= NEAR_TILES and (nq - NEAR_TILES) % 2 == 0
    vrow = QK_WIDTH // LANES
    vec = lambda width: pl.BlockSpec((1, width), lambda b, h, qi: (0, 0))
    return pl.pallas_call(
        functools.partial(_attn_kernel, tile=t, n_kv=nq, lam_init=lam_init),
        out_shape=jax.ShapeDtypeStruct((batch * seq, ATT_WIDTH), jnp.bfloat16),
        grid=(batch, N_HEADS, nq),
        in_specs=[pl.BlockSpec(memory_space=pltpu.SMEM),
                  pl.BlockSpec((LANES, t), lambda b, h, qi: (h, b * nq + qi)),
                  pl.BlockSpec((seq, LANES), lambda b, h, qi: (b, h)),
                  pl.BlockSpec((V_DIM, seq), lambda b, h, qi: (vrow + h, b)),
                  pl.BlockSpec((1, BIAS_TILES, t, t), lambda b, h, qi: (h, 0, 0, 0)),
                  vec(QK_DIM), vec(QK_DIM), vec(QK_DIM), vec(QK_DIM),
                  pl.BlockSpec((V_DIM, 1), lambda b, h, qi: (0, 0))],
        out_specs=pl.BlockSpec((t, V_DIM), lambda b, h, qi: (b * nq + qi, h)),
        scratch_shapes=[pltpu.VMEM((LANES, 2 * t), jnp.bfloat16),
                        pltpu.VMEM((V_EXT, seq), jnp.bfloat16),
                        pltpu.VMEM((t, 2 * t), jnp.float32),
                        pltpu.VMEM((t, 2 * t), jnp.float32),
                        pltpu.VMEM((1, 2 * t), jnp.float32),
                        pltpu.VMEM((1, 2 * t), jnp.float32),
                        pltpu.VMEM((1, 2 * t), jnp.float32),
                        pltpu.VMEM((V_EXT, 2 * t), jnp.float32)],
        compiler_params=pltpu.CompilerParams(dimension_semantics=("arbitrary", "arbitrary", "arbitrary"),
                                             vmem_limit_bytes=VMEM_LIMIT),
        name="diff_attention",
    )(far, qvt, kp, qvt, bias, lq1.reshape(1, -1), lk1.reshape(1, -1), lq2.reshape(1, -1),
      lk2.reshape(1, -1), sub_g.reshape(-1, 1))


def _mix_route_kernel(att_ref, p_ref, prev_ref, next_ref, x_ref, wpool_ref, pscale_ref, wout_ref, g2_ref,
                      wr_ref, br_ref, tri_ref,
                      x1_ref, hflat_ref, topi_ref, rank_ref, gate_ref, base_ref, cnt_out_ref, cnt_ref,
                      *, tile, seq):
    i = pl.program_id(0)
    tiles_per_seq = seq // tile
    si = i % tiles_per_seq

    @pl.when(i == 0)
    def _():
        cnt_ref[...] = jnp.zeros(cnt_ref.shape, jnp.float32)

    pc = p_ref[...].astype(jnp.float32)
    pp = jnp.where(si == 0, 0.0, prev_ref[...].astype(jnp.float32))
    pn = jnp.where(si == tiles_per_seq - 1, 0.0, next_ref[...].astype(jnp.float32))
    pe = jnp.concatenate([pp, pc, pn], axis=0)
    n_ext = tile + 2 * HALO
    pos = si * tile + lax.broadcasted_iota(jnp.int32, (tile, 1), 0)
    pooled = []
    for g, win in enumerate(POOL_WINDOWS):
        xg = pe[:, g * POOL_GROUP:(g + 1) * POOL_GROUP]
        w = xg + pltpu.roll(xg, 1, 0)
        half = 1
        while 2 * half < win:
            w = pltpu.roll(w, half, 0) + pltpu.roll(w, n_ext - half, 0)
            half *= 2
        lo = jnp.maximum(pos - win // 2, 0)
        hi = jnp.minimum(pos - win // 2 + win - 1, seq - 1)
        mean = w[HALO:HALO + tile] / (hi - lo + 1).astype(jnp.float32)
        dg = (mean - pc[:, g * POOL_GROUP:(g + 1) * POOL_GROUP]).astype(jnp.bfloat16)
        pooled.append(jnp.dot(dg, wpool_ref[g], preferred_element_type=jnp.float32))
    pool = jnp.concatenate(pooled, axis=1) * pscale_ref[...]

    mix = jnp.concatenate([att_ref[...], pool.astype(jnp.bfloat16)], axis=1)
    x1 = x_ref[...] + jnp.dot(mix, wout_ref[...], preferred_element_type=jnp.float32)
    x1_ref[...] = x1

    h2 = _rms(x1, g2_ref[...])
    for c in range(ROW_CHUNKS):
        hflat_ref[pl.ds(c, tile, stride=ROW_CHUNKS), :] = h2[:, c * LANES:(c + 1) * LANES]

    logits = lax.dot_general(wr_ref[...], h2.astype(jnp.bfloat16), _NT,
                             preferred_element_type=jnp.float32) + br_ref[...]
    eio = lax.broadcasted_iota(jnp.int32, logits.shape, 0)
    work = logits
    sel = jnp.zeros(logits.shape, jnp.float32)
    top_v, top_i, hot = [], [], []
    for _ in range(TOP_K):
        mx = jnp.max(work, axis=0, keepdims=True)
        idx = jnp.min(jnp.where(work == mx, eio, N_EXPERTS), axis=0, keepdims=True)
        oh = eio == idx
        top_v.append(mx)
        top_i.append(idx)
        hot.append(oh)
        work = jnp.where(oh, -jnp.inf, work)
        sel = sel + oh.astype(jnp.float32)
    ex = [jnp.exp(v - top_v[0]) for v in top_v]
    den = ex[0] + ex[1] + ex[2] + ex[3]
    ahead = jnp.dot(sel.astype(jnp.bfloat16), tri_ref[...], preferred_element_type=jnp.float32) + cnt_ref[...]
    for j in range(TOP_K):
        topi_ref[j:j + 1, :] = top_i[j]
        gate_ref[j:j + 1, :] = ex[j] / den
        rank_ref[j:j + 1, :] = jnp.sum(jnp.where(hot[j], ahead, 0.0), axis=0, keepdims=True).astype(jnp.int32)
    base_ref[0] = jnp.broadcast_to(cnt_ref[...], base_ref.shape[1:])
    cnt_ref[...] = cnt_ref[...] + jnp.sum(sel, axis=1, keepdims=True)
    cnt_out_ref[...] = jnp.broadcast_to(cnt_ref[...], cnt_out_ref.shape)


def _mix_route(att, proj, x2d, wpool, pscale, wout, g2, wr_t, br, tri, *, seq):
    n = x2d.shape[0]
    t = TOK_TILE
    nt = n // t
    pcol = QK_WIDTH // POOL_WIDTH
    hb = t // HALO
    last_halo = n // HALO - 1
    const = lambda shape: pl.BlockSpec(shape, lambda i: (0,) * len(shape))
    out_shape = (jax.ShapeDtypeStruct((n, D_MODEL), jnp.float32),
                 jax.ShapeDtypeStruct((n * ROW_CHUNKS, LANES), jnp.float32),
                 jax.ShapeDtypeStruct((TOP_K, n), jnp.int32),
                 jax.ShapeDtypeStruct((TOP_K, n), jnp.int32),
                 jax.ShapeDtypeStruct((TOP_K, n), jnp.float32),
                 jax.ShapeDtypeStruct((nt, N_EXPERTS, LANES), jnp.float32),
                 jax.ShapeDtypeStruct((N_EXPERTS, LANES), jnp.float32))
    return pl.pallas_call(
        functools.partial(_mix_route_kernel, tile=t, seq=seq),
        out_shape=out_shape,
        grid=(nt,),
        in_specs=[pl.BlockSpec((t, ATT_WIDTH), lambda i: (i, 0)),
                  pl.BlockSpec((t, POOL_WIDTH), lambda i: (i, pcol)),
                  pl.BlockSpec((HALO, POOL_WIDTH), lambda i: (jnp.maximum(i * hb - 1, 0), pcol)),
                  pl.BlockSpec((HALO, POOL_WIDTH), lambda i: (jnp.minimum((i + 1) * hb, last_halo), pcol)),
                  pl.BlockSpec((t, D_MODEL), lambda i: (i, 0)),
                  const((len(POOL_WINDOWS), POOL_GROUP, POOL_GROUP)),
                  const((1, POOL_WIDTH)),
                  const((D_MODEL, D_MODEL)),
                  const((1, D_MODEL)),
                  const((N_EXPERTS, D_MODEL)),
                  const((N_EXPERTS, 1)),
                  const((t, t))],
        out_specs=(pl.BlockSpec((t, D_MODEL), lambda i: (i, 0)),
                   pl.BlockSpec((t * ROW_CHUNKS, LANES), lambda i: (i, 0)),
                   pl.BlockSpec((TOP_K, t), lambda i: (0, i)),
                   pl.BlockSpec((TOP_K, t), lambda i: (0, i)),
                   pl.BlockSpec((TOP_K, t), lambda i: (0, i)),
                   pl.BlockSpec((1, N_EXPERTS, LANES), lambda i: (i, 0, 0)),
                   pl.BlockSpec((N_EXPERTS, LANES), lambda i: (0, 0))),
        scratch_shapes=[pltpu.VMEM((N_EXPERTS, 1), jnp.float32)],
        compiler_params=pltpu.CompilerParams(dimension_semantics=("arbitrary",),
                                             vmem_limit_bytes=VMEM_LIMIT),
        name="mix_out_route",
    )(att, proj, proj, proj, x2d, wpool, pscale.reshape(1, -1), wout, g2.reshape(1, -1), wr_t,
      br.reshape(-1, 1), tri)


def _row(ref, r):
    return ref.at[pl.ds(pl.multiple_of(r * ROW_CHUNKS, ROW_CHUNKS), ROW_CHUNKS), :]


def _dispatch_kernel(pad_lo_ref, pad_n_ref, dest_hbm, h_ref, xs_out, idx_ref, zero_ref, idx_sem, row_sem, pad_sem,
                     *, tile):
    i = pl.program_id(0)
    per = tile * TOP_K

    @pl.when(i == 0)
    def _():
        zero_ref[...] = jnp.zeros(zero_ref.shape, zero_ref.dtype)

        def zero_rows(first_row, n_rows, act):
            dst = xs_out.at[pl.ds(pl.multiple_of(first_row * ROW_CHUNKS, ROW_CHUNKS), n_rows * ROW_CHUNKS), :]
            act(pltpu.make_async_copy(zero_ref.at[pl.ds(0, n_rows * ROW_CHUNKS), :], dst, pad_sem))

        def each_piece(act):
            def per_expert(e, c):
                row, left = pad_lo_ref[e], pad_n_ref[e]
                piece = EXPERT_ROWS // 2
                while piece >= 1:
                    take = left >= piece

                    @pl.when(take)
                    def _(row=row, piece=piece):
                        zero_rows(row, piece, act)

                    row = row + jnp.where(take, piece, 0)
                    left = left - jnp.where(take, piece, 0)
                    piece //= 2
                return c

            lax.fori_loop(0, N_EXPERTS, per_expert, 0)
            lax.fori_loop(0, pad_n_ref[N_EXPERTS],
                          lambda k, c: (zero_rows(pad_lo_ref[N_EXPERTS] + k * EXPERT_ROWS, EXPERT_ROWS, act), c)[1], 0)

        each_piece(lambda cp: cp.start())
        each_piece(lambda cp: cp.wait())

    idx_copy = pltpu.make_async_copy(dest_hbm.at[pl.ds(pl.multiple_of(i * per, per), per)], idx_ref, idx_sem)
    idx_copy.start()
    idx_copy.wait()

    def body(t, c):
        src = _row(h_ref, t)
        for j in range(TOP_K):
            pltpu.make_async_copy(src, _row(xs_out, idx_ref[t * TOP_K + j]), row_sem).start(priority=j % 2)
        return c

    lax.fori_loop(0, tile, body, 0, unroll=8)
    for _ in range(TOP_K):
        pltpu.make_async_copy(h_ref, xs_out.at[pl.ds(0, tile * ROW_CHUNKS), :], row_sem).wait()


def _dispatch(pad_lo, pad_n, dest_flat, hflat, n_rows):
    t = DISPATCH_TILE
    per = t * TOP_K
    nt = dest_flat.shape[0] // per
    return pl.pallas_call(
        functools.partial(_dispatch_kernel, tile=t),
        out_shape=jax.ShapeDtypeStruct((n_rows * ROW_CHUNKS, LANES), jnp.float32),
        grid_spec=pltpu.PrefetchScalarGridSpec(
            num_scalar_prefetch=2,
            grid=(nt,),
            in_specs=[pl.BlockSpec(memory_space=pl.ANY),
                      pl.BlockSpec((t * ROW_CHUNKS, LANES), lambda i, lo, cnt: (i, 0))],
            out_specs=pl.BlockSpec(memory_space=pl.ANY),
            scratch_shapes=[pltpu.SMEM((per,), jnp.int32),
                            pltpu.VMEM((EXPERT_ROWS * ROW_CHUNKS, LANES), jnp.float32),
                            pltpu.SemaphoreType.DMA,
                            pltpu.SemaphoreType.DMA,
                            pltpu.SemaphoreType.DMA]),
        compiler_params=pltpu.CompilerParams(dimension_semantics=("arbitrary",),
                                             vmem_limit_bytes=VMEM_LIMIT),
        name="moe_dispatch",
    )(pad_lo, pad_n, dest_flat, hflat)


def _expert_kernel(be_ref, used_ref, x_ref, wg_ref, bg_ref, wu_ref, bu_ref, wd_ref, bd_ref, y_ref,
                   wg_bf, wu_bf, wd_bf, *, rows):
    b = pl.program_id(0)
    live = b < used_ref[0]

    @pl.when(jnp.logical_not(live))
    def _():
        y_ref[...] = jnp.zeros(y_ref.shape, y_ref.dtype)

    @pl.when(jnp.logical_and(live, jnp.logical_or(b == 0, be_ref[b] != be_ref[jnp.maximum(b - 1, 0)])))
    def _():
        wg_bf[...] = wg_ref[0, 0].astype(jnp.bfloat16)
        wu_bf[...] = wu_ref[0, 0].astype(jnp.bfloat16)
        wd_bf[...] = wd_ref[0, 0].astype(jnp.bfloat16)

    @pl.when(live)
    def _():
        x = jnp.concatenate([x_ref[pl.ds(c, rows, stride=ROW_CHUNKS), :] for c in range(ROW_CHUNKS)],
                            axis=1).astype(jnp.bfloat16)
        half = D_MODEL // 2
        halves = (slice(0, half), slice(half, D_MODEL))
        gu = [(jnp.dot(x, wg_bf[:, cols], preferred_element_type=jnp.float32) + bg_ref[0, 0, :, cols],
               jnp.dot(x, wu_bf[:, cols], preferred_element_type=jnp.float32) + bu_ref[0, 0, :, cols])
              for cols in halves]
        y = bd_ref[0, 0]
        for (g, u), cols in zip(gu, halves):
            g = jnp.minimum(g, SWIGLU_LIMIT)
            u = jnp.clip(u, -SWIGLU_LIMIT, SWIGLU_LIMIT)
            act = (u + 1.0) * (g * (1.0 / (1.0 + jnp.exp(-SWIGLU_ALPHA * g))))
            y = y + jnp.dot(act.astype(jnp.bfloat16), wd_bf[cols, :], preferred_element_type=jnp.float32)
        for c in range(ROW_CHUNKS):
            y_ref[pl.ds(c, rows, stride=ROW_CHUNKS), :] = y[:, c * LANES:(c + 1) * LANES]


def _experts(block_expert, n_used, xs, wg, bg, wu, bu, wd, bd, *, layer):
    nb = block_expert.shape[0]
    rows = EXPERT_ROWS
    depth = wg.shape[0]
    blk = lambda b, be, used: (jnp.minimum(b, used[0] - 1), 0)
    wsel = lambda b, be, used: (layer, be[b], 0, 0)
    wspec = pl.BlockSpec((1, 1, D_MODEL, D_MODEL), wsel)
    bspec = pl.BlockSpec((1, 1, 1, D_MODEL), wsel)
    bias4 = lambda a: a.reshape(depth, N_EXPERTS, 1, D_MODEL)
    return pl.pallas_call(
        functools.partial(_expert_kernel, rows=rows),
        out_shape=jax.ShapeDtypeStruct(xs.shape, jnp.float32),
        grid_spec=pltpu.PrefetchScalarGridSpec(
            num_scalar_prefetch=2,
            grid=(nb,),
            in_specs=[pl.BlockSpec((rows * ROW_CHUNKS, LANES), blk),
                      wspec, bspec, wspec, bspec, wspec, bspec],
            out_specs=pl.BlockSpec((rows * ROW_CHUNKS, LANES), lambda b, be, used: (b, 0)),
            scratch_shapes=[pltpu.VMEM((D_MODEL, D_MODEL), jnp.bfloat16)] * 3),
        compiler_params=pltpu.CompilerParams(dimension_semantics=("arbitrary",),
                                             vmem_limit_bytes=VMEM_LIMIT),
        name="moe_experts",
    )(block_expert, n_used, xs, wg, bias4(bg), wu, bias4(bu), wd, bias4(bd))


def _combine_kernel(cstart_ref, clen_ref, pchunk_ref, gbound_ref, ys_hbm, x_ref, dest_ref, gate_ref, gf_ref,
                    o_ref, zbuf_ref, acc_ref, destb_ref, gateb_ref, sem, *, tile, final):
    i = pl.program_id(0)
    n_tiles = pl.num_programs(0)
    slab_rows = SLAB_CHUNKS * COMBINE_CHUNK
    chunk_flat = COMBINE_CHUNK * ROW_CHUNKS

    def pair_copies(first_chunk, ring):
        return [pltpu.make_async_copy(
            ys_hbm.at[pl.ds(pl.multiple_of(cstart_ref[first_chunk + q] * ROW_CHUNKS, ROW_CHUNKS), chunk_flat), :],
            zbuf_ref.at[ring, q // SLAB_CHUNKS, pl.ds((q % SLAB_CHUNKS) * chunk_flat, chunk_flat), :],
            sem.at[ring]) for q in range(2 * SLAB_CHUNKS)]

    g0 = gbound_ref[i]
    n_pairs = gbound_ref[i + 1] - g0
    g_total = gbound_ref[n_tiles]

    @pl.when(i == 0)
    def _():
        for cp in pair_copies(pchunk_ref[0], 0):
            cp.start()

        @pl.when(g_total > 1)
        def _():
            for cp in pair_copies(pchunk_ref[1], 1):
                cp.start()

    acc_ref[...] = x_ref[...]
    for j in range(TOP_K):
        destb_ref[j] = jnp.broadcast_to(dest_ref[:, j:j + 1], (tile, LANES))
        gateb_ref[j] = jnp.broadcast_to(gate_ref[:, j:j + 1], (tile, LANES))
    col = lax.broadcasted_iota(jnp.int32, (1, slab_rows), 1)
    col_chunk = col // COMBINE_CHUNK
    col_row = col % COMBINE_CHUNK

    def gate_matrix(first):
        rows = jnp.full((1, slab_rows), -1, jnp.int32)
        for q in range(SLAB_CHUNKS):
            ok = jnp.logical_and(col_chunk == q, col_row < clen_ref[first + q])
            rows = jnp.where(ok, cstart_ref[first + q] + col_row, rows)
        g_cols = []
        for c0 in range(0, slab_rows, LANES):
            rows_c = rows[:, c0:c0 + LANES]
            g_c = jnp.zeros((tile, LANES), jnp.float32)
            for j in range(TOP_K):
                g_c = jnp.where(destb_ref[j] == rows_c, gateb_ref[j], g_c)
            g_cols.append(g_c.astype(jnp.bfloat16))
        return jnp.concatenate(g_cols, axis=1)

    def pair(p, carry):
        g = g0 + p
        ring = g % 2
        first = pchunk_ref[g]
        for cp in pair_copies(first, ring):
            cp.wait()
        ys_pair = [jnp.concatenate([zbuf_ref[ring, slot, pl.ds(c, slab_rows, stride=ROW_CHUNKS), :]
                                    for c in range(ROW_CHUNKS)], axis=1).astype(jnp.bfloat16)
                   for slot in range(2)]

        @pl.when(g + 2 < g_total)
        def _():
            for cp in pair_copies(pchunk_ref[g + 2], ring):
                cp.start()

        for slot in range(2):
            acc_ref[...] += jnp.dot(gate_matrix(first + slot * SLAB_CHUNKS), ys_pair[slot],
                                    preferred_element_type=jnp.float32)
        return carry

    lax.fori_loop(0, n_pairs, pair, 0)
    x2 = acc_ref[...]
    o_ref[...] = _rms(x2, gf_ref[...]) if final else x2


def _combine(chunk_start, chunk_len, pair_chunk, pair_bound, ys, x1, dest_tok, gates_tok, g_final, *, final):
    t = TOK_TILE
    n = x1.shape[0]
    tok = lambda width: pl.BlockSpec((t, width), lambda i, *_: (i, 0))
    return pl.pallas_call(
        functools.partial(_combine_kernel, tile=t, final=final),
        out_shape=jax.ShapeDtypeStruct((n, D_MODEL), jnp.float32),
        grid_spec=pltpu.PrefetchScalarGridSpec(
            num_scalar_prefetch=4,
            grid=(n // t,),
            in_specs=[pl.BlockSpec(memory_space=pl.ANY),
                      tok(D_MODEL), tok(TOP_K), tok(TOP_K),
                      pl.BlockSpec((1, D_MODEL), lambda i, *_: (0, 0))],
            out_specs=tok(D_MODEL),
            scratch_shapes=[pltpu.VMEM((2, 2, SLAB_CHUNKS * COMBINE_CHUNK * ROW_CHUNKS, LANES), jnp.float32),
                            pltpu.VMEM((t, D_MODEL), jnp.float32),
                            pltpu.VMEM((TOP_K, t, LANES), jnp.int32),
                            pltpu.VMEM((TOP_K, t, LANES), jnp.float32),
                            pltpu.SemaphoreType.DMA((2,))]),
        compiler_params=pltpu.CompilerParams(dimension_semantics=("arbitrary",),
                                             vmem_limit_bytes=VMEM_LIMIT),
        name="moe_combine",
    )(chunk_start, chunk_len, pair_chunk, pair_bound, ys, x1, dest_tok, gates_tok, g_final.reshape(1, -1))


def _combine_chunks(base, cnt, seg_start):
    n_run = jnp.concatenate([base[1:], cnt[None]], axis=0) - base
    run_start = seg_start[None, :] + base
    n_chunks = (n_run + COMBINE_CHUNK - 1) // COMBINE_CHUNK
    cum = jnp.cumsum(n_chunks, axis=1)
    total = cum[:, -1]
    slots = jnp.arange(MAX_CHUNKS, dtype=jnp.int32)
    expert = jnp.minimum(jnp.sum(cum[:, None, :] <= slots[None, :, None], axis=2), N_EXPERTS - 1)
    hot = expert[..., None] == jnp.arange(N_EXPERTS, dtype=jnp.int32)
    pick = lambda a: jnp.sum(jnp.where(hot, a[:, None, :], 0), axis=2)
    k = slots[None, :] - (pick(cum) - pick(n_chunks))
    valid = slots[None, :] < total[:, None]
    start = jnp.where(valid, pick(run_start) + COMBINE_CHUNK * k, 0)
    length = jnp.where(valid, jnp.clip(pick(n_run) - COMBINE_CHUNK * k, 0, COMBINE_CHUNK), 0)
    pair_size = 2 * SLAB_CHUNKS
    n_tiles = base.shape[0]
    n_pairs = (total + pair_size - 1) // pair_size
    pair_end = jnp.cumsum(n_pairs)
    pair_bound = jnp.concatenate([jnp.zeros((1,), pair_end.dtype), pair_end])
    g = jnp.arange(n_tiles * (MAX_CHUNKS // pair_size), dtype=jnp.int32)
    tile_of = jnp.minimum(jnp.sum(pair_end[None, :] <= g[:, None], axis=1), n_tiles - 1)
    tile_first = jnp.sum(jnp.where(tile_of[:, None] == jnp.arange(n_tiles), (pair_end - n_pairs)[None, :], 0), axis=1)
    pair_chunk = tile_of * MAX_CHUNKS + pair_size * (g - tile_first)
    i32 = lambda a: a.reshape(-1).astype(jnp.int32)
    return i32(start), i32(length), i32(pair_chunk), i32(pair_bound)


def kernel(x, rel_table, norm1, w_in, lambda_q1, lambda_k1, lambda_q2, lambda_k2, subln_g, w_pool, pool_scale,
           w_out, norm2, w_router, b_router, w_gate, b_gate, w_up, b_up, w_down, b_down, final_norm):
    batch, seq, d = x.shape
    depth = norm1.shape[0]
    n = batch * seq
    assert d == D_MODEL and seq % ATT_TILE == 0 and seq % TOK_TILE == 0
    assert n % DISPATCH_TILE == 0
    bf = jnp.bfloat16

    w_q = w_in[:, :, :QK_WIDTH] * (QK_DIM ** -0.5 * LOG2E)
    w_k = w_in[:, :, QK_WIDTH:2 * QK_WIDTH]
    w_v = w_in[:, :, 2 * QK_WIDTH:2 * QK_WIDTH + ATT_WIDTH]
    w_p = w_in[:, :, 2 * QK_WIDTH + ATT_WIDTH:]
    w_kp = jnp.concatenate([w_k, w_p], axis=2).astype(bf)
    w_qvt = jnp.swapaxes(jnp.concatenate([w_q, w_v], axis=2), 1, 2).astype(bf)
    w_out_b = w_out.astype(bf)
    w_pool_b = w_pool.astype(bf)
    wr_t = jnp.swapaxes(w_router, 1, 2).astype(bf)
    tri = (lax.broadcasted_iota(jnp.int32, (TOK_TILE, TOK_TILE), 0)
           < lax.broadcasted_iota(jnp.int32, (TOK_TILE, TOK_TILE), 1)).astype(bf)

    bias = _bias_tiles(rel_table, ATT_TILE)
    far = jnp.stack([rel_table[N_BUCKETS // 2 - 1], rel_table[N_BUCKETS - 1]], axis=1).reshape(-1)

    n_assign = n * TOP_K
    n_blocks = -(-(n_assign + N_EXPERTS * (EXPERT_ROWS - 1)) // EXPERT_ROWS) + 1
    n_rows = n_blocks * EXPERT_ROWS
    block_start = jnp.arange(n_blocks, dtype=jnp.int32) * EXPERT_ROWS

    x2d = x.reshape(n, d)
    for l in range(depth):
        lam_init = 0.8 - 0.6 * math.exp(-0.3 * l)
        kp, qvt = _norm_proj(x2d, norm1[l], w_kp[l], w_qvt[l])
        att = _attention(kp, qvt, bias, far, lambda_q1[l], lambda_k1[l], lambda_q2[l], lambda_k2[l], subln_g[l],
                         batch=batch, seq=seq, lam_init=lam_init)
        x1, hflat, topi, rank, gates, base, counts = _mix_route(
            att, kp, x2d, w_pool_b[l], pool_scale[l], w_out_b[l], norm2[l], wr_t[l], b_router[l], tri, seq=seq)

        cnt = counts[:, 0].astype(jnp.int32)
        padded = ((cnt + EXPERT_ROWS - 1) // EXPERT_ROWS) * EXPERT_ROWS
        pad_end = jnp.cumsum(padded)
        experts = jnp.arange(N_EXPERTS, dtype=jnp.int32)
        seg_start = pad_end - padded
        dest_tok = (jnp.sum(jnp.where(topi[..., None] == experts, seg_start, 0), axis=-1) + rank).T
        dest_flat = dest_tok.reshape(-1)
        block_expert = jnp.minimum(jnp.sum(pad_end[None, :] <= block_start[:, None], axis=1),
                                   N_EXPERTS - 1).astype(jnp.int32)
        n_used = (pad_end[-1:] // EXPERT_ROWS).astype(jnp.int32)

        pad_lo = jnp.concatenate([pad_end - padded + cnt, pad_end[-1:]])
        pad_n = jnp.concatenate([padded - cnt, (n_rows - pad_end[-1:]) // EXPERT_ROWS])
        xs = _dispatch(pad_lo, pad_n, dest_flat, hflat, n_rows)
        ys = _experts(block_expert, n_used, xs, w_gate, b_gate, w_up, b_up, w_down, b_down, layer=l)
        chunk_lists = _combine_chunks(base[:, :, 0].astype(jnp.int32), cnt, seg_start)
        x2d = _combine(*chunk_lists, ys, x1, dest_tok, gates.T, final_norm,
                       final=(l == depth - 1))
    return x2d.reshape(batch, seq, d)
```

```python
import functools
import math

import jax
import jax.numpy as jnp
from jax import lax
from jax.experimental import pallas as pl
from jax.experimental.pallas import tpu as pltpu

D_MODEL = 1024
N_HEADS = 4
QK_DIM = 64
V_DIM = 128
QK_WIDTH = N_HEADS * 2 * QK_DIM
ATT_WIDTH = N_HEADS * V_DIM
POOL_WINDOWS = (2, 4, 8, 16)
POOL_GROUP = 128
POOL_WIDTH = 512
N_BUCKETS = 32
MAX_DISTANCE = 128
N_EXPERTS = 32
TOP_K = 4
SWIGLU_LIMIT = 7.0
SWIGLU_ALPHA = 1.702
EPS = 1e-5

LANES = 128
BF16_SUBLANES = 16
MXU_DEPTH = 256
ROW_CHUNKS = D_MODEL // LANES
VMEM_LIMIT = 56 * 1024 * 1024

TOK_TILE = 512
ATT_TILE = 512
ATT_CHUNK = 512
BIAS_TILES = 5
NEAR_TILES = 4
V_EXT = V_DIM + BF16_SUBLANES
HALO = BF16_SUBLANES
EXPERT_ROWS = 512
DISPATCH_TILE = 2048
COMBINE_CHUNK = 32
SLAB_CHUNKS = MXU_DEPTH // COMBINE_CHUNK
MAX_CHUNKS = TOK_TILE * TOP_K // COMBINE_CHUNK + N_EXPERTS

_NT = (((1,), (1,)), ((), ()))
LOG2E = math.log2(math.e)


def _rms(x, g):
    return x * lax.rsqrt(jnp.mean(x * x, axis=-1, keepdims=True) + EPS) * g


def _bias_kernel(tab_ref, out_ref, *, tile):
    h = pl.program_id(0)
    d = pl.program_id(1)
    key = lax.broadcasted_iota(jnp.int32, (tile, tile), 0)
    qry = lax.broadcasted_iota(jnp.int32, (tile, tile), 1)
    rel = (d - BIAS_TILES // 2) * tile + key - qry
    nb = N_BUCKETS // 2
    max_exact = nb // 2
    ret = jnp.where(rel > 0, nb, 0)
    n = jnp.abs(rel)
    n_f = jnp.maximum(n, 1).astype(jnp.float32)
    large = max_exact + (jnp.log(n_f / max_exact) / math.log(MAX_DISTANCE / max_exact)
                         * (nb - max_exact)).astype(jnp.int32)
    large = jnp.minimum(large, nb - 1)
    bucket = ret + jnp.where(n < max_exact, n, large)
    acc = jnp.zeros((tile, tile), jnp.float32)
    for b in range(N_BUCKETS):
        acc = jnp.where(bucket == b, tab_ref[b * N_HEADS + h], acc)
    out_ref[0, 0] = acc * LOG2E


def _bias_tiles(rel_table, tile):
    nb, max_exact = N_BUCKETS // 2, N_BUCKETS // 4
    last_bucket_from = max_exact * (MAX_DISTANCE / max_exact) ** ((nb - 1 - max_exact) / (nb - max_exact))
    assert tile + 1 >= last_bucket_from
    return pl.pallas_call(
        functools.partial(_bias_kernel, tile=tile),
        out_shape=jax.ShapeDtypeStruct((N_HEADS, BIAS_TILES, tile, tile), jnp.float32),
        grid=(N_HEADS, BIAS_TILES),
        in_specs=[pl.BlockSpec(memory_space=pltpu.SMEM)],
        out_specs=pl.BlockSpec((1, 1, tile, tile), lambda h, d: (h, d, 0, 0)),
        name="rel_bias_tiles",
    )(rel_table.reshape(-1))


def _norm_proj_kernel(x_ref, g_ref, wkp_ref, wqvt_ref, kp_ref, qvt_ref):
    h = _rms(x_ref[...], g_ref[...]).astype(jnp.bfloat16)
    kp_ref[...] = jnp.dot(h, wkp_ref[...], preferred_element_type=jnp.float32).astype(kp_ref.dtype)
    qvt_ref[...] = lax.dot_general(wqvt_ref[...], h, _NT,
                                   preferred_element_type=jnp.float32).astype(qvt_ref.dtype)


def _norm_proj(x2d, g, w_kp, w_qvt):
    n = x2d.shape[0]
    width = QK_WIDTH + POOL_WIDTH
    width_t = QK_WIDTH + ATT_WIDTH
    return pl.pallas_call(
        _norm_proj_kernel,
        out_shape=(jax.ShapeDtypeStruct((n, width), jnp.bfloat16),
                   jax.ShapeDtypeStruct((width_t, n), jnp.bfloat16)),
        grid=(n // TOK_TILE,),
        in_specs=[pl.BlockSpec((TOK_TILE, D_MODEL), lambda i: (i, 0)),
                  pl.BlockSpec((1, D_MODEL), lambda i: (0, 0)),
                  pl.BlockSpec((D_MODEL, width), lambda i: (0, 0)),
                  pl.BlockSpec((width_t, D_MODEL), lambda i: (0, 0))],
        out_specs=(pl.BlockSpec((TOK_TILE, width), lambda i: (i, 0)),
                   pl.BlockSpec((width_t, TOK_TILE), lambda i: (0, i))),
        compiler_params=pltpu.CompilerParams(dimension_semantics=("parallel",),
                                             vmem_limit_bytes=VMEM_LIMIT),
        name="norm_in_proj",
    )(x2d, g.reshape(1, D_MODEL), w_kp, w_qvt)


def _attn_kernel(far_ref, qt_ref, k_ref, vt_ref, bias_ref, lq1_ref, lk1_ref, lq2_ref, lk2_ref, sg_ref,
                 o_ref, qs_ref, vte_ref, s0_ref, s1_ref, mc0_ref, mc1_ref, m_ref, acc_ref,
                 *, tile, n_kv, lam_init):
    h = pl.program_id(1)
    qi = pl.program_id(2)
    n_far = n_kv - NEAR_TILES

    @pl.when(qi == 0)
    def _():
        vte_ref[:V_DIM, :] = vt_ref[...]
        vte_ref[V_DIM:, :] = jnp.ones((V_EXT - V_DIM, vte_ref.shape[1]), vte_ref.dtype)

    qt = qt_ref[...]
    row = lax.broadcasted_iota(jnp.int32, qt.shape, 0)
    zero = jnp.zeros_like(qt)
    qs_ref[:, :tile] = jnp.where(row < QK_DIM, qt, zero)
    qs_ref[:, tile:] = jnp.where(row >= QK_DIM, qt, zero)

    m_ref[...] = jnp.full(m_ref.shape, -jnp.inf, jnp.float32)
    acc_ref[...] = jnp.zeros(acc_ref.shape, jnp.float32)
    chunks = [slice(c * ATT_CHUNK, (c + 1) * ATT_CHUNK) for c in range(2 * tile // ATT_CHUNK)]
    w0 = jnp.clip(qi - 1, 0, n_kv - NEAR_TILES)
    c_below = far_ref[2 * h] * LOG2E
    c_above = far_ref[2 * h + 1] * LOG2E

    def far_tile(f):
        return jnp.where(f < w0, f, f + NEAR_TILES)

    def far_const(ki):
        return jnp.where(ki < qi, c_below, c_above)

    def logits(ki, cols):
        k = k_ref[pl.ds(pl.multiple_of(ki * tile, tile), tile), :]
        return jnp.dot(k, qs_ref[:, cols], preferred_element_type=jnp.float32)

    def scores_near(ki, cols, s_ref, mc_ref):
        d = jnp.clip(ki - qi, -(BIAS_TILES // 2), BIAS_TILES // 2) + BIAS_TILES // 2
        q0 = cols.start % tile
        s = logits(ki, cols) + bias_ref[0, d, :, q0:q0 + ATT_CHUNK]
        s_ref[:, cols] = s
        mc_ref[:, cols] = jnp.max(s, axis=0, keepdims=True)

    def scores_far(ki, cols, s_ref, mc_ref):
        s = logits(ki, cols)
        s_ref[:, cols] = s
        mc_ref[:, cols] = jnp.max(s, axis=0, keepdims=True) + far_const(ki)

    def update(ki, cols, s_ref, mc_ref, const):
        vt = vte_ref[:, pl.ds(pl.multiple_of(ki * tile, tile), tile)]
        m_prev = m_ref[:, cols]
        m_new = jnp.maximum(m_prev, mc_ref[:, cols])
        alpha = jnp.exp2(m_prev - m_new)
        p = jnp.exp2(s_ref[:, cols] - (m_new - const))
        acc_ref[:, cols] = alpha * acc_ref[:, cols] + jnp.dot(vt, p.astype(jnp.bfloat16),
                                                              preferred_element_type=jnp.float32)
        m_ref[:, cols] = m_new

    bufs = ((s0_ref, mc0_ref), (s1_ref, mc1_ref))
    for cols in chunks:
        scores_near(w0, cols, *bufs[0])
    for i in range(NEAR_TILES):
        for cols in chunks:
            if i + 1 < NEAR_TILES:
                scores_near(w0 + i + 1, cols, *bufs[(i + 1) % 2])
            elif n_far > 0:
                scores_far(far_tile(0), cols, *bufs[(i + 1) % 2])
            update(w0 + i, cols, *bufs[i % 2], 0.0)

    def pair(j, look_ahead=True):
        ka, kb, kc = far_tile(2 * j), far_tile(2 * j + 1), far_tile(2 * j + 2)
        for cols in chunks:
            scores_far(kb, cols, *bufs[(NEAR_TILES + 1) % 2])
            update(ka, cols, *bufs[NEAR_TILES % 2], far_const(ka))
        for cols in chunks:
            if look_ahead:
                scores_far(kc, cols, *bufs[NEAR_TILES % 2])
            update(kb, cols, *bufs[(NEAR_TILES + 1) % 2], far_const(kb))

    if n_far > 0:
        lax.fori_loop(0, n_far // 2 - 1, lambda j, c: (pair(j), c)[1], 0)
        pair(n_far // 2 - 1, look_ahead=False)

    acc = acc_ref[...]
    o = acc[:V_DIM] / acc[V_DIM:V_DIM + 1]
    lam = (jnp.exp(jnp.sum(lq1_ref[...] * lk1_ref[...], axis=1, keepdims=True))
           - jnp.exp(jnp.sum(lq2_ref[...] * lk2_ref[...], axis=1, keepdims=True)) + lam_init)
    o = o[:, :tile] - lam * o[:, tile:]
    o = o * lax.rsqrt(jnp.mean(o * o, axis=0, keepdims=True) + EPS) * (sg_ref[...] * (1.0 - lam_init))
    o_ref[...] = o.T.astype(o_ref.dtype)


def _attention(kp, qvt, bias, far, lq1, lk1, lq2, lk2, sub_g, *, batch, seq, lam_init):
    t = ATT_TILE
    nq = seq // t
    assert nq >= NEAR_TILES and (nq - NEAR_TILES) % 2 == 0
    vrow = QK_WIDTH // LANES
    vec = lambda width: pl.BlockSpec((1, width), lambda b, h, qi: (0, 0))
    return pl.pallas_call(
        functools.partial(_attn_kernel, tile=t, n_kv=nq, lam_init=lam_init),
        out_shape=jax.ShapeDtypeStruct((batch * seq, ATT_WIDTH), jnp.bfloat16),
        grid=(batch, N_HEADS, nq),
        in_specs=[pl.BlockSpec(memory_space=pltpu.SMEM),
                  pl.BlockSpec((LANES, t), lambda b, h, qi: (h, b * nq + qi)),
                  pl.BlockSpec((seq, LANES), lambda b, h, qi: (b, h)),
                  pl.BlockSpec((V_DIM, seq), lambda b, h, qi: (vrow + h, b)),
                  pl.BlockSpec((1, BIAS_TILES, t, t), lambda b, h, qi: (h, 0, 0, 0)),
                  vec(QK_DIM), vec(QK_DIM), vec(QK_DIM), vec(QK_DIM),
                  pl.BlockSpec((V_DIM, 1), lambda b, h, qi: (0, 0))],
        out_specs=pl.BlockSpec((t, V_DIM), lambda b, h, qi: (b * nq + qi, h)),
        scratch_shapes=[pltpu.VMEM((LANES, 2 * t), jnp.bfloat16),
                        pltpu.VMEM((V_EXT, seq), jnp.bfloat16),
                        pltpu.VMEM((t, 2 * t), jnp.float32),
                        pltpu.VMEM((t, 2 * t), jnp.float32),
                        pltpu.VMEM((1, 2 * t), jnp.float32),
                        pltpu.VMEM((1, 2 * t), jnp.float32),
                        pltpu.VMEM((1, 2 * t), jnp.float32),
                        pltpu.VMEM((V_EXT, 2 * t), jnp.float32)],
        compiler_params=pltpu.CompilerParams(dimension_semantics=("arbitrary", "arbitrary", "arbitrary"),
                                             vmem_limit_bytes=VMEM_LIMIT),
        name="diff_attention",
    )(far, qvt, kp, qvt, bias, lq1.reshape(1, -1), lk1.reshape(1, -1), lq2.reshape(1, -1),
      lk2.reshape(1, -1), sub_g.reshape(-1, 1))


def _mix_route_kernel(att_ref, p_ref, prev_ref, next_ref, x_ref, wpool_ref, pscale_ref, wout_ref, g2_ref,
                      wr_ref, br_ref, tri_ref,
                      x1_ref, hflat_ref, topi_ref, rank_ref, gate_ref, base_ref, cnt_out_ref, cnt_ref,
                      *, tile, seq):
    i = pl.program_id(0)
    tiles_per_seq = seq // tile
    si = i % tiles_per_seq

    @pl.when(i == 0)
    def _():
        cnt_ref[...] = jnp.zeros(cnt_ref.shape, jnp.float32)

    pc = p_ref[...].astype(jnp.float32)
    pp = jnp.where(si == 0, 0.0, prev_ref[...].astype(jnp.float32))
    pn = jnp.where(si == tiles_per_seq - 1, 0.0, next_ref[...].astype(jnp.float32))
    pe = jnp.concatenate([pp, pc, pn], axis=0)
    n_ext = tile + 2 * HALO
    pos = si * tile + lax.broadcasted_iota(jnp.int32, (tile, 1), 0)
    pooled = []
    for g, win in enumerate(POOL_WINDOWS):
        xg = pe[:, g * POOL_GROUP:(g + 1) * POOL_GROUP]
        w = xg + pltpu.roll(xg, 1, 0)
        half = 1
        while 2 * half < win:
            w = pltpu.roll(w, half, 0) + pltpu.roll(w, n_ext - half, 0)
            half *= 2
        lo = jnp.maximum(pos - win // 2, 0)
        hi = jnp.minimum(pos - win // 2 + win - 1, seq - 1)
        mean = w[HALO:HALO + tile] / (hi - lo + 1).astype(jnp.float32)
        dg = (mean - pc[:, g * POOL_GROUP:(g + 1) * POOL_GROUP]).astype(jnp.bfloat16)
        pooled.append(jnp.dot(dg, wpool_ref[g], preferred_element_type=jnp.float32))
    pool = jnp.concatenate(pooled, axis=1) * pscale_ref[...]

    mix = jnp.concatenate([att_ref[...], pool.astype(jnp.bfloat16)], axis=1)
    x1 = x_ref[...] + jnp.dot(mix, wout_ref[...], preferred_element_type=jnp.float32)
    x1_ref[...] = x1

    h2 = _rms(x1, g2_ref[...])
    for c in range(ROW_CHUNKS):
        hflat_ref[pl.ds(c, tile, stride=ROW_CHUNKS), :] = h2[:, c * LANES:(c + 1) * LANES]

    logits = lax.dot_general(wr_ref[...], h2.astype(jnp.bfloat16), _NT,
                             preferred_element_type=jnp.float32) + br_ref[...]
    eio = lax.broadcasted_iota(jnp.int32, logits.shape, 0)
    work = logits
    sel = jnp.zeros(logits.shape, jnp.float32)
    top_v, top_i, hot = [], [], []
    for _ in range(TOP_K):
        mx = jnp.max(work, axis=0, keepdims=True)
        idx = jnp.min(jnp.where(work == mx, eio, N_EXPERTS), axis=0, keepdims=True)
        oh = eio == idx
        top_v.append(mx)
        top_i.append(idx)
        hot.append(oh)
        work = jnp.where(oh, -jnp.inf, work)
        sel = sel + oh.astype(jnp.float32)
    ex = [jnp.exp(v - top_v[0]) for v in top_v]
    den = ex[0] + ex[1] + ex[2] + ex[3]
    ahead = jnp.dot(sel.astype(jnp.bfloat16), tri_ref[...], preferred_element_type=jnp.float32) + cnt_ref[...]
    for j in range(TOP_K):
        topi_ref[j:j + 1, :] = top_i[j]
        gate_ref[j:j + 1, :] = ex[j] / den
        rank_ref[j:j + 1, :] = jnp.sum(jnp.where(hot[j], ahead, 0.0), axis=0, keepdims=True).astype(jnp.int32)
    base_ref[0] = jnp.broadcast_to(cnt_ref[...], base_ref.shape[1:])
    cnt_ref[...] = cnt_ref[...] + jnp.sum(sel, axis=1, keepdims=True)
    cnt_out_ref[...] = jnp.broadcast_to(cnt_ref[...], cnt_out_ref.shape)


def _mix_route(att, proj, x2d, wpool, pscale, wout, g2, wr_t, br, tri, *, seq):
    n = x2d.shape[0]
    t = TOK_TILE
    nt = n // t
    pcol = QK_WIDTH // POOL_WIDTH
    hb = t // HALO
    last_halo = n // HALO - 1
    const = lambda shape: pl.BlockSpec(shape, lambda i: (0,) * len(shape))
    out_shape = (jax.ShapeDtypeStruct((n, D_MODEL), jnp.float32),
                 jax.ShapeDtypeStruct((n * ROW_CHUNKS, LANES), jnp.float32),
                 jax.ShapeDtypeStruct((TOP_K, n), jnp.int32),
                 jax.ShapeDtypeStruct((TOP_K, n), jnp.int32),
                 jax.ShapeDtypeStruct((TOP_K, n), jnp.float32),
                 jax.ShapeDtypeStruct((nt, N_EXPERTS, LANES), jnp.float32),
                 jax.ShapeDtypeStruct((N_EXPERTS, LANES), jnp.float32))
    return pl.pallas_call(
        functools.partial(_mix_route_kernel, tile=t, seq=seq),
        out_shape=out_shape,
        grid=(nt,),
        in_specs=[pl.BlockSpec((t, ATT_WIDTH), lambda i: (i, 0)),
                  pl.BlockSpec((t, POOL_WIDTH), lambda i: (i, pcol)),
                  pl.BlockSpec((HALO, POOL_WIDTH), lambda i: (jnp.maximum(i * hb - 1, 0), pcol)),
                  pl.BlockSpec((HALO, POOL_WIDTH), lambda i: (jnp.minimum((i + 1) * hb, last_halo), pcol)),
                  pl.BlockSpec((t, D_MODEL), lambda i: (i, 0)),
                  const((len(POOL_WINDOWS), POOL_GROUP, POOL_GROUP)),
                  const((1, POOL_WIDTH)),
                  const((D_MODEL, D_MODEL)),
                  const((1, D_MODEL)),
                  const((N_EXPERTS, D_MODEL)),
                  const((N_EXPERTS, 1)),
                  const((t, t))],
        out_specs=(pl.BlockSpec((t, D_MODEL), lambda i: (i, 0)),
                   pl.BlockSpec((t * ROW_CHUNKS, LANES), lambda i: (i, 0)),
                   pl.BlockSpec((TOP_K, t), lambda i: (0, i)),
                   pl.BlockSpec((TOP_K, t), lambda i: (0, i)),
                   pl.BlockSpec((TOP_K, t), lambda i: (0, i)),
                   pl.BlockSpec((1, N_EXPERTS, LANES), lambda i: (i, 0, 0)),
                   pl.BlockSpec((N_EXPERTS, LANES), lambda i: (0, 0))),
        scratch_shapes=[pltpu.VMEM((N_EXPERTS, 1), jnp.float32)],
        compiler_params=pltpu.CompilerParams(dimension_semantics=("arbitrary",),
                                             vmem_limit_bytes=VMEM_LIMIT),
        name="mix_out_route",
    )(att, proj, proj, proj, x2d, wpool, pscale.reshape(1, -1), wout, g2.reshape(1, -1), wr_t,
      br.reshape(-1, 1), tri)


def _row(ref, r):
    return ref.at[pl.ds(pl.multiple_of(r * ROW_CHUNKS, ROW_CHUNKS), ROW_CHUNKS), :]


def _dispatch_kernel(pad_lo_ref, pad_n_ref, dest_hbm, h_ref, xs_out, idx_ref, zero_ref, idx_sem, row_sem, pad_sem,
                     *, tile):
    i = pl.program_id(0)
    per = tile * TOP_K

    @pl.when(i == 0)
    def _():
        zero_ref[...] = jnp.zeros(zero_ref.shape, zero_ref.dtype)

        def zero_rows(first_row, n_rows, act):
            dst = xs_out.at[pl.ds(pl.multiple_of(first_row * ROW_CHUNKS, ROW_CHUNKS), n_rows * ROW_CHUNKS), :]
            act(pltpu.make_async_copy(zero_ref.at[pl.ds(0, n_rows * ROW_CHUNKS), :], dst, pad_sem))

        def each_piece(act):
            def per_expert(e, c):
                row, left = pad_lo_ref[e], pad_n_ref[e]
                piece = EXPERT_ROWS // 2
                while piece >= 1:
                    take = left >= piece

                    @pl.when(take)
                    def _(row=row, piece=piece):
                        zero_rows(row, piece, act)

                    row = row + jnp.where(take, piece, 0)
                    left = left - jnp.where(take, piece, 0)
                    piece //= 2
                return c

            lax.fori_loop(0, N_EXPERTS, per_expert, 0)
            lax.fori_loop(0, pad_n_ref[N_EXPERTS],
                          lambda k, c: (zero_rows(pad_lo_ref[N_EXPERTS] + k * EXPERT_ROWS, EXPERT_ROWS, act), c)[1], 0)

        each_piece(lambda cp: cp.start())
        each_piece(lambda cp: cp.wait())

    idx_copy = pltpu.make_async_copy(dest_hbm.at[pl.ds(pl.multiple_of(i * per, per), per)], idx_ref, idx_sem)
    idx_copy.start()
    idx_copy.wait()

    def body(t, c):
        src = _row(h_ref, t)
        for j in range(TOP_K):
            pltpu.make_async_copy(src, _row(xs_out, idx_ref[t * TOP_K + j]), row_sem).start(priority=j % 2)
        return c

    lax.fori_loop(0, tile, body, 0, unroll=8)
    for _ in range(TOP_K):
        pltpu.make_async_copy(h_ref, xs_out.at[pl.ds(0, tile * ROW_CHUNKS), :], row_sem).wait()


def _dispatch(pad_lo, pad_n, dest_flat, hflat, n_rows):
    t = DISPATCH_TILE
    per = t * TOP_K
    nt = dest_flat.shape[0] // per
    return pl.pallas_call(
        functools.partial(_dispatch_kernel, tile=t),
        out_shape=jax.ShapeDtypeStruct((n_rows * ROW_CHUNKS, LANES), jnp.float32),
        grid_spec=pltpu.PrefetchScalarGridSpec(
            num_scalar_prefetch=2,
            grid=(nt,),
            in_specs=[pl.BlockSpec(memory_space=pl.ANY),
                      pl.BlockSpec((t * ROW_CHUNKS, LANES), lambda i, lo, cnt: (i, 0))],
            out_specs=pl.BlockSpec(memory_space=pl.ANY),
            scratch_shapes=[pltpu.SMEM((per,), jnp.int32),
                            pltpu.VMEM((EXPERT_ROWS * ROW_CHUNKS, LANES), jnp.float32),
                            pltpu.SemaphoreType.DMA,
                            pltpu.SemaphoreType.DMA,
                            pltpu.SemaphoreType.DMA]),
        compiler_params=pltpu.CompilerParams(dimension_semantics=("arbitrary",),
                                             vmem_limit_bytes=VMEM_LIMIT),
        name="moe_dispatch",
    )(pad_lo, pad_n, dest_flat, hflat)


def _expert_kernel(be_ref, used_ref, x_ref, wg_ref, bg_ref, wu_ref, bu_ref, wd_ref, bd_ref, y_ref,
                   wg_bf, wu_bf, wd_bf, *, rows):
    b = pl.program_id(0)
    live = b < used_ref[0]

    @pl.when(jnp.logical_not(live))
    def _():
        y_ref[...] = jnp.zeros(y_ref.shape, y_ref.dtype)

    @pl.when(jnp.logical_and(live, jnp.logical_or(b == 0, be_ref[b] != be_ref[jnp.maximum(b - 1, 0)])))
    def _():
        wg_bf[...] = wg_ref[0, 0].astype(jnp.bfloat16)
        wu_bf[...] = wu_ref[0, 0].astype(jnp.bfloat16)
        wd_bf[...] = wd_ref[0, 0].astype(jnp.bfloat16)

    @pl.when(live)
    def _():
        x = jnp.concatenate([x_ref[pl.ds(c, rows, stride=ROW_CHUNKS), :] for c in range(ROW_CHUNKS)],
                            axis=1).astype(jnp.bfloat16)
        half = D_MODEL // 2
        halves = (slice(0, half), slice(half, D_MODEL))
        gu = [(jnp.dot(x, wg_bf[:, cols], preferred_element_type=jnp.float32) + bg_ref[0, 0, :, cols],
               jnp.dot(x, wu_bf[:, cols], preferred_element_type=jnp.float32) + bu_ref[0, 0, :, cols])
              for cols in halves]
        y = bd_ref[0, 0]
        for (g, u), cols in zip(gu, halves):
            g = jnp.minimum(g, SWIGLU_LIMIT)
            u = jnp.clip(u, -SWIGLU_LIMIT, SWIGLU_LIMIT)
            act = (u + 1.0) * (g * (1.0 / (1.0 + jnp.exp(-SWIGLU_ALPHA * g))))
            y = y + jnp.dot(act.astype(jnp.bfloat16), wd_bf[cols, :], preferred_element_type=jnp.float32)
        for c in range(ROW_CHUNKS):
            y_ref[pl.ds(c, rows, stride=ROW_CHUNKS), :] = y[:, c * LANES:(c + 1) * LANES]


def _experts(block_expert, n_used, xs, wg, bg, wu, bu, wd, bd, *, layer):
    nb = block_expert.shape[0]
    rows = EXPERT_ROWS
    depth = wg.shape[0]
    blk = lambda b, be, used: (jnp.minimum(b, used[0] - 1), 0)
    wsel = lambda b, be, used: (layer, be[b], 0, 0)
    wspec = pl.BlockSpec((1, 1, D_MODEL, D_MODEL), wsel)
    bspec = pl.BlockSpec((1, 1, 1, D_MODEL), wsel)
    bias4 = lambda a: a.reshape(depth, N_EXPERTS, 1, D_MODEL)
    return pl.pallas_call(
        functools.partial(_expert_kernel, rows=rows),
        out_shape=jax.ShapeDtypeStruct(xs.shape, jnp.float32),
        grid_spec=pltpu.PrefetchScalarGridSpec(
            num_scalar_prefetch=2,
            grid=(nb,),
            in_specs=[pl.BlockSpec((rows * ROW_CHUNKS, LANES), blk),
                      wspec, bspec, wspec, bspec, wspec, bspec],
            out_specs=pl.BlockSpec((rows * ROW_CHUNKS, LANES), lambda b, be, used: (b, 0)),
            scratch_shapes=[pltpu.VMEM((D_MODEL, D_MODEL), jnp.bfloat16)] * 3),
        compiler_params=pltpu.CompilerParams(dimension_semantics=("arbitrary",),
                                             vmem_limit_bytes=VMEM_LIMIT),
        name="moe_experts",
    )(block_expert, n_used, xs, wg, bias4(bg), wu, bias4(bu), wd, bias4(bd))


def _combine_kernel(cstart_ref, clen_ref, pchunk_ref, gbound_ref, ys_hbm, x_ref, dest_ref, gate_ref, gn_ref,
                    *rest, tile, final):
    if final:
        o_ref, zbuf_ref, acc_ref, destb_ref, gateb_ref, sem = rest
    else:
        wkp_ref, wqvt_ref, o_ref, kp_ref, qvt_ref, zbuf_ref, acc_ref, destb_ref, gateb_ref, sem = rest
    i = pl.program_id(0)
    n_tiles = pl.num_programs(0)
    slab_rows = SLAB_CHUNKS * COMBINE_CHUNK
    chunk_flat = COMBINE_CHUNK * ROW_CHUNKS

    def pair_copies(first_chunk, ring):
        return [pltpu.make_async_copy(
            ys_hbm.at[pl.ds(pl.multiple_of(cstart_ref[first_chunk + q] * ROW_CHUNKS, ROW_CHUNKS), chunk_flat), :],
            zbuf_ref.at[ring, q // SLAB_CHUNKS, pl.ds((q % SLAB_CHUNKS) * chunk_flat, chunk_flat), :],
            sem.at[ring]) for q in range(2 * SLAB_CHUNKS)]

    g0 = gbound_ref[i]
    n_pairs = gbound_ref[i + 1] - g0
    g_total = gbound_ref[n_tiles]

    @pl.when(i == 0)
    def _():
        for cp in pair_copies(pchunk_ref[0], 0):
            cp.start()

        @pl.when(g_total > 1)
        def _():
            for cp in pair_copies(pchunk_ref[1], 1):
                cp.start()

    acc_ref[...] = x_ref[...]
    for j in range(TOP_K):
        destb_ref[j] = jnp.broadcast_to(dest_ref[:, j:j + 1], (tile, LANES))
        gateb_ref[j] = jnp.broadcast_to(gate_ref[:, j:j + 1], (tile, LANES))
    col = lax.broadcasted_iota(jnp.int32, (1, slab_rows), 1)
    col_chunk = col // COMBINE_CHUNK
    col_row = col % COMBINE_CHUNK

    def gate_matrix(first):
        rows = jnp.full((1, slab_rows), -1, jnp.int32)
        for q in range(SLAB_CHUNKS):
            ok = jnp.logical_and(col_chunk == q, col_row < clen_ref[first + q])
            rows = jnp.where(ok, cstart_ref[first + q] + col_row, rows)
        g_cols = []
        for c0 in range(0, slab_rows, LANES):
            rows_c = rows[:, c0:c0 + LANES]
            g_c = jnp.zeros((tile, LANES), jnp.float32)
            for j in range(TOP_K):
                g_c = jnp.where(destb_ref[j] == rows_c, gateb_ref[j], g_c)
            g_cols.append(g_c.astype(jnp.bfloat16))
        return jnp.concatenate(g_cols, axis=1)

    def pair(p, carry):
        g = g0 + p
        ring = g % 2
        first = pchunk_ref[g]
        for cp in pair_copies(first, ring):
            cp.wait()
        ys_pair = [jnp.concatenate([zbuf_ref[ring, slot, pl.ds(c, slab_rows, stride=ROW_CHUNKS), :]
                                    for c in range(ROW_CHUNKS)], axis=1).astype(jnp.bfloat16)
                   for slot in range(2)]

        @pl.when(g + 2 < g_total)
        def _():
            for cp in pair_copies(pchunk_ref[g + 2], ring):
                cp.start()

        for slot in range(2):
            acc_ref[...] += jnp.dot(gate_matrix(first + slot * SLAB_CHUNKS), ys_pair[slot],
                                    preferred_element_type=jnp.float32)
        return carry

    lax.fori_loop(0, n_pairs, pair, 0)
    x2 = acc_ref[...]
    if final:
        o_ref[...] = _rms(x2, gn_ref[...])
    else:
        o_ref[...] = x2
        h = _rms(x2, gn_ref[...]).astype(jnp.bfloat16)
        kp_ref[...] = jnp.dot(h, wkp_ref[...], preferred_element_type=jnp.float32).astype(kp_ref.dtype)
        qvt_ref[...] = lax.dot_general(wqvt_ref[...], h, _NT,
                                       preferred_element_type=jnp.float32).astype(qvt_ref.dtype)


def _combine(chunk_start, chunk_len, pair_chunk, pair_bound, ys, x1, dest_tok, gates_tok, g_norm, next_proj=None):
    t = TOK_TILE
    n = x1.shape[0]
    final = next_proj is None
    tok = lambda width: pl.BlockSpec((t, width), lambda i, *_: (i, 0))
    whole = lambda a: pl.BlockSpec(a.shape, lambda i, *_: (0,) * a.ndim)
    width, width_t = QK_WIDTH + POOL_WIDTH, QK_WIDTH + ATT_WIDTH
    out_shape = jax.ShapeDtypeStruct((n, D_MODEL), jnp.float32)
    out_specs = tok(D_MODEL)
    extra = ()
    if not final:
        extra = tuple(next_proj)
        out_shape = (out_shape, jax.ShapeDtypeStruct((n, width), jnp.bfloat16),
                     jax.ShapeDtypeStruct((width_t, n), jnp.bfloat16))
        out_specs = (out_specs, tok(width), pl.BlockSpec((width_t, t), lambda i, *_: (0, i)))
    return pl.pallas_call(
        functools.partial(_combine_kernel, tile=t, final=final),
        out_shape=out_shape,
        grid_spec=pltpu.PrefetchScalarGridSpec(
            num_scalar_prefetch=4,
            grid=(n // t,),
            in_specs=[pl.BlockSpec(memory_space=pl.ANY),
                      tok(D_MODEL), tok(TOP_K), tok(TOP_K),
                      pl.BlockSpec((1, D_MODEL), lambda i, *_: (0, 0))] + [whole(a) for a in extra],
            out_specs=out_specs,
            scratch_shapes=[pltpu.VMEM((2, 2, SLAB_CHUNKS * COMBINE_CHUNK * ROW_CHUNKS, LANES), jnp.float32),
                            pltpu.VMEM((t, D_MODEL), jnp.float32),
                            pltpu.VMEM((TOP_K, t, LANES), jnp.int32),
                            pltpu.VMEM((TOP_K, t, LANES), jnp.float32),
                            pltpu.SemaphoreType.DMA((2,))]),
        compiler_params=pltpu.CompilerParams(dimension_semantics=("arbitrary",),
                                             vmem_limit_bytes=VMEM_LIMIT),
        name="moe_combine",
    )(chunk_start, chunk_len, pair_chunk, pair_bound, ys, x1, dest_tok, gates_tok, g_norm.reshape(1, -1), *extra)


def _combine_chunks(base, cnt, seg_start):
    n_run = jnp.concatenate([base[1:], cnt[None]], axis=0) - base
    run_start = seg_start[None, :] + base
    n_chunks = (n_run + COMBINE_CHUNK - 1) // COMBINE_CHUNK
    cum = jnp.cumsum(n_chunks, axis=1)
    total = cum[:, -1]
    slots = jnp.arange(MAX_CHUNKS, dtype=jnp.int32)
    expert = jnp.minimum(jnp.sum(cum[:, None, :] <= slots[None, :, None], axis=2), N_EXPERTS - 1)
    hot = expert[..., None] == jnp.arange(N_EXPERTS, dtype=jnp.int32)
    pick = lambda a: jnp.sum(jnp.where(hot, a[:, None, :], 0), axis=2)
    k = slots[None, :] - (pick(cum) - pick(n_chunks))
    valid = slots[None, :] < total[:, None]
    start = jnp.where(valid, pick(run_start) + COMBINE_CHUNK * k, 0)
    length = jnp.where(valid, jnp.clip(pick(n_run) - COMBINE_CHUNK * k, 0, COMBINE_CHUNK), 0)
    pair_size = 2 * SLAB_CHUNKS
    n_tiles = base.shape[0]
    n_pairs = (total + pair_size - 1) // pair_size
    pair_end = jnp.cumsum(n_pairs)
    pair_bound = jnp.concatenate([jnp.zeros((1,), pair_end.dtype), pair_end])
    g = jnp.arange(n_tiles * (MAX_CHUNKS // pair_size), dtype=jnp.int32)
    tile_of = jnp.minimum(jnp.sum(pair_end[None, :] <= g[:, None], axis=1), n_tiles - 1)
    tile_first = jnp.sum(jnp.where(tile_of[:, None] == jnp.arange(n_tiles), (pair_end - n_pairs)[None, :], 0), axis=1)
    pair_chunk = tile_of * MAX_CHUNKS + pair_size * (g - tile_first)
    i32 = lambda a: a.reshape(-1).astype(jnp.int32)
    return i32(start), i32(length), i32(pair_chunk), i32(pair_bound)


def kernel(x, rel_table, norm1, w_in, lambda_q1, lambda_k1, lambda_q2, lambda_k2, subln_g, w_pool, pool_scale,
           w_out, norm2, w_router, b_router, w_gate, b_gate, w_up, b_up, w_down, b_down, final_norm):
    batch, seq, d = x.shape
    depth = norm1.shape[0]
    n = batch * seq
    assert d == D_MODEL and seq % ATT_TILE == 0 and seq % TOK_TILE == 0
    assert n % DISPATCH_TILE == 0
    bf = jnp.bfloat16

    w_q = w_in[:, :, :QK_WIDTH] * (QK_DIM ** -0.5 * LOG2E)
    w_k = w_in[:, :, QK_WIDTH:2 * QK_WIDTH]
    w_v = w_in[:, :, 2 * QK_WIDTH:2 * QK_WIDTH + ATT_WIDTH]
    w_p = w_in[:, :, 2 * QK_WIDTH + ATT_WIDTH:]
    w_kp = jnp.concatenate([w_k, w_p], axis=2).astype(bf)
    w_qvt = jnp.swapaxes(jnp.concatenate([w_q, w_v], axis=2), 1, 2).astype(bf)
    w_out_b = w_out.astype(bf)
    w_pool_b = w_pool.astype(bf)
    wr_t = jnp.swapaxes(w_router, 1, 2).astype(bf)
    tri = (lax.broadcasted_iota(jnp.int32, (TOK_TILE, TOK_TILE), 0)
           < lax.broadcasted_iota(jnp.int32, (TOK_TILE, TOK_TILE), 1)).astype(bf)

    bias = _bias_tiles(rel_table, ATT_TILE)
    far = jnp.stack([rel_table[N_BUCKETS // 2 - 1], rel_table[N_BUCKETS - 1]], axis=1).reshape(-1)

    n_assign = n * TOP_K
    n_blocks = -(-(n_assign + N_EXPERTS * (EXPERT_ROWS - 1)) // EXPERT_ROWS) + 1
    n_rows = n_blocks * EXPERT_ROWS
    block_start = jnp.arange(n_blocks, dtype=jnp.int32) * EXPERT_ROWS

    x2d = x.reshape(n, d)
    kp, qvt = _norm_proj(x2d, norm1[0], w_kp[0], w_qvt[0])
    for l in range(depth):
        lam_init = 0.8 - 0.6 * math.exp(-0.3 * l)
        att = _attention(kp, qvt, bias, far, lambda_q1[l], lambda_k1[l], lambda_q2[l], lambda_k2[l], subln_g[l],
                         batch=batch, seq=seq, lam_init=lam_init)
        x1, hflat, topi, rank, gates, base, counts = _mix_route(
            att, kp, x2d, w_pool_b[l], pool_scale[l], w_out_b[l], norm2[l], wr_t[l], b_router[l], tri, seq=seq)

        cnt = counts[:, 0].astype(jnp.int32)
        padded = ((cnt + EXPERT_ROWS - 1) // EXPERT_ROWS) * EXPERT_ROWS
        pad_end = jnp.cumsum(padded)
        experts = jnp.arange(N_EXPERTS, dtype=jnp.int32)
        seg_start = pad_end - padded
        dest_tok = (jnp.sum(jnp.where(topi[..., None] == experts, seg_start, 0), axis=-1) + rank).T
        dest_flat = dest_tok.reshape(-1)
        block_expert = jnp.minimum(jnp.sum(pad_end[None, :] <= block_start[:, None], axis=1),
                                   N_EXPERTS - 1).astype(jnp.int32)
        n_used = (pad_end[-1:] // EXPERT_ROWS).astype(jnp.int32)

        pad_lo = jnp.concatenate([pad_end - padded + cnt, pad_end[-1:]])
        pad_n = jnp.concatenate([padded - cnt, (n_rows - pad_end[-1:]) // EXPERT_ROWS])
        xs = _dispatch(pad_lo, pad_n, dest_flat, hflat, n_rows)
        ys = _experts(block_expert, n_used, xs, w_gate, b_gate, w_up, b_up, w_down, b_down, layer=l)
        chunk_lists = _combine_chunks(base[:, :, 0].astype(jnp.int32), cnt, seg_start)
        if l == depth - 1:
            x2d = _combine(*chunk_lists, ys, x1, dest_tok, gates.T, final_norm)
        else:
            x2d, kp, qvt = _combine(*chunk_lists, ys, x1, dest_tok, gates.T, norm1[l + 1],
                                    next_proj=(w_kp[l + 1], w_qvt[l + 1]))
    return x2d.reshape(batch, seq, d)
```
